```python
import math
import jax, jax.numpy as jnp
from jax import lax
import numpy as np

D_MODEL = 1024
BATCH = 8
SEQ = 2048
DEPTH = 4
DEC_BATCH = 128
DEC_SEQ = 1
PAST_LEN = 2048
PAGE_SIZE = 128

HEAD_DIM = 64
NSA_HEADS = 8
N_KV = 2
GQA = NSA_HEADS // N_KV
N_BRANCH = 3
CMP_LEN = 32
CMP_STRIDE = 16
CMP_R = CMP_LEN // CMP_STRIDE
CMP_HIDDEN = 2 * HEAD_DIM
SEL_BLOCK = 64
N_SEL = 16
SEL_Q_BLOCK = 64
WINDOW = 512
WIN_Q_BLOCK = 128
ATT_SCALE = HEAD_DIM ** -0.5
REL_BUCKETS = 32
REL_MAX_DIST = 128
GLA_HEADS = 4
GLA_DK = 32
GLA_DV = 64
GLA_RANK = 16
GLA_TAU = 16.0
GLA_CHUNK = 64
POOL_WINDOWS = (2, 4, 8, 16)
POOL_GC = 64
POOL_WIDTH = 4 * POOL_GC
POOL_BUF = 15
D_FF = 11 * D_MODEL // 4
FFN_CONV = 3

EPS = 1e-6
NEG = -1e30
BIG = 1e9
NSA_WIDTH = NSA_HEADS * HEAD_DIM
GLA_WIDTH = GLA_HEADS * GLA_DV
MIX_WIDTH = NSA_WIDTH + GLA_WIDTH + POOL_WIDTH
PROJ_SIZES = (NSA_WIDTH, N_BRANCH * 2 * N_KV * HEAD_DIM, NSA_HEADS * N_BRANCH, GLA_HEADS * GLA_DK, GLA_HEADS * GLA_DK, GLA_WIDTH, GLA_WIDTH, GLA_RANK, POOL_WIDTH)
PROJ_WIDTH = 2344

kernel_name = 'nsa_gla_pool_hybrid_step'


def _rmsnorm(x, g):
    x32 = x.astype(jnp.float32)
    y = x32 * lax.rsqrt(jnp.mean(x32 * x32, axis=-1, keepdims=True) + EPS)
    return (y * g.astype(jnp.float32)).astype(x.dtype)


def _rel_bucket(d):
    d = jnp.maximum(d, 0)
    exact = REL_BUCKETS // 2
    lg = jnp.log(jnp.maximum(d, 1).astype(jnp.float32) / exact) / math.log(REL_MAX_DIST / exact)
    large = jnp.minimum(exact + (lg * (REL_BUCKETS - exact)).astype(jnp.int32), REL_BUCKETS - 1)
    return jnp.where(d < exact, d, large)


def _masked_attn(q, q_pos, k, v, k_pos, rel_bias, window):
    d = q_pos[:, None] - k_pos[None, :]
    mask = (d >= 0) & (k_pos[None, :] >= 0)
    if window is not None:
        mask = mask & (d < window)
    bias = rel_bias[_rel_bucket(d)].astype(jnp.float32)
    bias = bias.reshape(d.shape + (N_KV, GQA)).transpose(2, 3, 0, 1)
    s = jnp.einsum('bqhgd,bkhd->bhgqk', q, k).astype(jnp.float32) * ATT_SCALE + bias
    p = jax.nn.softmax(jnp.where(mask, s, NEG), axis=-1) * mask
    o = jnp.einsum('bhgqk,bkhd->bqhgd', p.astype(v.dtype), v)
    return o, p


def _compress(x, pe, w1, w2):
    B, Tk = x.shape[:2]
    nch = Tk // CMP_STRIDE
    nc = nch - CMP_R + 1
    xc = x[:, :nch * CMP_STRIDE].reshape(B, nch, CMP_STRIDE, N_KV, HEAD_DIM)
    w1r = w1.reshape(CMP_R, CMP_STRIDE, HEAD_DIM, CMP_HIDDEN)
    per = pe.reshape(CMP_R, CMP_STRIDE, HEAD_DIM)
    h = jnp.einsum('rsd,rsde->e', per, w1r)
    for r in range(CMP_R):
        h = h + jnp.einsum('bnshd,sde->bnhe', xc[:, r:r + nc], w1r[r])
    return jnp.einsum('bnhe,ed->bnhd', jax.nn.gelu(h), w2)


def _compressed_branch(q, q_pos, k_raw, v_raw, pe, w1, w2, k_gain, rel_bias):
    kc = _rmsnorm(_compress(k_raw, pe[0], w1[0], w2[0]), k_gain)
    vc = _compress(v_raw, pe[1], w1[1], w2[1])
    nc = kc.shape[1]
    k_pos = jnp.arange(nc, dtype=jnp.int32) * CMP_STRIDE + (CMP_LEN - 1)
    return _masked_attn(q, q_pos, kc, vc, k_pos, rel_bias, None)


def _sel_block(q, q_pos, idx, kb, vb, rel_bias):
    take = jax.vmap(jax.vmap(lambda t, i: t[i]))
    kg = take(kb, idx)
    vg = take(vb, idx)
    kpos = idx[..., None] * SEL_BLOCK + jnp.arange(SEL_BLOCK, dtype=jnp.int32)
    d = q_pos[None, None, :, None, None] - kpos
    mask = (d >= 0)[:, :, None]
    tbl = rel_bias.reshape(REL_BUCKETS, N_KV, GQA).transpose(1, 0, 2)
    bias = jax.vmap(lambda t, b: t[b], in_axes=(0, 1), out_axes=1)(tbl, _rel_bucket(d))
    bias = jnp.moveaxis(bias, -1, 2).astype(jnp.float32)
    s = jnp.einsum('bqhgd,bhqjsd->bhgqjs', q, kg).astype(jnp.float32) * ATT_SCALE + bias
    b_, h_, g_, q_, s_, sb_ = s.shape
    s = jnp.where(mask, s, NEG).reshape(b_, h_, g_, q_, s_ * sb_)
    p = jax.nn.softmax(s, axis=-1).reshape(b_, h_, g_, q_, s_, sb_).astype(vg.dtype)
    return jnp.einsum('bhgqjs,bhqjsd->bqhgd', p, vg)


def _selected_branch(q, q_pos, k, v, p_cmp, rel_bias):
    B, Tk = k.shape[:2]
    Tq = q.shape[1]
    ns = -(-Tk // SEL_BLOCK)

    def blocks(a):
        a = jnp.pad(a, ((0, 0), (0, ns * SEL_BLOCK - Tk), (0, 0), (0, 0)))
        return a.reshape(B, ns, SEL_BLOCK, N_KV, HEAD_DIM).transpose(0, 3, 1, 2, 4)

    kb, vb = blocks(k), blocks(v)
    nc = p_cmp.shape[-1]
    c_start = jnp.arange(nc) * CMP_STRIDE
    s_start = jnp.arange(ns) * SEL_BLOCK
    overlap = ((c_start[:, None] < s_start[None, :] + SEL_BLOCK) & (c_start[:, None] + CMP_LEN > s_start[None, :])).astype(jnp.float32)
    imp = jnp.einsum('bhgqc,cn->bhqn', p_cmp, overlap)
    blk = jnp.arange(ns, dtype=jnp.int32)[None, :]
    cur = (q_pos // SEL_BLOCK)[:, None]
    valid = blk <= cur
    forced = (blk == 0) | (blk == cur) | (blk == cur - 1)
    score = jnp.where(forced, BIG, jnp.where(valid, imp, -BIG))
    _, idx = lax.top_k(score, min(N_SEL, ns))
    qb = min(SEL_Q_BLOCK, Tq)
    nq = -(-Tq // qb)
    pad = nq * qb - Tq
    qs = jnp.pad(q, ((0, 0), (0, pad), (0, 0), (0, 0), (0, 0))).reshape(B, nq, qb, N_KV, GQA, HEAD_DIM).swapaxes(0, 1)
    ps = jnp.pad(q_pos, (0, pad), mode='edge').reshape(nq, qb)
    ids = jnp.pad(idx, ((0, 0), (0, 0), (0, pad), (0, 0)), mode='edge').reshape(B, N_KV, nq, qb, -1).transpose(2, 0, 1, 3, 4)
    o = lax.map(lambda a: _sel_block(a[0], a[1], a[2], kb, vb, rel_bias), (qs, ps, ids))
    return o.swapaxes(0, 1).reshape(B, nq * qb, N_KV, GQA, HEAD_DIM)[:, :Tq]


def _window_banded(q, k, v, rel_bias):
    B, T = q.shape[:2]
    nb = T // WIN_Q_BLOCK
    r = WINDOW // WIN_Q_BLOCK

    def band(a):
        ap = jnp.pad(a, ((0, 0), (WINDOW, 0), (0, 0), (0, 0))).reshape(B, nb + r, WIN_Q_BLOCK, N_KV, HEAD_DIM)
        return jnp.concatenate([ap[:, i:i + nb] for i in range(r + 1)], axis=2)

    kb, vb = band(k), band(v)
    qb = q.reshape(B, nb, WIN_Q_BLOCK, N_KV, GQA, HEAD_DIM)
    starts = jnp.arange(nb, dtype=jnp.int32) * WIN_Q_BLOCK
    qpos = starts[:, None] + jnp.arange(WIN_Q_BLOCK, dtype=jnp.int32)[None, :]
    kpos = starts[:, None] - WINDOW + jnp.arange(WINDOW + WIN_Q_BLOCK, dtype=jnp.int32)[None, :]
    attn = lambda qq, qp, kk, vv, kp: _masked_attn(qq, qp, kk, vv, kp, rel_bias, WINDOW)[0]
    o = jax.vmap(attn, in_axes=(1, 0, 1, 1, 0), out_axes=1)(qb, qpos, kb, vb, kpos)
    return o.reshape(B, T, N_KV, GQA, HEAD_DIM)


def _gla(q, k, v, g, s0):
    B, L, H, _ = q.shape
    C = min(GLA_CHUNK, L)
    nc = -(-L // C)
    pad = nc * C - L

    def prep(a):
        a = jnp.pad(a.astype(jnp.float32), ((0, 0), (0, pad), (0, 0), (0, 0)))
        return a.reshape(B, nc, C, H, a.shape[-1]).transpose(1, 0, 3, 2, 4)

    qc, kc, vc, gc = prep(q * GLA_DK ** -0.5), prep(k), prep(v), prep(g)
    tril = jnp.tril(jnp.ones((C, C), dtype=bool))

    def step(S, inp):
        qi, ki, vi, gi = inp
        b = jnp.cumsum(gi, axis=2)
        qd = qi * jnp.exp(b)
        kd = ki * jnp.exp(-b)
        a = jnp.where(tril, jnp.einsum('bhid,bhjd->bhij', qd, kd), 0.0)
        o = jnp.einsum('bhid,bhde->bhie', qd, S) + jnp.einsum('bhij,bhje->bhie', a, vi)
        bl = b[:, :, -1:, :]
        S = jnp.exp(bl[:, :, 0, :, None]) * S + jnp.einsum('bhjd,bhje->bhde', ki * jnp.exp(bl - b), vi)
        return S, o

    S, o = lax.scan(step, s0.astype(jnp.float32), (qc, kc, vc, gc))
    o = o.transpose(1, 0, 3, 2, 4).reshape(B, nc * C, H, GLA_DV)[:, :L]
    return o, S


def _pool_mix(u, buf, pos0, w, scale):
    B, L, _ = u.shape
    P = buf.shape[1]
    ext = jnp.concatenate([buf, u], axis=1)
    c = jnp.pad(jnp.cumsum(ext.astype(jnp.float32), axis=1), ((0, 0), (1, 0), (0, 0)))
    pos = pos0 + jnp.arange(L, dtype=jnp.int32)
    means = []
    for gi, win in enumerate(POOL_WINDOWS):
        sl = slice(gi * POOL_GC, (gi + 1) * POOL_GC)
        s = c[:, P + 1:P + 1 + L, sl] - c[:, P + 1 - win:P + 1 - win + L, sl]
        cnt = jnp.minimum(win, pos + 1).astype(jnp.float32)[None, :, None]
        means.append(s / cnt)
    dlt = (jnp.concatenate(means, axis=-1) - u.astype(jnp.float32)).reshape(B, L, len(POOL_WINDOWS), POOL_GC)
    y = jnp.einsum('blgc,gce->blge', dlt, w.astype(jnp.float32)).reshape(B, L, POOL_WIDTH) * scale.astype(jnp.float32)
    return y.astype(u.dtype), ext[:, -P:]


def _conv_ffn(h, buf, w_up, conv_w, conv_b, w_down):
    L = h.shape[1]
    a, u = jnp.split(h @ w_up, 2, axis=-1)
    ext = jnp.concatenate([buf, a], axis=1)
    ac = conv_b + ext[:, 0:L] * conv_w[0]
    for j in range(1, FFN_CONV):
        ac = ac + ext[:, j:j + L] * conv_w[j]
    return (jax.nn.silu(ac) * u) @ w_down, ext[:, -(FFN_CONV - 1):]


def _layer(x, pos0, past, lp, rel_bias):
    B, L, _ = x.shape
    dt = x.dtype
    q_pos = pos0 + jnp.arange(L, dtype=jnp.int32)
    h = _rmsnorm(x, lp['g_mix'])
    splits = [int(i) for i in np.cumsum(PROJ_SIZES)[:-1]]
    q, kv, gate, gq, gk, gv, gr, glr, pu = jnp.split(h @ lp['w_in'], splits, axis=-1)
    q = _rmsnorm(q.reshape(B, L, N_KV, GQA, HEAD_DIM), lp['q_gain'])
    kv = kv.reshape(B, L, N_BRANCH, 2, N_KV, HEAD_DIM)
    cmp_new = kv[:, :, 0]
    slc_new = jnp.stack([_rmsnorm(kv[:, :, 1, 0], lp['k_gain'][1]), kv[:, :, 1, 1]], axis=2)
    win_new = jnp.stack([_rmsnorm(kv[:, :, 2, 0], lp['k_gain'][2]), kv[:, :, 2, 1]], axis=2)
    if past is None:
        cmp_all, slc_all = cmp_new, slc_new
        o_win = _window_banded(q, win_new[:, :, 0], win_new[:, :, 1], rel_bias)
        win_state = win_new[:, -min(WINDOW, L):]
        gla_s0 = jnp.zeros((B, GLA_HEADS, GLA_DK, GLA_DV), jnp.float32)
        pool_buf = jnp.zeros((B, POOL_BUF, POOL_WIDTH), dt)
        conv_buf = jnp.zeros((B, FFN_CONV - 1, D_FF), dt)
    else:
        cmp_all = jnp.concatenate([past['cmp'], cmp_new], axis=1)
        slc_all = jnp.concatenate([past['slc'], slc_new], axis=1)
        wb = past['win'].shape[1]
        win_all = jnp.concatenate([past['win'], win_new], axis=1)
        k_pos = pos0 - wb + jnp.arange(wb + L, dtype=jnp.int32)
        o_win = _masked_attn(q, q_pos, win_all[:, :, 0], win_all[:, :, 1], k_pos, rel_bias, WINDOW)[0]
        win_state = win_all[:, -wb:]
        gla_s0, pool_buf, conv_buf = past['gla'], past['pool'], past['conv']
    o_cmp, p_cmp = _compressed_branch(q, q_pos, cmp_all[:, :, 0], cmp_all[:, :, 1], lp['cmp_pe'], lp['cmp_w1'], lp['cmp_w2'], lp['k_gain'][0], rel_bias)
    o_slc = _selected_branch(q, q_pos, slc_all[:, :, 0], slc_all[:, :, 1], p_cmp, rel_bias)
    gt = jax.nn.sigmoid(gate.reshape(B, L, N_KV, GQA, N_BRANCH)).astype(dt)
    o_nsa = (gt[..., 0:1] * o_cmp + gt[..., 1:2] * o_slc + gt[..., 2:3] * o_win).reshape(B, L, NSA_WIDTH)
    z = (glr @ lp['gla_w_alpha'] + lp['gla_b_alpha']).astype(jnp.float32)
    g_log = jax.nn.log_sigmoid(z) / GLA_TAU
    shk = (B, L, GLA_HEADS, GLA_DK)
    o_gla, gla_state = _gla(gq.reshape(shk), gk.reshape(shk), gv.reshape(B, L, GLA_HEADS, GLA_DV), g_log.reshape(shk), gla_s0)
    o_gla = _rmsnorm(o_gla, lp['gla_norm'].reshape(GLA_HEADS, GLA_DV)).reshape(B, L, GLA_WIDTH) * jax.nn.silu(gr.astype(jnp.float32))
    o_pool, pool_state = _pool_mix(pu, pool_buf, pos0, lp['pool_w'], lp['pool_scale'])
    mix = jnp.concatenate([o_nsa.astype(dt), o_gla.astype(dt), o_pool], axis=-1)
    x = x + mix @ lp['w_out']
    y, conv_state = _conv_ffn(_rmsnorm(x, lp['g_ffn']), conv_buf, lp['w_ffn_up'], lp['ffn_conv_w'], lp['ffn_conv_b'], lp['w_ffn_down'])
    x = x + y
    return x, (cmp_new, slc_new, win_state, gla_state.astype(dt), pool_state, conv_state)


def setup_inputs(seed: int = 0) -> dict:
    key = jax.random.key(seed)
    ks = jax.random.split(key, 32)
    f32 = jnp.float32
    nrm = lambda k, shape, scale: jax.random.normal(k, shape, f32) * scale
    n_pages = PAST_LEN // PAGE_SIZE
    used = DEC_BATCH * n_pages
    n_phys = used + max(1, used // 4)
    perm = jax.random.permutation(ks[0], n_phys)
    page_table = perm[:used].reshape(DEC_BATCH, n_pages).astype(jnp.int32)
    win_buf = min(WINDOW, PAST_LEN)
    return dict(
        x_prompt=nrm(ks[1], (BATCH, SEQ, D_MODEL), 1.0),
        x_sample=nrm(ks[2], (DEC_BATCH, DEC_SEQ, D_MODEL), 1.0),
        cache_cmp_kv=nrm(ks[3], (DEPTH, n_phys, PAGE_SIZE, 2, N_KV, HEAD_DIM), 1.0),
        cache_slc_kv=nrm(ks[4], (DEPTH, n_phys, PAGE_SIZE, 2, N_KV, HEAD_DIM), 1.0),
        page_table=page_table,
        state_win_kv=nrm(ks[5], (DEPTH, DEC_BATCH, win_buf, 2, N_KV, HEAD_DIM), 1.0),
        state_gla=nrm(ks[6], (DEPTH, DEC_BATCH, GLA_HEADS, GLA_DK, GLA_DV), 1.0),
        state_pool=nrm(ks[7], (DEPTH, DEC_BATCH, POOL_BUF, POOL_WIDTH), 1.0),
        state_ffn_conv=nrm(ks[8], (DEPTH, DEC_BATCH, FFN_CONV - 1, D_FF), 1.0),
        rel_bias=nrm(ks[9], (REL_BUCKETS, NSA_HEADS), 0.2),
        g_mix=1.0 + nrm(ks[10], (DEPTH, D_MODEL), 0.02),
        w_in=nrm(ks[11], (DEPTH, D_MODEL, PROJ_WIDTH), D_MODEL ** -0.5),
        q_gain=1.0 + nrm(ks[12], (DEPTH, HEAD_DIM), 0.02),
        k_gain=1.0 + nrm(ks[13], (DEPTH, N_BRANCH, HEAD_DIM), 0.02),
        cmp_pe=nrm(ks[14], (DEPTH, 2, CMP_LEN, HEAD_DIM), 0.1),
        cmp_w1=nrm(ks[15], (DEPTH, 2, CMP_LEN, HEAD_DIM, CMP_HIDDEN), (CMP_LEN * HEAD_DIM) ** -0.5),
        cmp_w2=nrm(ks[16], (DEPTH, 2, CMP_HIDDEN, HEAD_DIM), CMP_HIDDEN ** -0.5),
        gla_w_alpha=nrm(ks[17], (DEPTH, GLA_RANK, GLA_HEADS * GLA_DK), GLA_RANK ** -0.5),
        gla_b_alpha=nrm(ks[18], (DEPTH, GLA_HEADS * GLA_DK), 0.1),
        gla_norm=1.0 + nrm(ks[19], (DEPTH, GLA_WIDTH), 0.02),
        pool_w=nrm(ks[20], (DEPTH, len(POOL_WINDOWS), POOL_GC, POOL_GC), POOL_GC ** -0.5),
        pool_scale=1.0 + nrm(ks[21], (DEPTH, POOL_WIDTH), 0.02),
        w_out=nrm(ks[22], (DEPTH, MIX_WIDTH, D_MODEL), MIX_WIDTH ** -0.5),
        g_ffn=1.0 + nrm(ks[23], (DEPTH, D_MODEL), 0.02),
        w_ffn_up=nrm(ks[24], (DEPTH, D_MODEL, 2 * D_FF), D_MODEL ** -0.5),
        ffn_conv_w=nrm(ks[25], (DEPTH, FFN_CONV, D_FF), FFN_CONV ** -0.5),
        ffn_conv_b=nrm(ks[26], (DEPTH, D_FF), 0.02),
        w_ffn_down=nrm(ks[27], (DEPTH, D_FF, D_MODEL), D_FF ** -0.5),
    )


def reference(x_prompt, x_sample, cache_cmp_kv, cache_slc_kv, page_table, state_win_kv, state_gla, state_pool, state_ffn_conv,
              rel_bias, g_mix, w_in, q_gain, k_gain, cmp_pe, cmp_w1, cmp_w2, gla_w_alpha, gla_b_alpha, gla_norm,
              pool_w, pool_scale, w_out, g_ffn, w_ffn_up, ffn_conv_w, ffn_conv_b, w_ffn_down):
    db = x_sample.shape[0]
    past_len = page_table.shape[1] * cache_cmp_kv.shape[2]
    xp, xs = x_prompt, x_sample
    st_p = [[] for _ in range(6)]
    st_s = [[] for _ in range(6)]
    for l in range(DEPTH):
        lp = dict(g_mix=g_mix[l], w_in=w_in[l], q_gain=q_gain[l], k_gain=k_gain[l], cmp_pe=cmp_pe[l], cmp_w1=cmp_w1[l],
                  cmp_w2=cmp_w2[l], gla_w_alpha=gla_w_alpha[l], gla_b_alpha=gla_b_alpha[l], gla_norm=gla_norm[l],
                  pool_w=pool_w[l], pool_scale=pool_scale[l], w_out=w_out[l], g_ffn=g_ffn[l], w_ffn_up=w_ffn_up[l],
                  ffn_conv_w=ffn_conv_w[l], ffn_conv_b=ffn_conv_b[l], w_ffn_down=w_ffn_down[l])
        past = dict(
            cmp=cache_cmp_kv[l][page_table].reshape(db, past_len, 2, N_KV, HEAD_DIM),
            slc=cache_slc_kv[l][page_table].reshape(db, past_len, 2, N_KV, HEAD_DIM),
            win=state_win_kv[l], gla=state_gla[l], pool=state_pool[l], conv=state_ffn_conv[l])
        xp, new_p = _layer(xp, 0, None, lp, rel_bias)
        xs, new_s = _layer(xs, past_len, past, lp, rel_bias)
        for i in range(6):
            st_p[i].append(new_p[i])
            st_s[i].append(new_s[i])
    cmp_p, slc_p, win_p, gla_p, pool_p, conv_p = [jnp.stack(a, axis=0) for a in st_p]
    cmp_s, slc_s, win_s, gla_s, pool_s, conv_s = [jnp.stack(a, axis=0) for a in st_s]
    return (xp, xs, cmp_p, slc_p, win_p, gla_p, pool_p, conv_p, cmp_s, slc_s, win_s, gla_s, pool_s, conv_s)
```

```python
import functools
import math

import numpy as np
import jax
import jax.numpy as jnp
from jax import lax
from jax.experimental import pallas as pl
from jax.experimental.pallas import tpu as pltpu

f32 = jnp.float32
bf16 = jnp.bfloat16

D_MODEL = 1024
HEAD_DIM = 64
N_KV = 2
GQA = 4
N_HEADS = 8
CMP_STRIDE = 16
CMP_LEN = 32
N_CMP = 127
SEL_BLOCK = 64
N_SEL = 16
WINDOW = 512
ATT_SCALE = HEAD_DIM ** -0.5
REL_BUCKETS = 32
REL_MAX_DIST = 128
GLA_HEADS = 4
GLA_DK = 32
GLA_DV = 64
GLA_RANK = 16
GLA_TAU = 16.0
GLA_CHUNK = 64
POOL_WINDOWS = (2, 4, 8, 16)
POOL_BUF = 15
D_FF = 2816
EPS = 1e-6
NEG = -1e30
BIG = 1e9
PAGE = 128

LANES = 128
SUBLANES = 8
QT = 128
PROJ_W = 2560
VMEM_LIMIT = 56 * 1024 * 1024


def _cparams(sem):
    return pltpu.CompilerParams(dimension_semantics=sem, vmem_limit_bytes=VMEM_LIMIT)


def _dot(a, b):
    return jnp.dot(a, b, preferred_element_type=f32)


def _dot_nt(a, b):
    return lax.dot_general(a, b, (((1,), (1,)), ((), ())), preferred_element_type=f32)


def _dot_tn(a, b):
    return lax.dot_general(a, b, (((0,), (0,)), ((), ())), preferred_element_type=f32)


def _dot_hilo(a, b_bf16):
    hi = a.astype(bf16)
    lo = (a - hi.astype(f32)).astype(bf16)
    return _dot(hi, b_bf16) + _dot(lo, b_bf16)


def _pair_rmsnorm(y, gain):
    lane = lax.broadcasted_iota(jnp.int32, y.shape, y.ndim - 1)
    lo = lane < HEAD_DIM
    y2 = y * y
    s_lo = jnp.sum(jnp.where(lo, y2, 0.0), axis=-1, keepdims=True)
    s_hi = jnp.sum(jnp.where(lo, 0.0, y2), axis=-1, keepdims=True)
    ms = jnp.where(lo, s_lo, s_hi) * (1.0 / HEAD_DIM)
    return y * lax.rsqrt(ms + EPS) * gain


_C_Q, _C_CK, _C_CV, _C_SK, _C_SV, _C_WK, _C_WV = 0, 512, 640, 768, 896, 1024, 1152
_C_GATE, _C_GQ, _C_GK, _C_GV, _C_GR, _C_GLR, _C_PU = 1280, 1408, 1536, 1664, 1920, 2176, 2304


def _proj_kernel(x_ref, g_ref, w_ref, qg_ref, kg_ref, wa_ref, ba_ref,
                 q_o, ck_o, cv_o, sk_o, sv_o, wk_o, wv_o, gt_o, gq_o, gk_o, gv_o, gr_o, gl_o, pu_o):
    x = x_ref[...]
    hn = (x * lax.rsqrt(jnp.mean(x * x, axis=-1, keepdims=True) + EPS) * g_ref[...]).astype(bf16)

    def proj(c0, n):
        return _dot(hn, w_ref[:, c0:c0 + n])

    for j in range(4):
        q_o[:, j * LANES:(j + 1) * LANES] = _pair_rmsnorm(proj(_C_Q + j * LANES, LANES), qg_ref[...]).astype(bf16)
    ck_o[...] = proj(_C_CK, LANES)
    cv_o[...] = proj(_C_CV, LANES)
    sk_o[...] = _pair_rmsnorm(proj(_C_SK, LANES), kg_ref[1:2, :])
    sv_o[...] = proj(_C_SV, LANES)
    wk_o[...] = _pair_rmsnorm(proj(_C_WK, LANES), kg_ref[2:3, :])
    wv_o[...] = proj(_C_WV, LANES)
    gt_o[...] = jax.nn.sigmoid(proj(_C_GATE, LANES))
    gq_o[...] = proj(_C_GQ, LANES)
    gk_o[...] = proj(_C_GK, LANES)
    gv_o[...] = proj(_C_GV, 2 * LANES)
    gr_o[...] = proj(_C_GR, 2 * LANES)
    z = _dot(proj(_C_GLR, LANES).astype(bf16), wa_ref[...]) + ba_ref[...]
    gl_o[...] = jax.nn.log_sigmoid(z) * (1.0 / GLA_TAU)
    pu_o[...] = proj(_C_PU, 2 * LANES)


def _in_proj(x, lw, tm):
    m = x.shape[0]
    row = lambda n: pl.BlockSpec((tm, n), lambda i: (i, 0))
    full = lambda a: pl.BlockSpec(a.shape, lambda i: (0,) * a.ndim)
    widths = (512, 128, 128, 128, 128, 128, 128, 128, 128, 128, 256, 256, 128, 256)
    dtypes = (bf16,) + (f32,) * 13
    args = (x, lw['g_mix'], lw['w_in'], lw['q_gain'], lw['k_gain'], lw['w_alpha'], lw['b_alpha'])
    return pl.pallas_call(
        _proj_kernel,
        grid=(m // tm,),
        in_specs=[row(D_MODEL)] + [full(a) for a in args[1:]],
        out_specs=[row(n) for n in widths],
        out_shape=[jax.ShapeDtypeStruct((m, n), dt) for n, dt in zip(widths, dtypes)],
        compiler_params=_cparams(("parallel",)),
        name="in_proj",
    )(*args)


def _compress(get_x, w_ref, pe_ref, w2_ref):
    acc = jnp.zeros((LANES + 2 * SUBLANES, 4 * LANES), f32)
    for s in range(CMP_STRIDE):
        xs = jnp.concatenate([get_x(s), pe_ref[s]], axis=0).astype(bf16)
        acc = acc + _dot(xs, w_ref[s])
    a0 = acc[:LANES, :2 * LANES]
    a1 = acc[:LANES, 2 * LANES:]
    c0 = acc[LANES:LANES + 1, :2 * LANES] + acc[LANES + SUBLANES:LANES + SUBLANES + 1, 2 * LANES:]
    h = a0 + pltpu.roll(a1, LANES - 1, 0) + c0
    return _dot(jax.nn.gelu(h).astype(bf16), w2_ref[...])


def _compress_prompt_kernel(ck_ref, cv_ref, wk_ref, wv_ref, pek_ref, pev_ref, w2k_ref, w2v_ref, kg_ref, kc_o, vc_o):
    kx = lambda s: ck_ref[pl.ds(s, LANES, stride=CMP_STRIDE), :]
    vx = lambda s: cv_ref[pl.ds(s, LANES, stride=CMP_STRIDE), :]
    kc_o[...] = _pair_rmsnorm(_compress(kx, wk_ref, pek_ref, w2k_ref), kg_ref[0:1, :])
    vc_o[...] = _compress(vx, wv_ref, pev_ref, w2v_ref)


def _compress_prompt(ck, cv, lw, nb, seq):
    full = lambda a: pl.BlockSpec(a.shape, lambda b: (0,) * a.ndim)
    seqblk = pl.BlockSpec((seq, LANES), lambda b: (b, 0))
    outblk = pl.BlockSpec((None, LANES, LANES), lambda b: (b, 0, 0))
    ws = (lw['cmp_wk'], lw['cmp_wv'], lw['cmp_pek'], lw['cmp_pev'], lw['cmp_w2k'], lw['cmp_w2v'], lw['k_gain'])
    return pl.pallas_call(
        _compress_prompt_kernel,
        grid=(nb,),
        in_specs=[seqblk, seqblk] + [full(a) for a in ws],
        out_specs=[outblk, outblk],
        out_shape=[jax.ShapeDtypeStruct((nb, LANES, LANES), f32)] * 2,
        compiler_params=_cparams(("parallel",)),
        name="compress_prompt",
    )(ck, cv, *ws)


def _rank_select(score, n_blocks):
    blk = lax.broadcasted_iota(jnp.int32, score.shape, 1)
    rank = jnp.zeros(score.shape, f32)
    for m in range(n_blocks):
        sm = score[:, m:m + 1]
        beats = (sm > score) | ((sm == score) & (m < blk))
        rank = rank + jnp.where(beats, 1.0, 0.0)
    return jnp.where(rank < float(N_SEL), 1.0, 0.0)


def _nsa_prompt_kernel(q_ref, kc_ref, vc_ref, sk_ref, sv_ref, wk_ref, wv_ref, gt_ref,
                       t0_ref, t1_ref, cf_ref, bc_ref, ov_ref, ee_ref, o_ref,
                       selx_ref, m_ref, l_ref, acc_ref):
    i = pl.program_id(1)
    rows4 = GQA * QT
    lane = lax.broadcasted_iota(jnp.int32, (QT, LANES), 1)
    row = lax.broadcasted_iota(jnp.int32, (QT, LANES), 0)
    lane4 = lax.broadcasted_iota(jnp.int32, (rows4, LANES), 1)
    row4 = lax.broadcasted_iota(jnp.int32, (rows4, LANES), 0) & (QT - 1)
    qpos = i * QT + row
    qpos4 = i * QT + row4
    gates = gt_ref[...]
    n_blk = 2048 // SEL_BLOCK
    outs = [[None, None] for _ in range(GQA)]

    def flash(q4, k_ref, v_ref, kt_lo, kt_hi, use_sel):
        m_ref[...] = jnp.full((rows4, 1), NEG, f32)
        l_ref[...] = jnp.zeros((rows4, 1), f32)
        acc_ref[...] = jnp.zeros((rows4, LANES), f32)

        def body(kt, carry):
            off = pl.multiple_of(kt * QT, QT)
            k = k_ref[pl.ds(off, QT), :].astype(bf16)
            v = v_ref[pl.ds(off, QT), :].astype(bf16)
            bias = jnp.where(kt == i, t0_ref[h], jnp.where(kt == i - 1, t1_ref[h], cf_ref[h]))
            s = _dot_nt(q4, k) * ATT_SCALE + bias
            dist = qpos - (off + lane)
            if use_sel:
                msk = (dist >= 0) & (selx_ref[:, pl.ds(off, QT)] > 0.5)
            else:
                msk = (dist >= 0) & (dist < WINDOW)
            s = jnp.where(msk[None], s.reshape(GQA, QT, LANES), NEG).reshape(rows4, LANES)
            m_old = m_ref[...]
            m_new = jnp.maximum(m_old, jnp.max(s, axis=-1, keepdims=True))
            alpha = jnp.exp(m_old - m_new)
            p = jnp.exp(s - m_new)
            l_ref[...] = alpha * l_ref[...] + jnp.sum(p, axis=-1, keepdims=True)
            acc_ref[...] = alpha * acc_ref[...] + _dot(p.astype(bf16), v)
            m_ref[...] = m_new
            return carry

        lax.fori_loop(kt_lo, kt_hi, body, 0)
        return acc_ref[...] / l_ref[...]

    for h in range(N_KV):
        half = (lane >= HEAD_DIM) if h else (lane < HEAD_DIM)
        q4 = jnp.concatenate(
            [jnp.where(half, q_ref[:, g * LANES:(g + 1) * LANES], jnp.zeros((), bf16)) for g in range(GQA)], axis=0)
        s = _dot_nt(q4, kc_ref[...].astype(bf16)) * ATT_SCALE + bc_ref[h]
        mskc = (lane4 * CMP_STRIDE + (CMP_LEN - 1) <= qpos4) & (lane4 < N_CMP)
        s = jnp.where(mskc, s, NEG)
        e = jnp.exp(s - jnp.max(s, axis=-1, keepdims=True))
        p = jnp.where(mskc, e / jnp.sum(e, axis=-1, keepdims=True), 0.0)
        o_cmp = _dot(p.astype(bf16), vc_ref[...].astype(bf16))
        psum = jnp.sum(p.reshape(GQA, QT, LANES), axis=0)
        imp = _dot_hilo(psum, ov_ref[...])
        cur = qpos >> 6
        forced = (lane == 0) | (lane == cur) | (lane == cur - 1)
        score = jnp.where(forced, BIG, jnp.where(lane <= cur, imp, -BIG))
        score = jnp.where(lane < n_blk, score, -3e38)
        sel = _rank_select(score, n_blk)
        selx_ref[...] = _dot(sel.astype(bf16), ee_ref[...])
        o_slc = flash(q4, sk_ref, sv_ref, 0, i + 1, True)
        o_win = flash(q4, wk_ref, wv_ref, jnp.maximum(i - WINDOW // QT, 0), i + 1, False)
        for g in range(GQA):
            c = (h * GQA + g) * 3
            sl = slice(g * QT, (g + 1) * QT)
            outs[g][h] = (gates[:, c:c + 1] * o_cmp[sl] + gates[:, c + 1:c + 2] * o_slc[sl]
                          + gates[:, c + 2:c + 3] * o_win[sl])
    for g in range(GQA):
        o_ref[:, g * LANES:(g + 1) * LANES] = jnp.where(lane < HEAD_DIM, outs[g][0], outs[g][1]).astype(bf16)


def _nsa_prompt(q, kc, vc, sk, sv, wk, wv, gt, tabs, nb, seq):
    nt = seq // QT
    tile = lambda n: pl.BlockSpec((QT, n), lambda b, i: (b * nt + i, 0))
    seqblk = pl.BlockSpec((seq, LANES), lambda b, i: (b, 0))
    cblk = pl.BlockSpec((None, LANES, LANES), lambda b, i: (b, 0, 0))
    full = lambda a: pl.BlockSpec(a.shape, lambda b, i: (0,) * a.ndim)
    t0, t1, cf, bc, ov, ee = tabs['t0'], tabs['t1'], tabs['cf'], tabs['bc'], tabs['ov'], tabs['ee']
    bcblk = pl.BlockSpec((N_KV, None, GQA * QT, LANES), lambda b, i: (0, i, 0, 0))
    return pl.pallas_call(
        _nsa_prompt_kernel,
        grid=(nb, nt),
        in_specs=[tile(512), cblk, cblk, seqblk, seqblk, seqblk, seqblk, tile(LANES),
                  full(t0), full(t1), full(cf), bcblk, full(ov), full(ee)],
        out_specs=tile(512),
        out_shape=jax.ShapeDtypeStruct((nb * seq, 512), bf16),
        scratch_shapes=[pltpu.VMEM((QT, seq), f32), pltpu.VMEM((GQA * QT, 1), f32),
                        pltpu.VMEM((GQA * QT, 1), f32), pltpu.VMEM((GQA * QT, LANES), f32)],
        compiler_params=_cparams(("parallel", "arbitrary")),
        name="nsa_prompt",
    )(q, kc, vc, sk, sv, wk, wv, gt, t0, t1, cf, bc, ov, ee)


N_PAGES = 16


def _nsa_sample_kernel(pt_ref, *refs):
    ckp = refs[0:N_PAGES]
    cvp = refs[N_PAGES:2 * N_PAGES]
    skp = refs[2 * N_PAGES:3 * N_PAGES]
    svp = refs[3 * N_PAGES:4 * N_PAGES]
    (win_ref, qr_ref, gr_ref, knew_ref, wcat_k, wcat_v, pek_ref, pev_ref, w2k_ref, w2v_ref, kg_ref,
     bc_ref, bs_ref, bw_ref, b0_ref, ov_ref, o_ref) = refs[4 * N_PAGES:]
    lane = lax.broadcasted_iota(jnp.int32, (SUBLANES, LANES), 1)

    def hist(pages):
        return lambda s: jnp.concatenate([pg[pl.ds(s, SUBLANES, stride=CMP_STRIDE), :] for pg in pages], axis=0)

    kc = _pair_rmsnorm(_compress(hist(ckp), wcat_k, pek_ref, w2k_ref), kg_ref[0:1, :])
    vc = _compress(hist(cvp), wcat_v, pev_ref, w2v_ref)
    qf = qr_ref[...]
    qb = qf.astype(bf16)
    gates = gr_ref[...]
    knew = knew_ref[...]
    hi_rows = lax.broadcasted_iota(jnp.int32, (SUBLANES, LANES), 0) >= GQA
    own = (lane >= HEAD_DIM) == hi_rows

    s = _dot_nt(qb, kc.astype(bf16)) * ATT_SCALE + bc_ref[...]
    mskc = lane < N_CMP
    s = jnp.where(mskc, s, NEG)
    e = jnp.exp(s - jnp.max(s, axis=-1, keepdims=True))
    p = jnp.where(mskc, e / jnp.sum(e, axis=-1, keepdims=True), 0.0)
    o_cmp = _dot(p.astype(bf16), vc.astype(bf16))
    psum = jnp.broadcast_to(jnp.sum(p.reshape(N_KV, GQA, LANES), axis=1, keepdims=True), (N_KV, GQA, LANES))
    imp = _dot_hilo(psum.reshape(SUBLANES, LANES), ov_ref[...])
    n_blk = N_PAGES * PAGE // SEL_BLOCK + 1
    cur = n_blk - 1
    forced = (lane == 0) | (lane == cur) | (lane == cur - 1)
    score = jnp.where(forced, BIG, imp)
    score = jnp.where(lane < n_blk, score, -3e38)
    sel = _rank_select(score, n_blk)

    def softmax_av(s_parts, s_new, v_parts, v_new):
        mx = s_new
        for sp in s_parts:
            mx = jnp.maximum(mx, jnp.max(sp, axis=-1, keepdims=True))
        p_new = jnp.exp(s_new - mx)
        den = p_new
        acc = p_new * v_new
        for sp, vp in zip(s_parts, v_parts):
            pp = jnp.exp(sp - mx)
            den = den + jnp.sum(pp, axis=-1, keepdims=True)
            acc = acc + _dot(pp.astype(bf16), vp.astype(bf16))
        return acc / den

    def new_row(r):
        return knew[r:r + 1, :]

    def score_new(krow):
        return jnp.sum(jnp.where(own, qf * krow, 0.0), axis=-1, keepdims=True) * ATT_SCALE + b0_ref[:, 0:1]

    s_parts = []
    for j in range(N_PAGES):
        sj = _dot_nt(qb, skp[j][...].astype(bf16)) * ATT_SCALE + bs_ref[:, j * PAGE:(j + 1) * PAGE]
        mj = jnp.where(lane < SEL_BLOCK, sel[:, 2 * j:2 * j + 1], sel[:, 2 * j + 1:2 * j + 2]) > 0.5
        s_parts.append(jnp.where(mj, sj, NEG))
    o_slc = softmax_av(s_parts, score_new(new_row(0)), [pg[...] for pg in svp], new_row(1))

    w_parts = []
    for j in range(WINDOW // PAGE):
        kw = win_ref[j * PAGE:(j + 1) * PAGE, 0:LANES]
        sj = _dot_nt(qb, kw.astype(bf16)) * ATT_SCALE + bw_ref[:, j * PAGE:(j + 1) * PAGE]
        if j == 0:
            sj = jnp.where(lane >= 1, sj, NEG)
        w_parts.append(sj)
    v_parts = [win_ref[j * PAGE:(j + 1) * PAGE, LANES:2 * LANES] for j in range(WINDOW // PAGE)]
    o_win = softmax_av(w_parts, score_new(new_row(2)), v_parts, new_row(3))

    o_ref[...] = gates[:, 0:1] * o_cmp + gates[:, 1:2] * o_slc + gates[:, 2:3] * o_win


def _nsa_sample(layer, page_table, cache_cmp, cache_slc, win_state, qrows, grows, knew, lw, tabs):
    nb = page_table.shape[0]

    def page(j, half):
        return pl.BlockSpec((None, None, PAGE, LANES), lambda b, pt: (layer, pt[b, j], 0, half))

    full = lambda a: pl.BlockSpec(a.shape, lambda b, pt: (0,) * a.ndim)
    per_b = pl.BlockSpec((None, SUBLANES, LANES), lambda b, pt: (b, 0, 0))
    consts = (lw['cmp_wk'], lw['cmp_wv'], lw['cmp_pek'], lw['cmp_pev'], lw['cmp_w2k'], lw['cmp_w2v'], lw['k_gain'],
              tabs['bc_s'], tabs['bs_s'], tabs['bw_s'], tabs['b0_s'], tabs['ov_s'])
    in_specs = ([page(j, 0) for j in range(N_PAGES)] + [page(j, 1) for j in range(N_PAGES)]
                + [page(j, 0) for j in range(N_PAGES)] + [page(j, 1) for j in range(N_PAGES)]
                + [pl.BlockSpec((None, None, WINDOW, 2 * LANES), lambda b, pt: (layer, b, 0, 0)), per_b, per_b, per_b]
                + [full(a) for a in consts])
    return pl.pallas_call(
        _nsa_sample_kernel,
        grid_spec=pltpu.PrefetchScalarGridSpec(
            num_scalar_prefetch=1, grid=(nb,), in_specs=in_specs, out_specs=per_b),
        out_shape=jax.ShapeDtypeStruct((nb, SUBLANES, LANES), f32),
        compiler_params=_cparams(("parallel",)),
        name="nsa_sample",
    )(page_table, *([cache_cmp] * (2 * N_PAGES)), *([cache_slc] * (2 * N_PAGES)), win_state, qrows, grows, knew, *consts)


def _gla_out(o, gn_ref, gr):
    o = jnp.concatenate([_pair_rmsnorm(o[:, :LANES], gn_ref[:, :LANES]),
                         _pair_rmsnorm(o[:, LANES:], gn_ref[:, LANES:])], axis=1)
    return o * jax.nn.silu(gr)


def _gla_prompt_kernel(gq_ref, gk_ref, gl_ref, gv_ref, gr_ref, gn_ref, o_ref, st_o, st_ref):
    c_rows = GLA_CHUNK
    lane = lax.broadcasted_iota(jnp.int32, (c_rows, LANES), 1)
    row = lax.broadcasted_iota(jnp.int32, (c_rows, LANES), 0)
    tril = (lax.broadcasted_iota(jnp.int32, (c_rows, c_rows), 0)
            >= lax.broadcasted_iota(jnp.int32, (c_rows, c_rows), 1))
    lane_v = lax.broadcasted_iota(jnp.int32, (c_rows, GLA_HEADS * GLA_DV), 1)
    srow = lax.broadcasted_iota(jnp.int32, (GLA_HEADS * GLA_DV, LANES), 0)
    scol = lax.broadcasted_iota(jnp.int32, (GLA_HEADS * GLA_DV, LANES), 1)
    diag = (srow >> 6) == (scol >> 5)
    st_ref[...] = jnp.zeros(st_ref.shape, f32)

    def body(c, carry):
        off = pl.multiple_of(c * c_rows, c_rows)
        q = gq_ref[pl.ds(off, c_rows), :] * (GLA_DK ** -0.5)
        k = gk_ref[pl.ds(off, c_rows), :]
        v = gv_ref[pl.ds(off, c_rows), :]
        b = gl_ref[pl.ds(off, c_rows), :]
        sh = 1
        while sh < c_rows:
            b = b + jnp.where(row >= sh, pltpu.roll(b, sh, 0), 0.0)
            sh *= 2
        qd = q * jnp.exp(b)
        kd = (k * jnp.exp(-b)).astype(bf16)
        bl = b[c_rows - 1:c_rows, :]
        kk = (k * jnp.exp(bl - b)).astype(bf16)
        st = st_ref[...]
        vb = v.astype(bf16)
        o = _dot_nt(qd.astype(bf16), st.astype(bf16))
        for hh in range(GLA_HEADS):
            qm = jnp.where((lane >> 5) == hh, qd, 0.0).astype(bf16)
            a = jnp.where(tril, _dot_nt(qm, kd), 0.0)
            o = o + jnp.where((lane_v >> 6) == hh, _dot(a.astype(bf16), vb), 0.0)
        st_ref[...] = jnp.exp(bl) * st + jnp.where(diag, _dot_tn(vb, kk), 0.0)
        o_ref[pl.ds(off, c_rows), :] = _gla_out(o, gn_ref, gr_ref[pl.ds(off, c_rows), :]).astype(bf16)
        return carry

    lax.fori_loop(0, gq_ref.shape[0] // c_rows, body, 0)
    st_o[...] = st_ref[...]


def _gla_prompt(gq, gk, gl, gv, gr, gn, nb, seq):
    blk = lambda n: pl.BlockSpec((seq, n), lambda b: (b, 0))
    return pl.pallas_call(
        _gla_prompt_kernel,
        grid=(nb,),
        in_specs=[blk(LANES), blk(LANES), blk(LANES), blk(2 * LANES), blk(2 * LANES),
                  pl.BlockSpec(gn.shape, lambda b: (0, 0))],
        out_specs=[blk(2 * LANES), pl.BlockSpec((None, GLA_HEADS * GLA_DV, LANES), lambda b: (b, 0, 0))],
        out_shape=[jax.ShapeDtypeStruct((nb * seq, 2 * LANES), bf16),
                   jax.ShapeDtypeStruct((nb, GLA_HEADS * GLA_DV, LANES), f32)],
        scratch_shapes=[pltpu.VMEM((GLA_HEADS * GLA_DV, LANES), f32)],
        compiler_params=_cparams(("parallel",)),
        name="gla_prompt",
    )(gq, gk, gl, gv, gr, gn)


def _gla_sample_kernel(cols_ref, v_ref, s0_ref, gr_ref, gn_ref, o_ref, s_o):
    nb = s0_ref.shape[0]
    c = cols_ref[...]
    q = c[:, :, 0:1] * (GLA_DK ** -0.5)
    k = c[:, :, 1:2]
    g = c[:, :, 2:3]
    eg = jnp.exp(g)
    qd = q * eg
    kd = k * jnp.exp(-g)
    s0 = s0_ref[...]
    v = v_ref[...]
    a = jnp.sum((qd * kd).reshape(nb, GLA_HEADS, GLA_DK, 1), axis=2)
    o = jnp.sum((qd * s0).reshape(nb, GLA_HEADS, GLA_DK, GLA_DV), axis=2) + a * v
    vexp = jnp.broadcast_to(v[:, :, None, :], (nb, GLA_HEADS, GLA_DK, GLA_DV)).reshape(nb, GLA_HEADS * GLA_DK, GLA_DV)
    s_o[...] = eg * s0 + k * vexp
    y = o * lax.rsqrt(jnp.mean(o * o, axis=-1, keepdims=True) + EPS) * gn_ref[...]
    o_ref[...] = y * jax.nn.silu(gr_ref[...])


def _gla_sample(cols, v, s0, gr, gn, bt=8):
    nb = s0.shape[0]
    blk = lambda a: pl.BlockSpec((bt,) + a.shape[1:], lambda i: (i,) + (0,) * (a.ndim - 1))
    return pl.pallas_call(
        _gla_sample_kernel,
        grid=(nb // bt,),
        in_specs=[blk(cols), blk(v), blk(s0), blk(gr), pl.BlockSpec(gn.shape, lambda i: (0, 0))],
        out_specs=[blk(v), blk(s0)],
        out_shape=[jax.ShapeDtypeStruct(v.shape, f32), jax.ShapeDtypeStruct(s0.shape, f32)],
        compiler_params=_cparams(("parallel",)),
        name="gla_sample",
    )(cols, v, s0, gr, gn)


def _pool_finish(sums, u, cnt_inv, w_ref, sc_ref):
    lane = lax.broadcasted_iota(jnp.int32, u.shape, 1)
    grp = lane >> 6
    s = jnp.where(grp == 0, sums[0], jnp.where(grp == 1, sums[1], jnp.where(grp == 2, sums[2], sums[3])))
    dlt = s * cnt_inv - u
    return _dot(dlt.astype(bf16), w_ref[...]) * sc_ref[...]


def _pool_prompt_kernel(u_ref, w_ref, sc_ref, o_ref):
    u = u_ref[...]
    row = lax.broadcasted_iota(jnp.int32, u.shape, 0)
    lane = lax.broadcasted_iota(jnp.int32, u.shape, 1)
    sums = []
    s = u
    sh = 1
    while sh < POOL_WINDOWS[-1]:
        s = s + jnp.where(row >= sh, pltpu.roll(s, sh, 0), 0.0)
        sums.append(s)
        sh *= 2
    grp = lane >> 6
    win = jnp.where(grp == 0, POOL_WINDOWS[0], jnp.where(grp == 1, POOL_WINDOWS[1],
                    jnp.where(grp == 2, POOL_WINDOWS[2], POOL_WINDOWS[3])))
    cnt = jnp.minimum(win, row + 1).astype(f32)
    o_ref[...] = _pool_finish(sums, u, 1.0 / cnt, w_ref, sc_ref).astype(bf16)


def _pool_prompt(pu, w, sc, nb, seq):
    blk = pl.BlockSpec((seq, 2 * LANES), lambda b: (b, 0))
    full = lambda a: pl.BlockSpec(a.shape, lambda b: (0,) * a.ndim)
    return pl.pallas_call(
        _pool_prompt_kernel,
        grid=(nb,),
        in_specs=[blk, full(w), full(sc)],
        out_specs=blk,
        out_shape=jax.ShapeDtypeStruct((nb * seq, 2 * LANES), bf16),
        compiler_params=_cparams(("parallel",)),
        name="pool_prompt",
    )(pu, w, sc)


def _pool_sample_kernel(u_ref, buf_ref, w_ref, sc_ref, o_ref):
    u = u_ref[...]
    lane = lax.broadcasted_iota(jnp.int32, u.shape, 1)
    sums = []
    s = u
    nxt = POOL_BUF - 1
    for win in POOL_WINDOWS:
        while POOL_BUF - nxt < win:
            s = s + buf_ref[nxt]
            nxt -= 1
        sums.append(s)
    grp = lane >> 6
    cnt_inv = jnp.where(grp == 0, 1.0 / POOL_WINDOWS[0], jnp.where(grp == 1, 1.0 / POOL_WINDOWS[1],
                        jnp.where(grp == 2, 1.0 / POOL_WINDOWS[2], 1.0 / POOL_WINDOWS[3])))
    o_ref[...] = _pool_finish(sums, u, cnt_inv, w_ref, sc_ref).astype(bf16)


def _pool_sample(pu, buf_t, w, sc):
    full = lambda a: pl.BlockSpec(a.shape, lambda i: (0,) * a.ndim)
    return pl.pallas_call(
        _pool_sample_kernel,
        grid=(1,),
        in_specs=[full(pu), full(buf_t), full(w), full(sc)],
        out_specs=full(pu),
        out_shape=jax.ShapeDtypeStruct(pu.shape, bf16),
        compiler_params=_cparams(("arbitrary",)),
        name="pool_sample",
    )(pu, buf_t, w, sc)


def _out_proj_kernel(a_ref, b_ref, c_ref, x_ref, w_ref, g_ref, x_o, h_o):
    y = (_dot(a_ref[...], w_ref[0:512, :]) + _dot(b_ref[...], w_ref[512:768, :])
         + _dot(c_ref[...], w_ref[768:1024, :]))
    x = x_ref[...] + y
    x_o[...] = x
    h_o[...] = (x * lax.rsqrt(jnp.mean(x * x, axis=-1, keepdims=True) + EPS) * g_ref[...]).astype(bf16)


def _out_proj(o_nsa, o_gla, o_pool, x, w, g, tm):
    m = x.shape[0]
    row = lambda n: pl.BlockSpec((tm, n), lambda i: (i, 0))
    full = lambda a: pl.BlockSpec(a.shape, lambda i: (0,) * a.ndim)
    return pl.pallas_call(
        _out_proj_kernel,
        grid=(m // tm,),
        in_specs=[row(512), row(256), row(256), row(D_MODEL), full(w), full(g)],
        out_specs=[row(D_MODEL), row(D_MODEL)],
        out_shape=[jax.ShapeDtypeStruct((m, D_MODEL), f32), jax.ShapeDtypeStruct((m, D_MODEL), bf16)],
        compiler_params=_cparams(("parallel",)),
        name="out_proj",
    )(o_nsa, o_gla, o_pool, x, w, g)


FF_CHUNK = 256


def _ffn_kernel(tiles_per_seq, decode, h_ref, p_ref, x_ref, wa_ref, wu_ref, cw_ref, cb_ref, wd_ref, x_o, a_o):
    hn = h_ref[...]
    tm = hn.shape[0]
    row = lax.broadcasted_iota(jnp.int32, (tm, FF_CHUNK), 0)
    if not decode:
        first = (pl.program_id(0) % tiles_per_seq) == 0
        halo = p_ref[...]
    acc = jnp.zeros((tm, D_MODEL), f32)
    for c in range(D_FF // FF_CHUNK):
        cs = slice(c * FF_CHUNK, (c + 1) * FF_CHUNK)
        a = _dot(hn, wa_ref[:, cs])
        u = _dot(hn, wu_ref[:, cs])
        if decode:
            a2 = p_ref[0, :, cs]
            a1 = p_ref[1, :, cs]
            a_o[:, cs] = a
        else:
            ah = jnp.where(first, 0.0, _dot(halo, wa_ref[:, cs]))
            p1 = ah[SUBLANES - 1:SUBLANES, :]
            p2 = ah[SUBLANES - 2:SUBLANES - 1, :]
            a1 = jnp.where(row == 0, p1, pltpu.roll(a, 1, 0))
            a2 = jnp.where(row == 0, p2, jnp.where(row == 1, p1, pltpu.roll(a, 2, 0)))
            a_o[:, cs] = a[tm - SUBLANES:, :]
        ac = cb_ref[:, cs] + a2 * cw_ref[0:1, cs] + a1 * cw_ref[1:2, cs] + a * cw_ref[2:3, cs]
        act = (jax.nn.silu(ac) * u).astype(bf16)
        acc = acc + _dot(act, wd_ref[cs, :])
    x_o[...] = x_ref[...] + acc


def _ffn(hn, prev, x, lw, tm, tiles_per_seq, decode):
    m = x.shape[0]
    nt = m // tm
    row = lambda n: pl.BlockSpec((tm, n), lambda i: (i, 0))
    const = lambda a: pl.BlockSpec(a.shape, lambda i: (0,) * a.ndim, pipeline_mode=pl.Buffered(1))
    if decode:
        prev_spec = pl.BlockSpec(prev.shape, lambda i: (0, 0, 0))
        a_spec = row(D_FF)
        a_shape = jax.ShapeDtypeStruct((m, D_FF), f32)
    else:
        per = tm // SUBLANES
        prev_spec = pl.BlockSpec((SUBLANES, D_MODEL), lambda i: (jnp.maximum(i * per - 1, 0), 0))
        a_spec = pl.BlockSpec((None, SUBLANES, D_FF), lambda i: (i, 0, 0))
        a_shape = jax.ShapeDtypeStruct((nt, SUBLANES, D_FF), f32)
    ws = (lw['w_up_a'], lw['w_up_u'], lw['conv_w'], lw['conv_b'], lw['w_down'])
    return pl.pallas_call(
        functools.partial(_ffn_kernel, tiles_per_seq, decode),
        grid=(nt,),
        in_specs=[row(D_MODEL), prev_spec, row(D_MODEL)] + [const(a) for a in ws],
        out_specs=[row(D_MODEL), a_spec],
        out_shape=[jax.ShapeDtypeStruct((m, D_MODEL), f32), a_shape],
        compiler_params=_cparams(("parallel",)),
        name="ffn_decode" if decode else "ffn_prompt",
    )(hn, prev, x, *ws)


def _rel_bucket_np(d):
    d = np.maximum(d, 0)
    exact = REL_BUCKETS // 2
    lg = np.log(np.maximum(d, 1).astype(np.float32) / np.float32(exact)) / np.float32(math.log(REL_MAX_DIST / exact))
    large = np.minimum(exact + (lg * np.float32(REL_BUCKETS - exact)).astype(np.int32), REL_BUCKETS - 1)
    return np.where(d < exact, d, large).astype(np.int32)


def _bias_tables(rel_bias, seq, past_len):
    nt = seq // QT
    r = np.arange(QT)
    hh = (np.arange(N_KV)[:, None, None] * GQA + np.arange(GQA)[None, :, None]) + 0 * r[None, None, :]
    hh = hh.reshape(N_KV, GQA * QT, 1)

    def look(dist):
        bucket = _rel_bucket_np(dist)
        return rel_bias[jnp.asarray(np.broadcast_to(bucket[None], (N_KV,) + bucket.shape)), jnp.asarray(hh)]

    r4 = np.tile(r, GQA)[:, None]
    cols = np.arange(LANES)[None, :]
    t0 = look(r4 - cols)
    t1 = look(QT + r4 - cols)
    cf = look(np.full((GQA * QT, LANES), 4 * REL_MAX_DIST))
    cpos = cols * CMP_STRIDE + (CMP_LEN - 1)
    bc = jnp.stack([look(i * QT + r4 - cpos) for i in range(nt)], axis=1)
    c = np.arange(LANES)[:, None]
    n = np.arange(LANES)[None, :]
    ov = ((c * CMP_STRIDE < n * SEL_BLOCK + SEL_BLOCK) & (c * CMP_STRIDE + CMP_LEN > n * SEL_BLOCK) & (c < N_CMP))
    ov_p = jnp.asarray(ov & (n < seq // SEL_BLOCK), dtype=bf16)
    ov_s = jnp.asarray(ov & (n < past_len // SEL_BLOCK + 1), dtype=bf16)
    ee = jnp.asarray((np.arange(seq)[None, :] // SEL_BLOCK) == np.arange(LANES)[:, None], dtype=bf16)

    def look_rows(dist):
        bucket = _rel_bucket_np(dist)
        return rel_bias[jnp.asarray(bucket)].T

    bc_s = look_rows(past_len - (np.arange(LANES) * CMP_STRIDE + CMP_LEN - 1))
    bs_s = look_rows(past_len - np.arange(past_len))
    bw_s = look_rows(WINDOW - np.arange(WINDOW))
    b0_s = look_rows(np.zeros((LANES,), np.int64))
    return dict(t0=t0, t1=t1, cf=cf, bc=bc, ov=ov_p, ee=ee, ov_s=ov_s, bc_s=bc_s, bs_s=bs_s, bw_s=bw_s, b0_s=b0_s)


def _proj_column_map():
    idx = np.full((PROJ_W,), -1, np.int64)
    for g in range(GQA):
        for h in range(N_KV):
            idx[g * LANES + h * HEAD_DIM + np.arange(HEAD_DIM)] = (h * GQA + g) * HEAD_DIM + np.arange(HEAD_DIM)
    idx[_C_CK:_C_CK + 768] = 512 + np.arange(768)
    idx[_C_GATE:_C_GATE + 24] = 1280 + np.arange(24)
    idx[_C_GQ:_C_GQ + 128] = 1304 + np.arange(128)
    idx[_C_GK:_C_GK + 128] = 1432 + np.arange(128)
    idx[_C_GV:_C_GV + 256] = 1560 + np.arange(256)
    idx[_C_GR:_C_GR + 256] = 1816 + np.arange(256)
    idx[_C_GLR:_C_GLR + 16] = 2072 + np.arange(16)
    idx[_C_PU:_C_PU + 256] = 2088 + np.arange(256)
    return idx


def _out_row_map():
    idx = np.arange(D_MODEL)
    for g in range(GQA):
        for h in range(N_KV):
            idx[g * LANES + h * HEAD_DIM + np.arange(HEAD_DIM)] = (h * GQA + g) * HEAD_DIM + np.arange(HEAD_DIM)
    return idx


def _block_diag(blocks):
    n, r, c = blocks.shape
    eye = jnp.eye(n, dtype=blocks.dtype)
    return (eye[:, None, :, None] * blocks[:, :, None, :]).reshape(n * r, n * c)


def _layer_weights(l, p):
    tile2 = lambda v: jnp.tile(v, 2)[None, :]
    cmap = _proj_column_map()
    w_in = jnp.concatenate([p['w_in'][l], jnp.zeros((D_MODEL, 1), f32)], axis=1)
    w_in = jnp.take(w_in, jnp.asarray(np.where(cmap < 0, w_in.shape[1] - 1, cmap)), axis=1).astype(bf16)
    k_gain = jnp.concatenate([jnp.tile(p['k_gain'][l], (1, 2)), jnp.zeros((SUBLANES - 3, LANES), f32)], axis=0)
    w_alpha = jnp.zeros((LANES, LANES), f32).at[:GLA_RANK].set(p['gla_w_alpha'][l]).astype(bf16)

    def cmp_w(kv):
        w1 = p['cmp_w1'][l, kv]
        two = jax.vmap(lambda w: _block_diag(jnp.stack([w, w])))(w1)
        wcat = jnp.concatenate([two[:CMP_STRIDE], two[CMP_STRIDE:]], axis=-1).astype(bf16)
        pe = jnp.tile(p['cmp_pe'][l, kv], (1, 2))
        pe = jnp.concatenate([jnp.broadcast_to(pe[:CMP_STRIDE, None, :], (CMP_STRIDE, SUBLANES, LANES)),
                              jnp.broadcast_to(pe[CMP_STRIDE:, None, :], (CMP_STRIDE, SUBLANES, LANES))], axis=1)
        w2 = _block_diag(jnp.stack([p['cmp_w2'][l, kv]] * 2)).astype(bf16)
        return wcat, pe, w2

    wk, pek, w2k = cmp_w(0)
    wv, pev, w2v = cmp_w(1)
    w_up = p['w_ffn_up'][l].astype(bf16)
    conv_w = jnp.concatenate([p['ffn_conv_w'][l], jnp.zeros((SUBLANES - 3, D_FF), f32)], axis=0)
    return dict(
        g_mix=p['g_mix'][l][None, :], w_in=w_in, q_gain=tile2(p['q_gain'][l]), k_gain=k_gain,
        w_alpha=w_alpha, b_alpha=p['gla_b_alpha'][l][None, :],
        cmp_wk=wk, cmp_wv=wv, cmp_pek=pek, cmp_pev=pev, cmp_w2k=w2k, cmp_w2v=w2v,
        gla_norm=p['gla_norm'][l][None, :],
        pool_w=_block_diag(p['pool_w'][l]).astype(bf16), pool_scale=p['pool_scale'][l][None, :],
        w_out=jnp.take(p['w_out'][l], jnp.asarray(_out_row_map()), axis=0).astype(bf16),
        g_ffn=p['g_ffn'][l][None, :],
        w_up_a=w_up[:, :D_FF], w_up_u=w_up[:, D_FF:], conv_w=conv_w, conv_b=p['ffn_conv_b'][l][None, :],
        w_down=p['w_ffn_down'][l].astype(bf16),
    )


def _kv_state(k, v, lead):
    return jnp.concatenate([k, v], axis=-1).reshape(lead + (2, N_KV, HEAD_DIM))


def _prompt_layer(x, lw, tabs, nb, seq):
    tm = 512
    (q, ck, cv, sk, sv, wk, wv, gt, gq, gk, gv, gr, gl, pu) = _in_proj(x, lw, tm)
    kc, vc = _compress_prompt(ck, cv, lw, nb, seq)
    o_nsa = _nsa_prompt(q, kc, vc, sk, sv, wk, wv, gt, tabs, nb, seq)
    o_gla, st = _gla_prompt(gq, gk, gl, gv, gr, lw['gla_norm'], nb, seq)
    o_pool = _pool_prompt(pu, lw['pool_w'], lw['pool_scale'], nb, seq)
    x_mid, hn = _out_proj(o_nsa, o_gla, o_pool, x, lw['w_out'], lw['g_ffn'], tm)
    x_out, a_tail = _ffn(hn, hn, x_mid, lw, tm, seq // tm, False)
    lead = (nb, seq)
    cmp_new = _kv_state(ck, cv, lead)
    slc_new = _kv_state(sk, sv, lead)
    win_new = _kv_state(wk, wv, lead)[:, seq - WINDOW:]
    st = st.reshape(nb, GLA_HEADS, GLA_DV, GLA_HEADS, GLA_DK)
    gla_state = jnp.stack([st[:, h, :, h, :] for h in range(GLA_HEADS)], axis=1).transpose(0, 1, 3, 2)
    pool_state = pu.reshape(nb, seq, 2 * LANES)[:, seq - POOL_BUF:]
    a_tail = a_tail.reshape(nb, seq // tm, SUBLANES, D_FF)
    conv_state = a_tail[:, -1, SUBLANES - 2:, :]
    return x_out, (cmp_new, slc_new, win_new, gla_state, pool_state, conv_state)


def _sample_layer(l, x, lw, tabs, page_table, cache_cmp, cache_slc, win_all, state_gla, state_pool, state_conv):
    nb = x.shape[0]
    (q, ck, cv, sk, sv, wk, wv, gt, gq, gk, gv, gr, gl, pu) = _in_proj(x, lw, nb)
    lane_hi = (np.arange(LANES) >= HEAD_DIM)
    own = jnp.asarray((lane_hi[None, :] == (np.arange(N_KV)[:, None] == 1)).astype(np.float32))
    qrows = (q.astype(f32).reshape(nb, 1, GQA, LANES) * own[None, :, None, :]).reshape(nb, N_HEADS, LANES)
    grows = jnp.pad(gt[:, :N_HEADS * 3].reshape(nb, N_HEADS, 3), ((0, 0), (0, 0), (0, LANES - 3)))
    knew = jnp.pad(jnp.stack([sk, sv, wk, wv], axis=1), ((0, 0), (0, SUBLANES - 4), (0, 0)))
    o = _nsa_sample(l, page_table, cache_cmp, cache_slc, win_all, qrows, grows, knew, lw, tabs)
    o = o.reshape(nb, N_KV, GQA, N_KV, HEAD_DIM)
    o_nsa = jnp.stack([o[:, h, :, h, :] for h in range(N_KV)], axis=2).reshape(nb, GQA * LANES).astype(bf16)
    cols = jnp.stack([gq, gk, gl], axis=-1)
    o_gla, s_new = _gla_sample(cols, gv.reshape(nb, GLA_HEADS, GLA_DV),
                               state_gla[l].reshape(nb, GLA_HEADS * GLA_DK, GLA_DV),
                               gr.reshape(nb, GLA_HEADS, GLA_DV), lw['gla_norm'].reshape(GLA_HEADS, GLA_DV))
    o_gla = o_gla.reshape(nb, GLA_HEADS * GLA_DV).astype(bf16)
    o_pool = _pool_sample(pu, state_pool[l].transpose(1, 0, 2), lw['pool_w'], lw['pool_scale'])
    x_mid, hn = _out_proj(o_nsa, o_gla, o_pool, x, lw['w_out'], lw['g_ffn'], nb)
    x_out, a = _ffn(hn, state_conv[l].transpose(1, 0, 2), x_mid, lw, nb, 1, True)
    lead = (nb, 1)
    cmp_new = _kv_state(ck, cv, lead)
    slc_new = _kv_state(sk, sv, lead)
    win_state = jnp.concatenate([win_all[l, :, 1:], jnp.concatenate([wk, wv], axis=-1)[:, None, :]], axis=1)
    win_state = win_state.reshape(nb, WINDOW, 2, N_KV, HEAD_DIM)
    gla_state = s_new.reshape(nb, GLA_HEADS, GLA_DK, GLA_DV)
    pool_state = jnp.concatenate([state_pool[l][:, 1:], pu[:, None, :]], axis=1)
    conv_state = jnp.concatenate([state_conv[l][:, 1:], a[:, None, :]], axis=1)
    return x_out, (cmp_new, slc_new, win_state, gla_state, pool_state, conv_state)


def kernel(x_prompt, x_sample, cache_cmp_kv, cache_slc_kv, page_table, state_win_kv, state_gla, state_pool,
           state_ffn_conv, rel_bias, g_mix, w_in, q_gain, k_gain, cmp_pe, cmp_w1, cmp_w2, gla_w_alpha, gla_b_alpha,
           gla_norm, pool_w, pool_scale, w_out, g_ffn, w_ffn_up, ffn_conv_w, ffn_conv_b, w_ffn_down):
    nb, seq, _ = x_prompt.shape
    db = x_sample.shape[0]
    depth = w_in.shape[0]
    n_phys = cache_cmp_kv.shape[1]
    past_len = page_table.shape[1] * cache_cmp_kv.shape[2]
    assert (seq, past_len, page_table.shape[1], cache_cmp_kv.shape[2]) == (2048, 2048, N_PAGES, PAGE)
    assert state_win_kv.shape[2] == WINDOW and x_sample.shape[1] == 1
    params = dict(g_mix=g_mix, w_in=w_in, q_gain=q_gain, k_gain=k_gain, cmp_pe=cmp_pe, cmp_w1=cmp_w1, cmp_w2=cmp_w2,
                  gla_w_alpha=gla_w_alpha, gla_b_alpha=gla_b_alpha, gla_norm=gla_norm, pool_w=pool_w,
                  pool_scale=pool_scale, w_out=w_out, g_ffn=g_ffn, w_ffn_up=w_ffn_up, ffn_conv_w=ffn_conv_w,
                  ffn_conv_b=ffn_conv_b, w_ffn_down=w_ffn_down)
    tabs = _bias_tables(rel_bias, seq, past_len)
    cache_cmp = cache_cmp_kv.reshape(depth, n_phys, PAGE, 2 * LANES)
    cache_slc = cache_slc_kv.reshape(depth, n_phys, PAGE, 2 * LANES)
    win_all = state_win_kv.reshape(depth, db, WINDOW, 2 * LANES)
    xp = x_prompt.reshape(nb * seq, D_MODEL)
    xs = x_sample.reshape(db, D_MODEL)
    st_p = [[] for _ in range(6)]
    st_s = [[] for _ in range(6)]
    for l in range(depth):
        lw = _layer_weights(l, params)
        xp, new_p = _prompt_layer(xp, lw, tabs, nb, seq)
        xs, new_s = _sample_layer(l, xs, lw, tabs, page_table, cache_cmp, cache_slc, win_all, state_gla,
                                  state_pool, state_ffn_conv)
        for i in range(6):
            st_p[i].append(new_p[i])
            st_s[i].append(new_s[i])
    outs_p = [jnp.stack(a, axis=0) for a in st_p]
    outs_s = [jnp.stack(a, axis=0) for a in st_s]
    return (xp.reshape(nb, seq, D_MODEL), xs.reshape(db, 1, D_MODEL), *outs_p, *outs_s)
```

```python
import functools
import math

import numpy as np
import jax
import jax.numpy as jnp
from jax import lax
from jax.experimental import pallas as pl
from jax.experimental.pallas import tpu as pltpu

f32 = jnp.float32
bf16 = jnp.bfloat16

D_MODEL = 1024
HEAD_DIM = 64
N_KV = 2
GQA = 4
N_HEADS = 8
CMP_STRIDE = 16
CMP_LEN = 32
N_CMP = 127
SEL_BLOCK = 64
N_SEL = 16
WINDOW = 512
ATT_SCALE = HEAD_DIM ** -0.5
REL_BUCKETS = 32
REL_MAX_DIST = 128
GLA_HEADS = 4
GLA_DK = 32
GLA_DV = 64
GLA_RANK = 16
GLA_TAU = 16.0
GLA_CHUNK = 64
POOL_WINDOWS = (2, 4, 8, 16)
POOL_BUF = 15
D_FF = 2816
EPS = 1e-6
NEG = -1e30
BIG = 1e9
PAGE = 128

LANES = 128
SUBLANES = 8
QT = 128
PROJ_W = 2560
VMEM_LIMIT = 56 * 1024 * 1024


def _cparams(sem):
    return pltpu.CompilerParams(dimension_semantics=sem, vmem_limit_bytes=VMEM_LIMIT)


def _dot(a, b):
    return jnp.dot(a, b, preferred_element_type=f32)


def _dot_nt(a, b):
    return lax.dot_general(a, b, (((1,), (1,)), ((), ())), preferred_element_type=f32)


def _dot_tn(a, b):
    return lax.dot_general(a, b, (((0,), (0,)), ((), ())), preferred_element_type=f32)


def _dot_hilo(a, b_bf16):
    hi = a.astype(bf16)
    lo = (a - hi.astype(f32)).astype(bf16)
    return _dot(hi, b_bf16) + _dot(lo, b_bf16)


def _pair_rmsnorm(y, gain):
    lane = lax.broadcasted_iota(jnp.int32, y.shape, y.ndim - 1)
    lo = lane < HEAD_DIM
    y2 = y * y
    s_lo = jnp.sum(jnp.where(lo, y2, 0.0), axis=-1, keepdims=True)
    s_hi = jnp.sum(jnp.where(lo, 0.0, y2), axis=-1, keepdims=True)
    ms = jnp.where(lo, s_lo, s_hi) * (1.0 / HEAD_DIM)
    return y * lax.rsqrt(ms + EPS) * gain


_C_Q, _C_CK, _C_CV, _C_SK, _C_SV, _C_WK, _C_WV = 0, 512, 640, 768, 896, 1024, 1152
_C_GATE, _C_GQ, _C_GK, _C_GV, _C_GR, _C_GLR, _C_PU = 1280, 1408, 1536, 1664, 1920, 2176, 2304


def _proj_kernel(x_ref, g_ref, w_ref, qg_ref, kg_ref, wa_ref, ba_ref,
                 q_o, ckv_o, skv_o, wkv_o, gt_o, gq_o, gk_o, gv_o, gr_o, gl_o, pu_o):
    x = x_ref[...]
    hn = (x * lax.rsqrt(jnp.mean(x * x, axis=-1, keepdims=True) + EPS) * g_ref[...]).astype(bf16)

    def proj(c0, n):
        return _dot(hn, w_ref[:, c0:c0 + n])

    for j in range(4):
        q_o[:, j * LANES:(j + 1) * LANES] = _pair_rmsnorm(proj(_C_Q + j * LANES, LANES), qg_ref[...]).astype(bf16)
    ckv_o[...] = proj(_C_CK, 2 * LANES)
    skv_o[:, :LANES] = _pair_rmsnorm(proj(_C_SK, LANES), kg_ref[1:2, :])
    skv_o[:, LANES:] = proj(_C_SV, LANES)
    wkv_o[:, :LANES] = _pair_rmsnorm(proj(_C_WK, LANES), kg_ref[2:3, :])
    wkv_o[:, LANES:] = proj(_C_WV, LANES)
    gt_o[...] = jax.nn.sigmoid(proj(_C_GATE, LANES))
    gq_o[...] = proj(_C_GQ, LANES)
    gk_o[...] = proj(_C_GK, LANES)
    gv_o[...] = proj(_C_GV, 2 * LANES)
    gr_o[...] = proj(_C_GR, 2 * LANES)
    z = _dot(proj(_C_GLR, LANES).astype(bf16), wa_ref[...]) + ba_ref[...]
    gl_o[...] = jax.nn.log_sigmoid(z) * (1.0 / GLA_TAU)
    pu_o[...] = proj(_C_PU, 2 * LANES)


def _in_proj(x, lw, tm):
    m = x.shape[0]
    row = lambda n: pl.BlockSpec((tm, n), lambda i: (i, 0))
    full = lambda a: pl.BlockSpec(a.shape, lambda i: (0,) * a.ndim)
    widths = (512, 256, 256, 256, 128, 128, 128, 256, 256, 128, 256)
    dtypes = (bf16,) + (f32,) * 10
    args = (x, lw['g_mix'], lw['w_in'], lw['q_gain'], lw['k_gain'], lw['w_alpha'], lw['b_alpha'])
    return pl.pallas_call(
        _proj_kernel,
        grid=(m // tm,),
        in_specs=[row(D_MODEL)] + [full(a) for a in args[1:]],
        out_specs=[row(n) for n in widths],
        out_shape=[jax.ShapeDtypeStruct((m, n), dt) for n, dt in zip(widths, dtypes)],
        compiler_params=_cparams(("parallel",)),
        name="in_proj",
    )(*args)


def _compress(get_x, w_ref, pe_ref, w2_ref):
    acc = jnp.zeros((LANES + 2 * SUBLANES, 4 * LANES), f32)
    for s in range(CMP_STRIDE):
        xs = jnp.concatenate([get_x(s), pe_ref[s]], axis=0).astype(bf16)
        acc = acc + _dot(xs, w_ref[s])
    a0 = acc[:LANES, :2 * LANES]
    a1 = acc[:LANES, 2 * LANES:]
    c0 = acc[LANES:LANES + 1, :2 * LANES] + acc[LANES + SUBLANES:LANES + SUBLANES + 1, 2 * LANES:]
    h = a0 + pltpu.roll(a1, LANES - 1, 0) + c0
    return _dot(jax.nn.gelu(h).astype(bf16), w2_ref[...])


def _compress_prompt_kernel(ck_ref, cv_ref, wk_ref, wv_ref, pek_ref, pev_ref, w2k_ref, w2v_ref, kg_ref, kc_o, vc_o):
    kx = lambda s: ck_ref[pl.ds(s, LANES, stride=CMP_STRIDE), :]
    vx = lambda s: cv_ref[pl.ds(s, LANES, stride=CMP_STRIDE), :]
    kc_o[...] = _pair_rmsnorm(_compress(kx, wk_ref, pek_ref, w2k_ref), kg_ref[0:1, :])
    vc_o[...] = _compress(vx, wv_ref, pev_ref, w2v_ref)


def _compress_prompt(ckv, lw, nb, seq):
    full = lambda a: pl.BlockSpec(a.shape, lambda b: (0,) * a.ndim)
    kblk = pl.BlockSpec((seq, LANES), lambda b: (b, 0))
    vblk = pl.BlockSpec((seq, LANES), lambda b: (b, 1))
    outblk = pl.BlockSpec((None, LANES, LANES), lambda b: (b, 0, 0))
    ws = (lw['cmp_wk'], lw['cmp_wv'], lw['cmp_pek'], lw['cmp_pev'], lw['cmp_w2k'], lw['cmp_w2v'], lw['k_gain'])
    return pl.pallas_call(
        _compress_prompt_kernel,
        grid=(nb,),
        in_specs=[kblk, vblk] + [full(a) for a in ws],
        out_specs=[outblk, outblk],
        out_shape=[jax.ShapeDtypeStruct((nb, LANES, LANES), f32)] * 2,
        compiler_params=_cparams(("parallel",)),
        name="compress_prompt",
    )(ckv, ckv, *ws)


def _rank_select(score, n_blocks):
    blk = lax.broadcasted_iota(jnp.int32, score.shape, 1)
    rank = jnp.zeros(score.shape, f32)
    for m in range(n_blocks):
        sm = score[:, m:m + 1]
        beats = (sm > score) | ((sm == score) & (m < blk))
        rank = rank + jnp.where(beats, 1.0, 0.0)
    return jnp.where(rank < float(N_SEL), 1.0, 0.0)


def _rank_select_rows(score, n_blocks):
    blk = lax.broadcasted_iota(jnp.int32, score.shape, 0)
    rank = jnp.zeros(score.shape, f32)
    for m in range(n_blocks):
        sm = score[m:m + 1, :]
        beats = (sm > score) | ((sm == score) & (m < blk))
        rank = rank + jnp.where(beats, 1.0, 0.0)
    return jnp.where(rank < float(N_SEL), 1.0, 0.0)


def _nsa_prompt_kernel(q_ref, kc_ref, vc_ref, sk_ref, sv_ref, wk_ref, wv_ref, gt_ref,
                       td_ref, ts_ref, cf_ref, we_ref, bc_ref, ovt_ref, eet_ref, o_ref,
                       selx_ref, m_ref, l_ref, acc_ref):
    i = pl.program_id(1)
    lane = lax.broadcasted_iota(jnp.int32, (QT, LANES), 1)
    row = lax.broadcasted_iota(jnp.int32, (QT, LANES), 0)
    n_blk = selx_ref.shape[1] // SEL_BLOCK
    blk = lax.broadcasted_iota(jnp.int32, (n_blk, QT), 0)
    cur = (i * QT + lax.broadcasted_iota(jnp.int32, (n_blk, QT), 1)) >> 6
    gates_t = gt_ref[...].T
    outs = [[None, None] for _ in range(GQA)]

    def tile4(x):
        return jnp.concatenate([x] * GQA, axis=1)

    def attend(chain, q4, k_ref, v_ref, kt, bias, sel_head=None, live=None):
        off = pl.multiple_of(kt * QT, QT)
        k = k_ref[pl.ds(off, QT), :].astype(bf16)
        v = v_ref[pl.ds(off, QT), :].astype(bf16)
        s = _dot_nt(k, q4) * ATT_SCALE + bias
        if live is not None:
            s = s + jnp.where(live, 0.0, NEG)
        if sel_head is not None:
            s = s + tile4(selx_ref[sel_head, pl.ds(off, QT), :])
        m_old = m_ref[chain]
        m_new = jnp.maximum(m_old, jnp.max(s, axis=0, keepdims=True))
        alpha = jnp.exp(m_old - m_new)
        p = jnp.exp(s - m_new)
        l_ref[chain] = alpha * l_ref[chain] + jnp.sum(p, axis=0, keepdims=True)
        acc_ref[chain] = alpha * acc_ref[chain] + _dot_tn(v, p.astype(bf16))
        m_ref[chain] = m_new

    m_ref[...] = jnp.full(m_ref.shape, NEG, f32)
    l_ref[...] = jnp.zeros(l_ref.shape, f32)
    acc_ref[...] = jnp.zeros(acc_ref.shape, f32)

    q4s, fars, o_cmps = [], [], []
    for h in range(N_KV):
        half = (lane >= HEAD_DIM) if h else (lane < HEAD_DIM)
        q4 = jnp.concatenate(
            [jnp.where(half, q_ref[:, g * LANES:(g + 1) * LANES], jnp.zeros((), bf16)) for g in range(GQA)], axis=0)
        q4s.append(q4)
        fars.append(cf_ref[h, 0:1, :])
        bias_c = bc_ref[h, pl.ds(pl.multiple_of(LANES - SUBLANES * i, SUBLANES), LANES), :]
        s = _dot_nt(kc_ref[...].astype(bf16), q4) * ATT_SCALE + bias_c
        e = jnp.exp(s - jnp.max(s, axis=0, keepdims=True))
        p = jnp.where(s > 0.5 * NEG, e / jnp.sum(e, axis=0, keepdims=True), 0.0)
        o_cmps.append(_dot_tn(vc_ref[...].astype(bf16), p.astype(bf16)))
        psum = p[:, 0:QT] + p[:, QT:2 * QT] + p[:, 2 * QT:3 * QT] + p[:, 3 * QT:4 * QT]
        hi = psum.astype(bf16)
        lo = (psum - hi.astype(f32)).astype(bf16)
        imp = (_dot(ovt_ref[...], hi) + _dot(ovt_ref[...], lo))[0:n_blk, :]
        forced = (blk == 0) | (blk == cur) | (blk == cur - 1)
        score = jnp.where(forced, BIG, jnp.where(blk <= cur, imp, -BIG))
        sel = _rank_select_rows(score, n_blk)
        sel = jnp.concatenate([sel, jnp.zeros((LANES - n_blk, QT), f32)], axis=0).astype(bf16)
        selx_ref[h] = (_dot(eet_ref[...], sel) - 1.0) * (-NEG)

    slc = lambda h: h
    win = lambda h: N_KV + h

    def far_tiles(kt, carry):
        for h in range(N_KV):
            attend(slc(h), q4s[h], sk_ref, sv_ref, kt, fars[h], sel_head=h)
        return carry

    lax.fori_loop(0, jnp.maximum(i - 1, 0), far_tiles, 0)

    def back_tile(back):
        return dict(kt=jnp.maximum(i - back, 0), live=(i >= back) if back else None)

    for h in range(N_KV):
        attend(win(h), q4s[h], wk_ref, wv_ref, bias=we_ref[h], **back_tile(4))
    for back in (3, 2):
        for h in range(N_KV):
            attend(win(h), q4s[h], wk_ref, wv_ref, bias=fars[h], **back_tile(back))
    for back, tab_ref in ((1, ts_ref), (0, td_ref)):
        for h in range(N_KV):
            attend(slc(h), q4s[h], sk_ref, sv_ref, bias=tab_ref[h], sel_head=h, **back_tile(back))
            attend(win(h), q4s[h], wk_ref, wv_ref, bias=tab_ref[h], **back_tile(back))

    for h in range(N_KV):
        o_slc = acc_ref[slc(h)] / l_ref[slc(h)]
        o_win = acc_ref[win(h)] / l_ref[win(h)]
        for g in range(GQA):
            c = (h * GQA + g) * 3
            sl = slice(g * QT, (g + 1) * QT)
            outs[g][h] = (gates_t[c:c + 1, :] * o_cmps[h][:, sl] + gates_t[c + 1:c + 2, :] * o_slc[:, sl]
                          + gates_t[c + 2:c + 3, :] * o_win[:, sl])
    for g in range(GQA):
        o_ref[:, g * LANES:(g + 1) * LANES] = jnp.where(row < HEAD_DIM, outs[g][0], outs[g][1]).T.astype(bf16)


def _nsa_prompt(q, kc, vc, skv, wkv, gt, tabs, nb, seq):
    nt = seq // QT
    tile = lambda n: pl.BlockSpec((QT, n), lambda b, i: (b * nt + i, 0))
    kblk = pl.BlockSpec((seq, LANES), lambda b, i: (b, 0))
    vblk = pl.BlockSpec((seq, LANES), lambda b, i: (b, 1))
    cblk = pl.BlockSpec((None, LANES, LANES), lambda b, i: (b, 0, 0))
    full = lambda a: pl.BlockSpec(a.shape, lambda b, i: (0,) * a.ndim)
    consts = (tabs['td'], tabs['ts'], tabs['cf'], tabs['we'], tabs['bc'], tabs['ovt'], tabs['eet'])
    return pl.pallas_call(
        _nsa_prompt_kernel,
        grid=(nb, nt),
        in_specs=[tile(512), cblk, cblk, kblk, vblk, kblk, vblk, tile(LANES)] + [full(a) for a in consts],
        out_specs=tile(512),
        out_shape=jax.ShapeDtypeStruct((nb * seq, 512), bf16),
        scratch_shapes=[pltpu.VMEM((N_KV, seq, QT), f32), pltpu.VMEM((2 * N_KV, 1, GQA * QT), f32),
                        pltpu.VMEM((2 * N_KV, 1, GQA * QT), f32), pltpu.VMEM((2 * N_KV, LANES, GQA * QT), f32)],
        compiler_params=_cparams(("parallel", "arbitrary")),
        name="nsa_prompt",
    )(q, kc, vc, skv, skv, wkv, wkv, gt, *consts)


N_PAGES = 16


def _nsa_sample_kernel(pt_ref, *refs):
    ckp = refs[0:N_PAGES]
    cvp = refs[N_PAGES:2 * N_PAGES]
    skp = refs[2 * N_PAGES:3 * N_PAGES]
    svp = refs[3 * N_PAGES:4 * N_PAGES]
    (win_ref, qr_ref, gr_ref, knew_ref, wcat_k, wcat_v, pek_ref, pev_ref, w2k_ref, w2v_ref, kg_ref,
     bc_ref, bs_ref, bw_ref, b0_ref, ov_ref, o_ref) = refs[4 * N_PAGES:]
    lane = lax.broadcasted_iota(jnp.int32, (SUBLANES, LANES), 1)

    def hist(pages):
        return lambda s: jnp.concatenate([pg[pl.ds(s, SUBLANES, stride=CMP_STRIDE), :] for pg in pages], axis=0)

    kc = _pair_rmsnorm(_compress(hist(ckp), wcat_k, pek_ref, w2k_ref), kg_ref[0:1, :])
    vc = _compress(hist(cvp), wcat_v, pev_ref, w2v_ref)
    qf = qr_ref[...]
    qb = qf.astype(bf16)
    gates = gr_ref[...]
    knew = knew_ref[...]
    hi_rows = lax.broadcasted_iota(jnp.int32, (SUBLANES, LANES), 0) >= GQA
    own = (lane >= HEAD_DIM) == hi_rows

    s = _dot_nt(qb, kc.astype(bf16)) * ATT_SCALE + bc_ref[...]
    mskc = lane < N_CMP
    s = jnp.where(mskc, s, NEG)
    e = jnp.exp(s - jnp.max(s, axis=-1, keepdims=True))
    p = jnp.where(mskc, e / jnp.sum(e, axis=-1, keepdims=True), 0.0)
    o_cmp = _dot(p.astype(bf16), vc.astype(bf16))
    psum = jnp.broadcast_to(jnp.sum(p.reshape(N_KV, GQA, LANES), axis=1, keepdims=True), (N_KV, GQA, LANES))
    imp = _dot_hilo(psum.reshape(SUBLANES, LANES), ov_ref[...])
    n_blk = N_PAGES * PAGE // SEL_BLOCK + 1
    cur = n_blk - 1
    forced = (lane == 0) | (lane == cur) | (lane == cur - 1)
    score = jnp.where(forced, BIG, imp)
    score = jnp.where(lane < n_blk, score, -3e38)
    sel = _rank_select(score, n_blk)

    def softmax_av(s_parts, s_new, v_parts, v_new):
        mx = s_new
        for sp in s_parts:
            mx = jnp.maximum(mx, jnp.max(sp, axis=-1, keepdims=True))
        p_new = jnp.exp(s_new - mx)
        den = p_new
        acc = p_new * v_new
        for sp, vp in zip(s_parts, v_parts):
            pp = jnp.exp(sp - mx)
            den = den + jnp.sum(pp, axis=-1, keepdims=True)
            acc = acc + _dot(pp.astype(bf16), vp.astype(bf16))
        return acc / den

    def new_row(r):
        return knew[r:r + 1, :]

    def score_new(krow):
        return jnp.sum(jnp.where(own, qf * krow, 0.0), axis=-1, keepdims=True) * ATT_SCALE + b0_ref[:, 0:1]

    s_parts = []
    for j in range(N_PAGES):
        sj = _dot_nt(qb, skp[j][...].astype(bf16)) * ATT_SCALE + bs_ref[:, j * PAGE:(j + 1) * PAGE]
        mj = jnp.where(lane < SEL_BLOCK, sel[:, 2 * j:2 * j + 1], sel[:, 2 * j + 1:2 * j + 2]) > 0.5
        s_parts.append(jnp.where(mj, sj, NEG))
    o_slc = softmax_av(s_parts, score_new(new_row(0)), [pg[...] for pg in svp], new_row(1))

    w_parts = []
    for j in range(WINDOW // PAGE):
        kw = win_ref[j * PAGE:(j + 1) * PAGE, 0:LANES]
        sj = _dot_nt(qb, kw.astype(bf16)) * ATT_SCALE + bw_ref[:, j * PAGE:(j + 1) * PAGE]
        if j == 0:
            sj = jnp.where(lane >= 1, sj, NEG)
        w_parts.append(sj)
    v_parts = [win_ref[j * PAGE:(j + 1) * PAGE, LANES:2 * LANES] for j in range(WINDOW // PAGE)]
    o_win = softmax_av(w_parts, score_new(new_row(2)), v_parts, new_row(3))

    o_ref[...] = gates[:, 0:1] * o_cmp + gates[:, 1:2] * o_slc + gates[:, 2:3] * o_win


def _nsa_sample(layer, page_table, cache_cmp, cache_slc, win_state, qrows, grows, knew, lw, tabs):
    nb = page_table.shape[0]

    def page(j, half):
        return pl.BlockSpec((None, None, PAGE, LANES), lambda b, pt: (layer, pt[b, j], 0, half))

    full = lambda a: pl.BlockSpec(a.shape, lambda b, pt: (0,) * a.ndim)
    per_b = pl.BlockSpec((None, SUBLANES, LANES), lambda b, pt: (b, 0, 0))
    consts = (lw['cmp_wk'], lw['cmp_wv'], lw['cmp_pek'], lw['cmp_pev'], lw['cmp_w2k'], lw['cmp_w2v'], lw['k_gain'],
              tabs['bc_s'], tabs['bs_s'], tabs['bw_s'], tabs['b0_s'], tabs['ov_s'])
    in_specs = ([page(j, 0) for j in range(N_PAGES)] + [page(j, 1) for j in range(N_PAGES)]
                + [page(j, 0) for j in range(N_PAGES)] + [page(j, 1) for j in range(N_PAGES)]
                + [pl.BlockSpec((None, None, WINDOW, 2 * LANES), lambda b, pt: (layer, b, 0, 0)), per_b, per_b, per_b]
                + [full(a) for a in consts])
    return pl.pallas_call(
        _nsa_sample_kernel,
        grid_spec=pltpu.PrefetchScalarGridSpec(
            num_scalar_prefetch=1, grid=(nb,), in_specs=in_specs, out_specs=per_b),
        out_shape=jax.ShapeDtypeStruct((nb, SUBLANES, LANES), f32),
        compiler_params=_cparams(("parallel",)),
        name="nsa_sample",
    )(page_table, *([cache_cmp] * (2 * N_PAGES)), *([cache_slc] * (2 * N_PAGES)), win_state, qrows, grows, knew, *consts)


def _gla_out(o, gn_ref, gr):
    o = jnp.concatenate([_pair_rmsnorm(o[:, :LANES], gn_ref[:, :LANES]),
                         _pair_rmsnorm(o[:, LANES:], gn_ref[:, LANES:])], axis=1)
    return o * jax.nn.silu(gr)


def _gla_prompt_kernel(gq_ref, gk_ref, gl_ref, gv_ref, gr_ref, gn_ref, o_ref, st_o, st_ref):
    c_rows = GLA_CHUNK
    lane = lax.broadcasted_iota(jnp.int32, (c_rows, LANES), 1)
    row = lax.broadcasted_iota(jnp.int32, (c_rows, LANES), 0)
    tril = (lax.broadcasted_iota(jnp.int32, (c_rows, c_rows), 0)
            >= lax.broadcasted_iota(jnp.int32, (c_rows, c_rows), 1))
    lane_v = lax.broadcasted_iota(jnp.int32, (c_rows, GLA_HEADS * GLA_DV), 1)
    srow = lax.broadcasted_iota(jnp.int32, (GLA_HEADS * GLA_DV, LANES), 0)
    scol = lax.broadcasted_iota(jnp.int32, (GLA_HEADS * GLA_DV, LANES), 1)
    diag = (srow >> 6) == (scol >> 5)
    st_ref[...] = jnp.zeros(st_ref.shape, f32)

    def body(c, carry):
        off = pl.multiple_of(c * c_rows, c_rows)
        q = gq_ref[pl.ds(off, c_rows), :] * (GLA_DK ** -0.5)
        k = gk_ref[pl.ds(off, c_rows), :]
        v = gv_ref[pl.ds(off, c_rows), :]
        b = gl_ref[pl.ds(off, c_rows), :]
        sh = 1
        while sh < c_rows:
            b = b + jnp.where(row >= sh, pltpu.roll(b, sh, 0), 0.0)
            sh *= 2
        qd = q * jnp.exp(b)
        kd = (k * jnp.exp(-b)).astype(bf16)
        bl = b[c_rows - 1:c_rows, :]
        kk = (k * jnp.exp(bl - b)).astype(bf16)
        st = st_ref[...]
        vb = v.astype(bf16)
        o = _dot_nt(qd.astype(bf16), st.astype(bf16))
        for hh in range(GLA_HEADS):
            qm = jnp.where((lane >> 5) == hh, qd, 0.0).astype(bf16)
            a = jnp.where(tril, _dot_nt(qm, kd), 0.0)
            o = o + jnp.where((lane_v >> 6) == hh, _dot(a.astype(bf16), vb), 0.0)
        st_ref[...] = jnp.exp(bl) * st + jnp.where(diag, _dot_tn(vb, kk), 0.0)
        o_ref[pl.ds(off, c_rows), :] = _gla_out(o, gn_ref, gr_ref[pl.ds(off, c_rows), :]).astype(bf16)
        return carry

    lax.fori_loop(0, gq_ref.shape[0] // c_rows, body, 0)
    st_o[...] = st_ref[...]


def _gla_prompt(gq, gk, gl, gv, gr, gn, nb, seq):
    blk = lambda n: pl.BlockSpec((seq, n), lambda b: (b, 0))
    return pl.pallas_call(
        _gla_prompt_kernel,
        grid=(nb,),
        in_specs=[blk(LANES), blk(LANES), blk(LANES), blk(2 * LANES), blk(2 * LANES),
                  pl.BlockSpec(gn.shape, lambda b: (0, 0))],
        out_specs=[blk(2 * LANES), pl.BlockSpec((None, GLA_HEADS * GLA_DV, LANES), lambda b: (b, 0, 0))],
        out_shape=[jax.ShapeDtypeStruct((nb * seq, 2 * LANES), bf16),
                   jax.ShapeDtypeStruct((nb, GLA_HEADS * GLA_DV, LANES), f32)],
        scratch_shapes=[pltpu.VMEM((GLA_HEADS * GLA_DV, LANES), f32)],
        compiler_params=_cparams(("parallel",)),
        name="gla_prompt",
    )(gq, gk, gl, gv, gr, gn)


def _gla_sample_kernel(cols_ref, v_ref, s0_ref, gr_ref, gn_ref, o_ref, s_o):
    nb = s0_ref.shape[0]
    c = cols_ref[...]
    q = c[:, :, 0:1] * (GLA_DK ** -0.5)
    k = c[:, :, 1:2]
    g = c[:, :, 2:3]
    eg = jnp.exp(g)
    qd = q * eg
    kd = k * jnp.exp(-g)
    s0 = s0_ref[...]
    v = v_ref[...]
    a = jnp.sum((qd * kd).reshape(nb, GLA_HEADS, GLA_DK, 1), axis=2)
    o = jnp.sum((qd * s0).reshape(nb, GLA_HEADS, GLA_DK, GLA_DV), axis=2) + a * v
    vexp = jnp.broadcast_to(v[:, :, None, :], (nb, GLA_HEADS, GLA_DK, GLA_DV)).reshape(nb, GLA_HEADS * GLA_DK, GLA_DV)
    s_o[...] = eg * s0 + k * vexp
    y = o * lax.rsqrt(jnp.mean(o * o, axis=-1, keepdims=True) + EPS) * gn_ref[...]
    o_ref[...] = y * jax.nn.silu(gr_ref[...])


def _gla_sample(cols, v, s0, gr, gn, bt=8):
    nb = s0.shape[0]
    blk = lambda a: pl.BlockSpec((bt,) + a.shape[1:], lambda i: (i,) + (0,) * (a.ndim - 1))
    return pl.pallas_call(
        _gla_sample_kernel,
        grid=(nb // bt,),
        in_specs=[blk(cols), blk(v), blk(s0), blk(gr), pl.BlockSpec(gn.shape, lambda i: (0, 0))],
        out_specs=[blk(v), blk(s0)],
        out_shape=[jax.ShapeDtypeStruct(v.shape, f32), jax.ShapeDtypeStruct(s0.shape, f32)],
        compiler_params=_cparams(("parallel",)),
        name="gla_sample",
    )(cols, v, s0, gr, gn)


def _pool_finish(sums, u, cnt_inv, w_ref, sc_ref):
    lane = lax.broadcasted_iota(jnp.int32, u.shape, 1)
    grp = lane >> 6
    s = jnp.where(grp == 0, sums[0], jnp.where(grp == 1, sums[1], jnp.where(grp == 2, sums[2], sums[3])))
    dlt = s * cnt_inv - u
    return _dot(dlt.astype(bf16), w_ref[...]) * sc_ref[...]


def _pool_prompt_kernel(u_ref, w_ref, sc_ref, o_ref):
    u = u_ref[...]
    row = lax.broadcasted_iota(jnp.int32, u.shape, 0)
    lane = lax.broadcasted_iota(jnp.int32, u.shape, 1)
    sums = []
    s = u
    sh = 1
    while sh < POOL_WINDOWS[-1]:
        s = s + jnp.where(row >= sh, pltpu.roll(s, sh, 0), 0.0)
        sums.append(s)
        sh *= 2
    grp = lane >> 6
    win = jnp.where(grp == 0, POOL_WINDOWS[0], jnp.where(grp == 1, POOL_WINDOWS[1],
                    jnp.where(grp == 2, POOL_WINDOWS[2], POOL_WINDOWS[3])))
    cnt = jnp.minimum(win, row + 1).astype(f32)
    o_ref[...] = _pool_finish(sums, u, 1.0 / cnt, w_ref, sc_ref).astype(bf16)


def _pool_prompt(pu, w, sc, nb, seq):
    blk = pl.BlockSpec((seq, 2 * LANES), lambda b: (b, 0))
    full = lambda a: pl.BlockSpec(a.shape, lambda b: (0,) * a.ndim)
    return pl.pallas_call(
        _pool_prompt_kernel,
        grid=(nb,),
        in_specs=[blk, full(w), full(sc)],
        out_specs=blk,
        out_shape=jax.ShapeDtypeStruct((nb * seq, 2 * LANES), bf16),
        compiler_params=_cparams(("parallel",)),
        name="pool_prompt",
    )(pu, w, sc)


def _pool_sample_kernel(u_ref, buf_ref, w_ref, sc_ref, o_ref):
    u = u_ref[...]
    lane = lax.broadcasted_iota(jnp.int32, u.shape, 1)
    sums = []
    s = u
    nxt = POOL_BUF - 1
    for win in POOL_WINDOWS:
        while POOL_BUF - nxt < win:
            s = s + buf_ref[nxt]
            nxt -= 1
        sums.append(s)
    grp = lane >> 6
    cnt_inv = jnp.where(grp == 0, 1.0 / POOL_WINDOWS[0], jnp.where(grp == 1, 1.0 / POOL_WINDOWS[1],
                        jnp.where(grp == 2, 1.0 / POOL_WINDOWS[2], 1.0 / POOL_WINDOWS[3])))
    o_ref[...] = _pool_finish(sums, u, cnt_inv, w_ref, sc_ref).astype(bf16)


def _pool_sample(pu, buf_t, w, sc):
    full = lambda a: pl.BlockSpec(a.shape, lambda i: (0,) * a.ndim)
    return pl.pallas_call(
        _pool_sample_kernel,
        grid=(1,),
        in_specs=[full(pu), full(buf_t), full(w), full(sc)],
        out_specs=full(pu),
        out_shape=jax.ShapeDtypeStruct(pu.shape, bf16),
        compiler_params=_cparams(("arbitrary",)),
        name="pool_sample",
    )(pu, buf_t, w, sc)


def _out_proj_kernel(a_ref, b_ref, c_ref, x_ref, w_ref, g_ref, x_o, h_o):
    y = (_dot(a_ref[...], w_ref[0:512, :]) + _dot(b_ref[...], w_ref[512:768, :])
         + _dot(c_ref[...], w_ref[768:1024, :]))
    x = x_ref[...] + y
    x_o[...] = x
    h_o[...] = (x * lax.rsqrt(jnp.mean(x * x, axis=-1, keepdims=True) + EPS) * g_ref[...]).astype(bf16)


def _out_proj(o_nsa, o_gla, o_pool, x, w, g, tm):
    m = x.shape[0]
    row = lambda n: pl.BlockSpec((tm, n), lambda i: (i, 0))
    full = lambda a: pl.BlockSpec(a.shape, lambda i: (0,) * a.ndim)
    return pl.pallas_call(
        _out_proj_kernel,
        grid=(m // tm,),
        in_specs=[row(512), row(256), row(256), row(D_MODEL), full(w), full(g)],
        out_specs=[row(D_MODEL), row(D_MODEL)],
        out_shape=[jax.ShapeDtypeStruct((m, D_MODEL), f32), jax.ShapeDtypeStruct((m, D_MODEL), bf16)],
        compiler_params=_cparams(("parallel",)),
        name="out_proj",
    )(o_nsa, o_gla, o_pool, x, w, g)


FF_CHUNK = 256


def _ffn_kernel(tiles_per_seq, decode, h_ref, p_ref, x_ref, wa_ref, wu_ref, cw_ref, cb_ref, wd_ref, x_o, a_o):
    hn = h_ref[...]
    tm = hn.shape[0]
    row = lax.broadcasted_iota(jnp.int32, (tm, FF_CHUNK), 0)
    if not decode:
        first = (pl.program_id(0) % tiles_per_seq) == 0
        halo = p_ref[...]
    acc = jnp.zeros((tm, D_MODEL), f32)
    for c in range(D_FF // FF_CHUNK):
        cs = slice(c * FF_CHUNK, (c + 1) * FF_CHUNK)
        a = _dot(hn, wa_ref[:, cs])
        u = _dot(hn, wu_ref[:, cs])
        if decode:
            a2 = p_ref[0, :, cs]
            a1 = p_ref[1, :, cs]
            a_o[:, cs] = a
        else:
            ah = jnp.where(first, 0.0, _dot(halo, wa_ref[:, cs]))
            p1 = ah[SUBLANES - 1:SUBLANES, :]
            p2 = ah[SUBLANES - 2:SUBLANES - 1, :]
            a1 = jnp.where(row == 0, p1, pltpu.roll(a, 1, 0))
            a2 = jnp.where(row == 0, p2, jnp.where(row == 1, p1, pltpu.roll(a, 2, 0)))
            a_o[:, cs] = a[tm - SUBLANES:, :]
        ac = cb_ref[:, cs] + a2 * cw_ref[0:1, cs] + a1 * cw_ref[1:2, cs] + a * cw_ref[2:3, cs]
        act = (jax.nn.silu(ac) * u).astype(bf16)
        acc = acc + _dot(act, wd_ref[cs, :])
    x_o[...] = x_ref[...] + acc


def _ffn(hn, prev, x, lw, tm, tiles_per_seq, decode):
    m = x.shape[0]
    nt = m // tm
    row = lambda n: pl.BlockSpec((tm, n), lambda i: (i, 0))
    const = lambda a: pl.BlockSpec(a.shape, lambda i: (0,) * a.ndim, pipeline_mode=pl.Buffered(1))
    if decode:
        prev_spec = pl.BlockSpec(prev.shape, lambda i: (0, 0, 0))
        a_spec = row(D_FF)
        a_shape = jax.ShapeDtypeStruct((m, D_FF), f32)
    else:
        per = tm // SUBLANES
        prev_spec = pl.BlockSpec((SUBLANES, D_MODEL), lambda i: (jnp.maximum(i * per - 1, 0), 0))
        a_spec = pl.BlockSpec((None, SUBLANES, D_FF), lambda i: (i, 0, 0))
        a_shape = jax.ShapeDtypeStruct((nt, SUBLANES, D_FF), f32)
    ws = (lw['w_up_a'], lw['w_up_u'], lw['conv_w'], lw['conv_b'], lw['w_down'])
    return pl.pallas_call(
        functools.partial(_ffn_kernel, tiles_per_seq, decode),
        grid=(nt,),
        in_specs=[row(D_MODEL), prev_spec, row(D_MODEL)] + [const(a) for a in ws],
        out_specs=[row(D_MODEL), a_spec],
        out_shape=[jax.ShapeDtypeStruct((m, D_MODEL), f32), a_shape],
        compiler_params=_cparams(("parallel",)),
        name="ffn_decode" if decode else "ffn_prompt",
    )(hn, prev, x, *ws)


def _rel_bucket_np(d):
    d = np.maximum(d, 0)
    exact = REL_BUCKETS // 2
    lg = np.log(np.maximum(d, 1).astype(np.float32) / np.float32(exact)) / np.float32(math.log(REL_MAX_DIST / exact))
    large = np.minimum(exact + (lg * np.float32(REL_BUCKETS - exact)).astype(np.int32), REL_BUCKETS - 1)
    return np.where(d < exact, d, large).astype(np.int32)


def _bias_lookup(rel_bias, dist):
    bucket = _rel_bucket_np(np.asarray(dist)).reshape(-1)
    onehot = jnp.asarray((bucket[:, None] == np.arange(REL_BUCKETS)[None, :]).astype(np.int8)).astype(f32)
    out = jnp.dot(onehot, rel_bias, precision=lax.Precision.HIGHEST)
    return out.reshape(tuple(np.shape(dist)) + (N_HEADS,))


def _bias_tables(rel_bias, seq, past_len):
    c = np.arange(QT)[:, None]
    r = np.arange(QT)[None, :]

    def key_major(t):
        rows = t.shape[0]
        return t.reshape(rows, QT, N_KV, GQA).transpose(2, 0, 3, 1).reshape(N_KV, rows, GQA * QT)

    def masked(t, keep):
        return jnp.where(jnp.asarray(np.tile(keep, (1, GQA)))[None], t, NEG)

    far = jnp.broadcast_to(rel_bias[REL_BUCKETS - 1].reshape(N_KV, 1, GQA, 1), (N_KV, QT, GQA, QT))
    far = far.reshape(N_KV, QT, GQA * QT)
    td = masked(key_major(_bias_lookup(rel_bias, r - c)), c <= r)
    ts = key_major(_bias_lookup(rel_bias, QT + r - c))
    we = masked(far, c > r)
    cf = far[:, :SUBLANES, :]
    j = np.arange(2 * QT)[:, None]
    dist_c = r - (CMP_STRIDE * (j - QT) + CMP_LEN - 1)
    bc = masked(key_major(_bias_lookup(rel_bias, dist_c)), dist_c >= 0)
    cc = np.arange(LANES)[:, None]
    n = np.arange(LANES)[None, :]
    ov = ((cc * CMP_STRIDE < n * SEL_BLOCK + SEL_BLOCK) & (cc * CMP_STRIDE + CMP_LEN > n * SEL_BLOCK) & (cc < N_CMP))
    ovt = jnp.asarray((ov & (n < seq // SEL_BLOCK)).T, dtype=bf16)
    ov_s = jnp.asarray(ov & (n < past_len // SEL_BLOCK + 1), dtype=bf16)
    eet = jnp.asarray((np.arange(seq)[:, None] // SEL_BLOCK) == np.arange(LANES)[None, :], dtype=bf16)

    def look_rows(dist):
        return _bias_lookup(rel_bias, dist).T

    bc_s = look_rows(past_len - (np.arange(LANES) * CMP_STRIDE + CMP_LEN - 1))
    bs_s = look_rows(past_len - np.arange(past_len))
    bw_s = look_rows(WINDOW - np.arange(WINDOW))
    b0_s = look_rows(np.zeros((LANES,), np.int64))
    return dict(td=td, ts=ts, cf=cf, we=we, bc=bc, ovt=ovt, eet=eet,
                ov_s=ov_s, bc_s=bc_s, bs_s=bs_s, bw_s=bw_s, b0_s=b0_s)


def _pair_order(w, axis):
    shape = w.shape
    w = w.reshape(shape[:axis] + (N_KV, GQA, HEAD_DIM) + shape[axis + 1:])
    return jnp.swapaxes(w, axis, axis + 1).reshape(shape)


def _repack_w_in(w):
    z = lambda n: jnp.zeros((w.shape[0], n), w.dtype)
    return jnp.concatenate([
        _pair_order(w[:, :512], 1), w[:, 512:1280],
        w[:, 1280:1304], z(LANES - 24),
        w[:, 1304:2072],
        w[:, 2072:2088], z(LANES - GLA_RANK),
        w[:, 2088:2344]], axis=1)


def _block_diag(blocks):
    n, r, c = blocks.shape
    eye = jnp.eye(n, dtype=blocks.dtype)
    return (eye[:, None, :, None] * blocks[:, :, None, :]).reshape(n * r, n * c)


def _layer_weights(l, p):
    tile2 = lambda v: jnp.tile(v, 2)[None, :]
    w_in = _repack_w_in(p['w_in'][l]).astype(bf16)
    w_out = jnp.concatenate([_pair_order(p['w_out'][l][:512], 0), p['w_out'][l][512:]], axis=0).astype(bf16)
    k_gain = jnp.concatenate([jnp.tile(p['k_gain'][l], (1, 2)), jnp.zeros((SUBLANES - 3, LANES), f32)], axis=0)
    w_alpha = jnp.concatenate([p['gla_w_alpha'][l], jnp.zeros((LANES - GLA_RANK, LANES), f32)], axis=0).astype(bf16)

    def cmp_w(kv):
        w1 = p['cmp_w1'][l, kv]
        two = jax.vmap(lambda w: _block_diag(jnp.stack([w, w])))(w1)
        wcat = jnp.concatenate([two[:CMP_STRIDE], two[CMP_STRIDE:]], axis=-1).astype(bf16)
        pe = jnp.tile(p['cmp_pe'][l, kv], (1, 2))
        pe = jnp.concatenate([jnp.broadcast_to(pe[:CMP_STRIDE, None, :], (CMP_STRIDE, SUBLANES, LANES)),
                              jnp.broadcast_to(pe[CMP_STRIDE:, None, :], (CMP_STRIDE, SUBLANES, LANES))], axis=1)
        w2 = _block_diag(jnp.stack([p['cmp_w2'][l, kv]] * 2)).astype(bf16)
        return wcat, pe, w2

    wk, pek, w2k = cmp_w(0)
    wv, pev, w2v = cmp_w(1)
    w_up = p['w_ffn_up'][l].astype(bf16)
    conv_w = jnp.concatenate([p['ffn_conv_w'][l], jnp.zeros((SUBLANES - 3, D_FF), f32)], axis=0)
    return dict(
        g_mix=p['g_mix'][l][None, :], w_in=w_in, q_gain=tile2(p['q_gain'][l]), k_gain=k_gain,
        w_alpha=w_alpha, b_alpha=p['gla_b_alpha'][l][None, :],
        cmp_wk=wk, cmp_wv=wv, cmp_pek=pek, cmp_pev=pev, cmp_w2k=w2k, cmp_w2v=w2v,
        gla_norm=p['gla_norm'][l][None, :],
        pool_w=_block_diag(p['pool_w'][l]).astype(bf16), pool_scale=p['pool_scale'][l][None, :],
        w_out=w_out,
        g_ffn=p['g_ffn'][l][None, :],
        w_up_a=w_up[:, :D_FF], w_up_u=w_up[:, D_FF:], conv_w=conv_w, conv_b=p['ffn_conv_b'][l][None, :],
        w_down=p['w_ffn_down'][l].astype(bf16),
    )


def _prompt_layer(x, lw, tabs, nb, seq):
    tm = 512
    (q, ckv, skv, wkv, gt, gq, gk, gv, gr, gl, pu) = _in_proj(x, lw, tm)
    kc, vc = _compress_prompt(ckv, lw, nb, seq)
    o_nsa = _nsa_prompt(q, kc, vc, skv, wkv, gt, tabs, nb, seq)
    o_gla, st = _gla_prompt(gq, gk, gl, gv, gr, lw['gla_norm'], nb, seq)
    o_pool = _pool_prompt(pu, lw['pool_w'], lw['pool_scale'], nb, seq)
    x_mid, hn = _out_proj(o_nsa, o_gla, o_pool, x, lw['w_out'], lw['g_ffn'], tm)
    x_out, a_tail = _ffn(hn, hn, x_mid, lw, tm, seq // tm, False)
    win_tail = wkv.reshape(nb, seq, 2 * LANES)[:, seq - WINDOW:]
    st = st.reshape(nb, GLA_HEADS, GLA_DV, GLA_HEADS, GLA_DK)
    gla_state = jnp.stack([st[:, h, :, h, :] for h in range(GLA_HEADS)], axis=1).transpose(0, 1, 3, 2)
    pool_state = pu.reshape(nb, seq, 2 * LANES)[:, seq - POOL_BUF:]
    a_tail = a_tail.reshape(nb, seq // tm, SUBLANES, D_FF)
    conv_state = a_tail[:, -1, SUBLANES - 2:, :]
    return x_out, (ckv, skv, win_tail, gla_state, pool_state, conv_state)


def _sample_layer(l, x, lw, tabs, page_table, cache_cmp, cache_slc, win_all, state_gla, state_pool, state_conv):
    nb = x.shape[0]
    (q, ckv, skv, wkv, gt, gq, gk, gv, gr, gl, pu) = _in_proj(x, lw, nb)
    lane_hi = (np.arange(LANES) >= HEAD_DIM)
    own = jnp.asarray((lane_hi[None, :] == (np.arange(N_KV)[:, None] == 1)).astype(np.float32))
    qrows = (q.astype(f32).reshape(nb, 1, GQA, LANES) * own[None, :, None, :]).reshape(nb, N_HEADS, LANES)
    grows = jnp.pad(gt[:, :N_HEADS * 3].reshape(nb, N_HEADS, 3), ((0, 0), (0, 0), (0, LANES - 3)))
    knew = jnp.pad(jnp.concatenate([skv, wkv], axis=1).reshape(nb, 4, LANES), ((0, 0), (0, SUBLANES - 4), (0, 0)))
    o = _nsa_sample(l, page_table, cache_cmp, cache_slc, win_all, qrows, grows, knew, lw, tabs)
    o = o.reshape(nb, N_KV, GQA, N_KV, HEAD_DIM)
    o_nsa = jnp.stack([o[:, h, :, h, :] for h in range(N_KV)], axis=2).reshape(nb, GQA * LANES).astype(bf16)
    cols = jnp.stack([gq, gk, gl], axis=-1)
    o_gla, s_new = _gla_sample(cols, gv.reshape(nb, GLA_HEADS, GLA_DV),
                               state_gla[l].reshape(nb, GLA_HEADS * GLA_DK, GLA_DV),
                               gr.reshape(nb, GLA_HEADS, GLA_DV), lw['gla_norm'].reshape(GLA_HEADS, GLA_DV))
    o_gla = o_gla.reshape(nb, GLA_HEADS * GLA_DV).astype(bf16)
    o_pool = _pool_sample(pu, state_pool[l].transpose(1, 0, 2), lw['pool_w'], lw['pool_scale'])
    x_mid, hn = _out_proj(o_nsa, o_gla, o_pool, x, lw['w_out'], lw['g_ffn'], nb)
    x_out, a = _ffn(hn, state_conv[l].transpose(1, 0, 2), x_mid, lw, nb, 1, True)
    gla_state = s_new.reshape(nb, GLA_HEADS, GLA_DK, GLA_DV)
    return x_out, (ckv, skv, wkv, gla_state, pu, a)


def kernel(x_prompt, x_sample, cache_cmp_kv, cache_slc_kv, page_table, state_win_kv, state_gla, state_pool,
           state_ffn_conv, rel_bias, g_mix, w_in, q_gain, k_gain, cmp_pe, cmp_w1, cmp_w2, gla_w_alpha, gla_b_alpha,
           gla_norm, pool_w, pool_scale, w_out, g_ffn, w_ffn_up, ffn_conv_w, ffn_conv_b, w_ffn_down):
    nb, seq, _ = x_prompt.shape
    db = x_sample.shape[0]
    depth = w_in.shape[0]
    n_phys = cache_cmp_kv.shape[1]
    past_len = page_table.shape[1] * cache_cmp_kv.shape[2]
    assert (seq, past_len, page_table.shape[1], cache_cmp_kv.shape[2]) == (2048, 2048, N_PAGES, PAGE)
    assert state_win_kv.shape[2] == WINDOW and x_sample.shape[1] == 1
    params = dict(g_mix=g_mix, w_in=w_in, q_gain=q_gain, k_gain=k_gain, cmp_pe=cmp_pe, cmp_w1=cmp_w1, cmp_w2=cmp_w2,
                  gla_w_alpha=gla_w_alpha, gla_b_alpha=gla_b_alpha, gla_norm=gla_norm, pool_w=pool_w,
                  pool_scale=pool_scale, w_out=w_out, g_ffn=g_ffn, w_ffn_up=w_ffn_up, ffn_conv_w=ffn_conv_w,
                  ffn_conv_b=ffn_conv_b, w_ffn_down=w_ffn_down)
    tabs = _bias_tables(rel_bias, seq, past_len)
    cache_cmp = cache_cmp_kv.reshape(depth, n_phys, PAGE, 2 * LANES)
    cache_slc = cache_slc_kv.reshape(depth, n_phys, PAGE, 2 * LANES)
    win_all = state_win_kv.reshape(depth, db, WINDOW, 2 * LANES)
    xp = x_prompt.reshape(nb * seq, D_MODEL)
    xs = x_sample.reshape(db, D_MODEL)
    st_p = [[] for _ in range(6)]
    st_s = [[] for _ in range(6)]
    for l in range(depth):
        lw = _layer_weights(l, params)
        xp, new_p = _prompt_layer(xp, lw, tabs, nb, seq)
        xs, new_s = _sample_layer(l, xs, lw, tabs, page_table, cache_cmp, cache_slc, win_all, state_gla,
                                  state_pool, state_ffn_conv)
        for i in range(6):
            st_p[i].append(new_p[i])
            st_s[i].append(new_s[i])
    stk = lambda a: jnp.stack(a, axis=0)
    kv = lambda a, rows: a.reshape(depth, a.shape[1] // rows, rows, 2, N_KV, HEAD_DIM)
    outs_p = (kv(stk(st_p[0]), seq), kv(stk(st_p[1]), seq), kv(stk(st_p[2]).reshape(depth, nb * WINDOW, -1), WINDOW),
              stk(st_p[3]), stk(st_p[4]), stk(st_p[5]))
    shift_in = lambda old, new: jnp.concatenate([old[:, :, 1:], stk(new)[:, :, None, :]], axis=2)
    win_s = shift_in(win_all, st_s[2]).reshape(depth, db, WINDOW, 2, N_KV, HEAD_DIM)
    outs_s = (kv(stk(st_s[0]), 1), kv(stk(st_s[1]), 1), win_s, stk(st_s[3]),
              shift_in(state_pool, st_s[4]), shift_in(state_ffn_conv, st_s[5]))
    return (xp.reshape(nb, seq, D_MODEL), xs.reshape(db, 1, D_MODEL), *outs_p, *outs_s)
```

```python
import functools
import math

import numpy as np
import jax
import jax.numpy as jnp
from jax import lax
from jax.experimental import pallas as pl
from jax.experimental.pallas import tpu as pltpu

f32 = jnp.float32
bf16 = jnp.bfloat16

D_MODEL = 1024
HEAD_DIM = 64
N_KV = 2
GQA = 4
N_HEADS = 8
CMP_STRIDE = 16
CMP_LEN = 32
N_CMP = 127
SEL_BLOCK = 64
N_SEL = 16
WINDOW = 512
ATT_SCALE = HEAD_DIM ** -0.5
REL_BUCKETS = 32
REL_MAX_DIST = 128
GLA_HEADS = 4
GLA_DK = 32
GLA_DV = 64
GLA_RANK = 16
GLA_TAU = 16.0
GLA_CHUNK = 64
POOL_WINDOWS = (2, 4, 8, 16)
POOL_BUF = 15
D_FF = 2816
EPS = 1e-6
NEG = -1e30
BIG = 1e9
PAGE = 128

LANES = 128
SUBLANES = 8
QT = 128
PROJ_W = 2560
VMEM_LIMIT = 56 * 1024 * 1024


def _cparams(sem):
    return pltpu.CompilerParams(dimension_semantics=sem, vmem_limit_bytes=VMEM_LIMIT)


def _dot(a, b):
    return jnp.dot(a, b, preferred_element_type=f32)


def _dot_nt(a, b):
    return lax.dot_general(a, b, (((1,), (1,)), ((), ())), preferred_element_type=f32)


def _dot_tn(a, b):
    return lax.dot_general(a, b, (((0,), (0,)), ((), ())), preferred_element_type=f32)


def _dot_hilo(a, b_bf16):
    hi = a.astype(bf16)
    lo = (a - hi.astype(f32)).astype(bf16)
    return _dot(hi, b_bf16) + _dot(lo, b_bf16)


def _pair_rmsnorm(y, gain):
    lane = lax.broadcasted_iota(jnp.int32, y.shape, y.ndim - 1)
    lo = lane < HEAD_DIM
    y2 = y * y
    s_lo = jnp.sum(jnp.where(lo, y2, 0.0), axis=-1, keepdims=True)
    s_hi = jnp.sum(jnp.where(lo, 0.0, y2), axis=-1, keepdims=True)
    ms = jnp.where(lo, s_lo, s_hi) * (1.0 / HEAD_DIM)
    return y * lax.rsqrt(ms + EPS) * gain


_C_Q, _C_CK, _C_CV, _C_SK, _C_SV, _C_WK, _C_WV = 0, 512, 640, 768, 896, 1024, 1152
_C_GATE, _C_GQ, _C_GK, _C_GV, _C_GR, _C_GLR, _C_PU = 1280, 1408, 1536, 1664, 1920, 2176, 2304


def _proj_kernel(x_ref, g_ref, w_ref, qg_ref, kg_ref, wa_ref, ba_ref,
                 q_o, ckv_o, skv_o, wkv_o, gt_o, gq_o, gk_o, gv_o, gr_o, gl_o, pu_o):
    x = x_ref[...]
    hn = (x * lax.rsqrt(jnp.mean(x * x, axis=-1, keepdims=True) + EPS) * g_ref[...]).astype(bf16)

    def proj(c0, n):
        return _dot(hn, w_ref[:, c0:c0 + n])

    for j in range(4):
        q_o[:, j * LANES:(j + 1) * LANES] = _pair_rmsnorm(proj(_C_Q + j * LANES, LANES), qg_ref[...]).astype(bf16)
    ckv_o[...] = proj(_C_CK, 2 * LANES)
    skv_o[:, :LANES] = _pair_rmsnorm(proj(_C_SK, LANES), kg_ref[1:2, :])
    skv_o[:, LANES:] = proj(_C_SV, LANES)
    wkv_o[:, :LANES] = _pair_rmsnorm(proj(_C_WK, LANES), kg_ref[2:3, :])
    wkv_o[:, LANES:] = proj(_C_WV, LANES)
    gt_o[...] = jax.nn.sigmoid(proj(_C_GATE, LANES))
    gq_o[...] = proj(_C_GQ, LANES)
    gk_o[...] = proj(_C_GK, LANES)
    gv_o[...] = proj(_C_GV, 2 * LANES)
    gr_o[...] = proj(_C_GR, 2 * LANES)
    z = _dot(proj(_C_GLR, LANES).astype(bf16), wa_ref[...]) + ba_ref[...]
    gl_o[...] = jax.nn.log_sigmoid(z) * (1.0 / GLA_TAU)
    pu_o[...] = proj(_C_PU, 2 * LANES)


def _in_proj(x, lw, tm):
    m = x.shape[0]
    row = lambda n: pl.BlockSpec((tm, n), lambda i: (i, 0))
    full = lambda a: pl.BlockSpec(a.shape, lambda i: (0,) * a.ndim)
    widths = (512, 256, 256, 256, 128, 128, 128, 256, 256, 128, 256)
    dtypes = (bf16,) + (f32,) * 10
    args = (x, lw['g_mix'], lw['w_in'], lw['q_gain'], lw['k_gain'], lw['w_alpha'], lw['b_alpha'])
    return pl.pallas_call(
        _proj_kernel,
        grid=(m // tm,),
        in_specs=[row(D_MODEL)] + [full(a) for a in args[1:]],
        out_specs=[row(n) for n in widths],
        out_shape=[jax.ShapeDtypeStruct((m, n), dt) for n, dt in zip(widths, dtypes)],
        compiler_params=_cparams(("parallel",)),
        name="in_proj",
    )(*args)


def _compress(get_x, w_ref, pe_ref, w2_ref):
    acc = jnp.zeros((LANES + 2 * SUBLANES, 4 * LANES), f32)
    for s in range(CMP_STRIDE):
        xs = jnp.concatenate([get_x(s), pe_ref[s]], axis=0).astype(bf16)
        acc = acc + _dot(xs, w_ref[s])
    a0 = acc[:LANES, :2 * LANES]
    a1 = acc[:LANES, 2 * LANES:]
    c0 = acc[LANES:LANES + 1, :2 * LANES] + acc[LANES + SUBLANES:LANES + SUBLANES + 1, 2 * LANES:]
    h = a0 + pltpu.roll(a1, LANES - 1, 0) + c0
    return _dot(jax.nn.gelu(h).astype(bf16), w2_ref[...])


def _compress_prompt_kernel(ck_ref, cv_ref, wk_ref, wv_ref, pek_ref, pev_ref, w2k_ref, w2v_ref, kg_ref, kc_o, vc_o):
    kx = lambda s: ck_ref[pl.ds(s, LANES, stride=CMP_STRIDE), :]
    vx = lambda s: cv_ref[pl.ds(s, LANES, stride=CMP_STRIDE), :]
    kc_o[...] = _pair_rmsnorm(_compress(kx, wk_ref, pek_ref, w2k_ref), kg_ref[0:1, :])
    vc_o[...] = _compress(vx, wv_ref, pev_ref, w2v_ref)


def _compress_prompt(ckv, lw, nb, seq):
    full = lambda a: pl.BlockSpec(a.shape, lambda b: (0,) * a.ndim)
    kblk = pl.BlockSpec((seq, LANES), lambda b: (b, 0))
    vblk = pl.BlockSpec((seq, LANES), lambda b: (b, 1))
    outblk = pl.BlockSpec((None, LANES, LANES), lambda b: (b, 0, 0))
    ws = (lw['cmp_wk'], lw['cmp_wv'], lw['cmp_pek'], lw['cmp_pev'], lw['cmp_w2k'], lw['cmp_w2v'], lw['k_gain'])
    return pl.pallas_call(
        _compress_prompt_kernel,
        grid=(nb,),
        in_specs=[kblk, vblk] + [full(a) for a in ws],
        out_specs=[outblk, outblk],
        out_shape=[jax.ShapeDtypeStruct((nb, LANES, LANES), f32)] * 2,
        compiler_params=_cparams(("parallel",)),
        name="compress_prompt",
    )(ckv, ckv, *ws)


def _rank_select(score, n_blocks):
    blk = lax.broadcasted_iota(jnp.int32, score.shape, 1)
    rank = jnp.zeros(score.shape, f32)
    for m in range(n_blocks):
        sm = score[:, m:m + 1]
        beats = (sm > score) | ((sm == score) & (m < blk))
        rank = rank + jnp.where(beats, 1.0, 0.0)
    return jnp.where(rank < float(N_SEL), 1.0, 0.0)


def _rank_select_rows(score, n_blocks):
    blk = lax.broadcasted_iota(jnp.int32, score.shape, 0)
    rank = jnp.zeros(score.shape, f32)
    for m in range(n_blocks):
        sm = score[m:m + 1, :]
        beats = (sm > score) | ((sm == score) & (m < blk))
        rank = rank + jnp.where(beats, 1.0, 0.0)
    return jnp.where(rank < float(N_SEL), 1.0, 0.0)


def _nsa_prompt_kernel(q_ref, kc_ref, vc_ref, sk_ref, sv_ref, wk_ref, wv_ref, gt_ref,
                       td_ref, ts_ref, cf_ref, we_ref, bc_ref, ovt_ref, eet_ref, o_ref,
                       selx_ref, m_ref, l_ref, acc_ref):
    i = pl.program_id(1)
    lane = lax.broadcasted_iota(jnp.int32, (QT, LANES), 1)
    row = lax.broadcasted_iota(jnp.int32, (QT, LANES), 0)
    n_blk = selx_ref.shape[1] // SEL_BLOCK
    blk = lax.broadcasted_iota(jnp.int32, (n_blk, QT), 0)
    cur = (i * QT + lax.broadcasted_iota(jnp.int32, (n_blk, QT), 1)) >> 6
    gates_t = gt_ref[...].T
    outs = [[None, None] for _ in range(GQA)]

    def tile4(x):
        return jnp.concatenate([x] * GQA, axis=1)

    def scores(q4, k_ref, kt, bias, sel_head=None, live=None):
        off = pl.multiple_of(kt * QT, QT)
        s = _dot_nt(k_ref[pl.ds(off, QT), :].astype(bf16), q4) * ATT_SCALE + bias
        if live is not None:
            s = s + jnp.where(live, 0.0, NEG)
        if sel_head is not None:
            s = s + tile4(selx_ref[sel_head, pl.ds(off, QT), :])
        return s, off

    def attend(v_ref, tiles, state=None):
        m = state[0] if state is not None else None
        for s, _ in tiles:
            ms = jnp.max(s, axis=0, keepdims=True)
            m = ms if m is None else jnp.maximum(m, ms)
        l, acc = None, None
        for s, off in tiles:
            p = jnp.exp(s - m)
            pv = _dot_tn(v_ref[pl.ds(off, QT), :].astype(bf16), p.astype(bf16))
            ps = jnp.sum(p, axis=0, keepdims=True)
            l = ps if l is None else l + ps
            acc = pv if acc is None else acc + pv
        if state is not None:
            alpha = jnp.exp(state[0] - m)
            l = l + alpha * state[1]
            acc = acc + alpha * state[2]
        return m, l, acc

    m_ref[...] = jnp.full(m_ref.shape, NEG, f32)
    l_ref[...] = jnp.zeros(l_ref.shape, f32)
    acc_ref[...] = jnp.zeros(acc_ref.shape, f32)

    q4s, fars, o_cmps = [], [], []
    for h in range(N_KV):
        half = (lane >= HEAD_DIM) if h else (lane < HEAD_DIM)
        q4 = jnp.concatenate(
            [jnp.where(half, q_ref[:, g * LANES:(g + 1) * LANES], jnp.zeros((), bf16)) for g in range(GQA)], axis=0)
        q4s.append(q4)
        fars.append(cf_ref[h, 0:1, :])
        bias_c = bc_ref[h, pl.ds(pl.multiple_of(LANES - SUBLANES * i, SUBLANES), LANES), :]
        s = _dot_nt(kc_ref[...].astype(bf16), q4) * ATT_SCALE + bias_c
        e = jnp.exp(s - jnp.max(s, axis=0, keepdims=True))
        p = jnp.where(s > 0.5 * NEG, e / jnp.sum(e, axis=0, keepdims=True), 0.0)
        o_cmps.append(_dot_tn(vc_ref[...].astype(bf16), p.astype(bf16)))
        psum = p[:, 0:QT] + p[:, QT:2 * QT] + p[:, 2 * QT:3 * QT] + p[:, 3 * QT:4 * QT]
        hi = psum.astype(bf16)
        lo = (psum - hi.astype(f32)).astype(bf16)
        imp = (_dot(ovt_ref[...], hi) + _dot(ovt_ref[...], lo))[0:n_blk, :]
        forced = (blk == 0) | (blk == cur) | (blk == cur - 1)
        score = jnp.where(forced, BIG, jnp.where(blk <= cur, imp, -BIG))
        sel = _rank_select_rows(score, n_blk)
        sel = jnp.concatenate([sel, jnp.zeros((LANES - n_blk, QT), f32)], axis=0).astype(bf16)
        selx_ref[h] = (_dot(eet_ref[...], sel) - 1.0) * (-NEG)

    n_far = jnp.maximum(i - 1, 0)

    def far_pairs(j, carry):
        kt1 = 2 * j + 1
        tiles = [[scores(q4s[h], sk_ref, 2 * j, fars[h], sel_head=h),
                  scores(q4s[h], sk_ref, jnp.minimum(kt1, i), fars[h], sel_head=h, live=kt1 < n_far)]
                 for h in range(N_KV)]
        for h in range(N_KV):
            m_ref[h], l_ref[h], acc_ref[h] = attend(sv_ref, tiles[h], (m_ref[h], l_ref[h], acc_ref[h]))
        return carry

    lax.fori_loop(0, (n_far + 1) // 2, far_pairs, 0)

    def back_tile(back):
        return dict(kt=jnp.maximum(i - back, 0), live=(i >= back) if back else None)

    slc_tiles = [[scores(q4s[h], sk_ref, bias=tab_ref[h], sel_head=h, **back_tile(back))
                  for back, tab_ref in ((1, ts_ref), (0, td_ref))] for h in range(N_KV)]
    win_tiles = [[scores(q4s[h], wk_ref, bias=bias, **back_tile(back))
                  for back, bias in ((4, we_ref[h]), (3, fars[h]), (2, fars[h]), (1, ts_ref[h]), (0, td_ref[h]))]
                 for h in range(N_KV)]
    for h in range(N_KV):
        _, l_s, acc_s = attend(sv_ref, slc_tiles[h], (m_ref[h], l_ref[h], acc_ref[h]))
        _, l_w, acc_w = attend(wv_ref, win_tiles[h])
        o_slc = acc_s / l_s
        o_win = acc_w / l_w
        for g in range(GQA):
            c = (h * GQA + g) * 3
            sl = slice(g * QT, (g + 1) * QT)
            outs[g][h] = (gates_t[c:c + 1, :] * o_cmps[h][:, sl] + gates_t[c + 1:c + 2, :] * o_slc[:, sl]
                          + gates_t[c + 2:c + 3, :] * o_win[:, sl])
    for g in range(GQA):
        o_ref[:, g * LANES:(g + 1) * LANES] = jnp.where(row < HEAD_DIM, outs[g][0], outs[g][1]).T.astype(bf16)


def _nsa_prompt(q, kc, vc, skv, wkv, gt, tabs, nb, seq):
    nt = seq // QT
    tile = lambda n: pl.BlockSpec((QT, n), lambda b, i: (b * nt + i, 0))
    kblk = pl.BlockSpec((seq, LANES), lambda b, i: (b, 0))
    vblk = pl.BlockSpec((seq, LANES), lambda b, i: (b, 1))
    cblk = pl.BlockSpec((None, LANES, LANES), lambda b, i: (b, 0, 0))
    full = lambda a: pl.BlockSpec(a.shape, lambda b, i: (0,) * a.ndim)
    consts = (tabs['td'], tabs['ts'], tabs['cf'], tabs['we'], tabs['bc'], tabs['ovt'], tabs['eet'])
    return pl.pallas_call(
        _nsa_prompt_kernel,
        grid=(nb, nt),
        in_specs=[tile(512), cblk, cblk, kblk, vblk, kblk, vblk, tile(LANES)] + [full(a) for a in consts],
        out_specs=tile(512),
        out_shape=jax.ShapeDtypeStruct((nb * seq, 512), bf16),
        scratch_shapes=[pltpu.VMEM((N_KV, seq, QT), f32), pltpu.VMEM((N_KV, 1, GQA * QT), f32),
                        pltpu.VMEM((N_KV, 1, GQA * QT), f32), pltpu.VMEM((N_KV, LANES, GQA * QT), f32)],
        compiler_params=_cparams(("parallel", "arbitrary")),
        name="nsa_prompt",
    )(q, kc, vc, skv, skv, wkv, wkv, gt, *consts)


N_PAGES = 16


def _nsa_sample_kernel(pt_ref, *refs):
    ckp = refs[0:N_PAGES]
    cvp = refs[N_PAGES:2 * N_PAGES]
    slc_pages = refs[2 * N_PAGES:3 * N_PAGES]
    (win_ref, qr_ref, gr_ref, knew_ref, wcat_k, wcat_v, pek_ref, pev_ref, w2k_ref, w2v_ref, kg_ref,
     bc_ref, bs_ref, bw_ref, b0_ref, ov_ref, o_ref, win_o) = refs[3 * N_PAGES:]
    lane = lax.broadcasted_iota(jnp.int32, (SUBLANES, LANES), 1)

    def hist(pages):
        return lambda s: jnp.concatenate([pg[pl.ds(s, SUBLANES, stride=CMP_STRIDE), :] for pg in pages], axis=0)

    kc = _pair_rmsnorm(_compress(hist(ckp), wcat_k, pek_ref, w2k_ref), kg_ref[0:1, :])
    vc = _compress(hist(cvp), wcat_v, pev_ref, w2v_ref)
    qf = qr_ref[...]
    qb = qf.astype(bf16)
    gates = gr_ref[...]
    knew = knew_ref[...]
    hi_rows = lax.broadcasted_iota(jnp.int32, (SUBLANES, LANES), 0) >= GQA
    own = (lane >= HEAD_DIM) == hi_rows

    s = _dot_nt(qb, kc.astype(bf16)) * ATT_SCALE + bc_ref[...]
    mskc = lane < N_CMP
    s = jnp.where(mskc, s, NEG)
    e = jnp.exp(s - jnp.max(s, axis=-1, keepdims=True))
    p = jnp.where(mskc, e / jnp.sum(e, axis=-1, keepdims=True), 0.0)
    o_cmp = _dot(p.astype(bf16), vc.astype(bf16))
    psum = jnp.broadcast_to(jnp.sum(p.reshape(N_KV, GQA, LANES), axis=1, keepdims=True), (N_KV, GQA, LANES))
    imp = _dot_hilo(psum.reshape(SUBLANES, LANES), ov_ref[...])
    n_blk = N_PAGES * PAGE // SEL_BLOCK + 1
    cur = n_blk - 1
    forced = (lane == 0) | (lane == cur) | (lane == cur - 1)
    score = jnp.where(forced, BIG, imp)
    score = jnp.where(lane < n_blk, score, -3e38)
    sel = _rank_select(score, n_blk)

    def softmax_av(s_parts, s_new, vt_parts, v_new):
        mx = s_new
        for sp in s_parts:
            mx = jnp.maximum(mx, jnp.max(sp, axis=-1, keepdims=True))
        p_new = jnp.exp(s_new - mx)
        den = p_new
        acc = p_new * v_new
        for sp, vt in zip(s_parts, vt_parts):
            pp = jnp.exp(sp - mx)
            den = den + jnp.sum(pp, axis=-1, keepdims=True)
            acc = acc + _dot_nt(pp.astype(bf16), vt.astype(bf16))
        return acc / den

    def new_row(r):
        return knew[r:r + 1, :]

    def score_new(krow):
        return jnp.sum(jnp.where(own, qf * krow, 0.0), axis=-1, keepdims=True) * ATT_SCALE + b0_ref[:, 0:1]

    s_parts = []
    for j in range(N_PAGES):
        sj = _dot(qb, slc_pages[j][0].astype(bf16)) * ATT_SCALE + bs_ref[:, j * PAGE:(j + 1) * PAGE]
        mj = jnp.where(lane < SEL_BLOCK, sel[:, 2 * j:2 * j + 1], sel[:, 2 * j + 1:2 * j + 2]) > 0.5
        s_parts.append(jnp.where(mj, sj, NEG))
    o_slc = softmax_av(s_parts, score_new(new_row(0)), [pg[1] for pg in slc_pages], new_row(1))

    w_parts = []
    for j in range(WINDOW // PAGE):
        kw = win_ref[0, :, j * PAGE:(j + 1) * PAGE]
        sj = _dot(qb, kw.astype(bf16)) * ATT_SCALE + bw_ref[:, j * PAGE:(j + 1) * PAGE]
        if j == 0:
            sj = jnp.where(lane >= 1, sj, NEG)
        w_parts.append(sj)
    vt_parts = [win_ref[1, :, j * PAGE:(j + 1) * PAGE] for j in range(WINDOW // PAGE)]
    o_win = softmax_av(w_parts, score_new(new_row(2)), vt_parts, new_row(3))

    o_ref[...] = gates[:, 0:1] * o_cmp + gates[:, 1:2] * o_slc + gates[:, 2:3] * o_win

    new_cols = knew.T
    last = lax.broadcasted_iota(jnp.int32, (LANES, WINDOW), 1) == WINDOW - 1
    for kv in range(2):
        win_o[kv] = jnp.where(last, new_cols[:, 2 + kv:3 + kv], pltpu.roll(win_ref[kv], WINDOW - 1, 1))


def _nsa_sample(layer, page_table, cache_cmp, cache_slc, win_state, qrows, grows, knew, lw, tabs):
    nb = page_table.shape[0]

    def page(j, half):
        return pl.BlockSpec((None, None, PAGE, LANES), lambda b, pt: (layer, pt[b, j], 0, half))

    def page_t(j):
        return pl.BlockSpec((None, None, 2, LANES, PAGE), lambda b, pt: (layer, pt[b, j], 0, 0, 0))

    full = lambda a: pl.BlockSpec(a.shape, lambda b, pt: (0,) * a.ndim)
    per_b = pl.BlockSpec((None, SUBLANES, LANES), lambda b, pt: (b, 0, 0))
    consts = (lw['cmp_wk'], lw['cmp_wv'], lw['cmp_pek'], lw['cmp_pev'], lw['cmp_w2k'], lw['cmp_w2v'], lw['k_gain'],
              tabs['bc_s'], tabs['bs_s'], tabs['bw_s'], tabs['b0_s'], tabs['ov_s'])
    in_specs = ([page(j, 0) for j in range(N_PAGES)] + [page(j, 1) for j in range(N_PAGES)]
                + [page_t(j) for j in range(N_PAGES)]
                + [pl.BlockSpec((None, None, 2, LANES, WINDOW), lambda b, pt: (layer, b, 0, 0, 0)), per_b, per_b, per_b]
                + [full(a) for a in consts])
    return pl.pallas_call(
        _nsa_sample_kernel,
        grid_spec=pltpu.PrefetchScalarGridSpec(
            num_scalar_prefetch=1, grid=(nb,), in_specs=in_specs,
            out_specs=[per_b, pl.BlockSpec((None, 2, LANES, WINDOW), lambda b, pt: (b, 0, 0, 0))]),
        out_shape=[jax.ShapeDtypeStruct((nb, SUBLANES, LANES), f32),
                   jax.ShapeDtypeStruct((nb, 2, LANES, WINDOW), f32)],
        compiler_params=_cparams(("parallel",)),
        name="nsa_sample",
    )(page_table, *([cache_cmp] * (2 * N_PAGES)), *([cache_slc] * N_PAGES), win_state, qrows, grows, knew, *consts)


def _gla_out(o, gn_ref, gr):
    o = jnp.concatenate([_pair_rmsnorm(o[:, :LANES], gn_ref[:, :LANES]),
                         _pair_rmsnorm(o[:, LANES:], gn_ref[:, LANES:])], axis=1)
    return o * jax.nn.silu(gr)


def _gla_prompt_kernel(gq_ref, gk_ref, gl_ref, gv_ref, gr_ref, gn_ref, o_ref, st_o, st_ref):
    c_rows = GLA_CHUNK
    lane = lax.broadcasted_iota(jnp.int32, (c_rows, LANES), 1)
    row = lax.broadcasted_iota(jnp.int32, (c_rows, LANES), 0)
    tril = (lax.broadcasted_iota(jnp.int32, (c_rows, c_rows), 0)
            >= lax.broadcasted_iota(jnp.int32, (c_rows, c_rows), 1))
    lane_v = lax.broadcasted_iota(jnp.int32, (c_rows, GLA_HEADS * GLA_DV), 1)
    srow = lax.broadcasted_iota(jnp.int32, (GLA_HEADS * GLA_DV, LANES), 0)
    scol = lax.broadcasted_iota(jnp.int32, (GLA_HEADS * GLA_DV, LANES), 1)
    diag = (srow >> 6) == (scol >> 5)
    st_ref[...] = jnp.zeros(st_ref.shape, f32)

    def body(c, carry):
        off = pl.multiple_of(c * c_rows, c_rows)
        q = gq_ref[pl.ds(off, c_rows), :] * (GLA_DK ** -0.5)
        k = gk_ref[pl.ds(off, c_rows), :]
        v = gv_ref[pl.ds(off, c_rows), :]
        b = gl_ref[pl.ds(off, c_rows), :]
        sh = 1
        while sh < c_rows:
            b = b + jnp.where(row >= sh, pltpu.roll(b, sh, 0), 0.0)
            sh *= 2
        qd = q * jnp.exp(b)
        kd = (k * jnp.exp(-b)).astype(bf16)
        bl = b[c_rows - 1:c_rows, :]
        kk = (k * jnp.exp(bl - b)).astype(bf16)
        st = st_ref[...]
        vb = v.astype(bf16)
        o = _dot_nt(qd.astype(bf16), st.astype(bf16))
        for hh in range(GLA_HEADS):
            qm = jnp.where((lane >> 5) == hh, qd, 0.0).astype(bf16)
            a = jnp.where(tril, _dot_nt(qm, kd), 0.0)
            o = o + jnp.where((lane_v >> 6) == hh, _dot(a.astype(bf16), vb), 0.0)
        st_ref[...] = jnp.exp(bl) * st + jnp.where(diag, _dot_tn(vb, kk), 0.0)
        o_ref[pl.ds(off, c_rows), :] = _gla_out(o, gn_ref, gr_ref[pl.ds(off, c_rows), :]).astype(bf16)
        return carry

    lax.fori_loop(0, gq_ref.shape[0] // c_rows, body, 0)
    st_o[...] = st_ref[...]


def _gla_prompt(gq, gk, gl, gv, gr, gn, nb, seq):
    blk = lambda n: pl.BlockSpec((seq, n), lambda b: (b, 0))
    return pl.pallas_call(
        _gla_prompt_kernel,
        grid=(nb,),
        in_specs=[blk(LANES), blk(LANES), blk(LANES), blk(2 * LANES), blk(2 * LANES),
                  pl.BlockSpec(gn.shape, lambda b: (0, 0))],
        out_specs=[blk(2 * LANES), pl.BlockSpec((None, GLA_HEADS * GLA_DV, LANES), lambda b: (b, 0, 0))],
        out_shape=[jax.ShapeDtypeStruct((nb * seq, 2 * LANES), bf16),
                   jax.ShapeDtypeStruct((nb, GLA_HEADS * GLA_DV, LANES), f32)],
        scratch_shapes=[pltpu.VMEM((GLA_HEADS * GLA_DV, LANES), f32)],
        compiler_params=_cparams(("parallel",)),
        name="gla_prompt",
    )(gq, gk, gl, gv, gr, gn)


def _gla_sample_kernel(cols_ref, v_ref, s0_ref, gr_ref, gn_ref, o_ref, s_o):
    nb = s0_ref.shape[0]
    c = cols_ref[...]
    q = c[:, :, 0:1] * (GLA_DK ** -0.5)
    k = c[:, :, 1:2]
    g = c[:, :, 2:3]
    eg = jnp.exp(g)
    qd = q * eg
    kd = k * jnp.exp(-g)
    s0 = s0_ref[...]
    v = v_ref[...]
    a = jnp.sum((qd * kd).reshape(nb, GLA_HEADS, GLA_DK, 1), axis=2)
    o = jnp.sum((qd * s0).reshape(nb, GLA_HEADS, GLA_DK, GLA_DV), axis=2) + a * v
    vexp = jnp.broadcast_to(v[:, :, None, :], (nb, GLA_HEADS, GLA_DK, GLA_DV)).reshape(nb, GLA_HEADS * GLA_DK, GLA_DV)
    s_o[...] = eg * s0 + k * vexp
    y = o * lax.rsqrt(jnp.mean(o * o, axis=-1, keepdims=True) + EPS) * gn_ref[...]
    o_ref[...] = y * jax.nn.silu(gr_ref[...])


def _gla_sample(cols, v, s0, gr, gn, bt=8):
    nb = s0.shape[0]
    blk = lambda a: pl.BlockSpec((bt,) + a.shape[1:], lambda i: (i,) + (0,) * (a.ndim - 1))
    return pl.pallas_call(
        _gla_sample_kernel,
        grid=(nb // bt,),
        in_specs=[blk(cols), blk(v), blk(s0), blk(gr), pl.BlockSpec(gn.shape, lambda i: (0, 0))],
        out_specs=[blk(v), blk(s0)],
        out_shape=[jax.ShapeDtypeStruct(v.shape, f32), jax.ShapeDtypeStruct(s0.shape, f32)],
        compiler_params=_cparams(("parallel",)),
        name="gla_sample",
    )(cols, v, s0, gr, gn)


def _pool_finish(sums, u, cnt_inv, w_ref, sc_ref):
    lane = lax.broadcasted_iota(jnp.int32, u.shape, 1)
    grp = lane >> 6
    s = jnp.where(grp == 0, sums[0], jnp.where(grp == 1, sums[1], jnp.where(grp == 2, sums[2], sums[3])))
    dlt = s * cnt_inv - u
    return _dot(dlt.astype(bf16), w_ref[...]) * sc_ref[...]


def _pool_prompt_kernel(u_ref, w_ref, sc_ref, o_ref):
    u = u_ref[...]
    row = lax.broadcasted_iota(jnp.int32, u.shape, 0)
    lane = lax.broadcasted_iota(jnp.int32, u.shape, 1)
    sums = []
    s = u
    sh = 1
    while sh < POOL_WINDOWS[-1]:
        s = s + jnp.where(row >= sh, pltpu.roll(s, sh, 0), 0.0)
        sums.append(s)
        sh *= 2
    grp = lane >> 6
    win = jnp.where(grp == 0, POOL_WINDOWS[0], jnp.where(grp == 1, POOL_WINDOWS[1],
                    jnp.where(grp == 2, POOL_WINDOWS[2], POOL_WINDOWS[3])))
    cnt = jnp.minimum(win, row + 1).astype(f32)
    o_ref[...] = _pool_finish(sums, u, 1.0 / cnt, w_ref, sc_ref).astype(bf16)


def _pool_prompt(pu, w, sc, nb, seq):
    blk = pl.BlockSpec((seq, 2 * LANES), lambda b: (b, 0))
    full = lambda a: pl.BlockSpec(a.shape, lambda b: (0,) * a.ndim)
    return pl.pallas_call(
        _pool_prompt_kernel,
        grid=(nb,),
        in_specs=[blk, full(w), full(sc)],
        out_specs=blk,
        out_shape=jax.ShapeDtypeStruct((nb * seq, 2 * LANES), bf16),
        compiler_params=_cparams(("parallel",)),
        name="pool_prompt",
    )(pu, w, sc)


def _pool_sample_kernel(u_ref, buf_ref, w_ref, sc_ref, o_ref):
    u = u_ref[...]
    lane = lax.broadcasted_iota(jnp.int32, u.shape, 1)
    sums = []
    s = u
    nxt = POOL_BUF - 1
    for win in POOL_WINDOWS:
        while POOL_BUF - nxt < win:
            s = s + buf_ref[nxt]
            nxt -= 1
        sums.append(s)
    grp = lane >> 6
    cnt_inv = jnp.where(grp == 0, 1.0 / POOL_WINDOWS[0], jnp.where(grp == 1, 1.0 / POOL_WINDOWS[1],
                        jnp.where(grp == 2, 1.0 / POOL_WINDOWS[2], 1.0 / POOL_WINDOWS[3])))
    o_ref[...] = _pool_finish(sums, u, cnt_inv, w_ref, sc_ref).astype(bf16)


def _pool_sample(pu, buf_t, w, sc):
    full = lambda a: pl.BlockSpec(a.shape, lambda i: (0,) * a.ndim)
    return pl.pallas_call(
        _pool_sample_kernel,
        grid=(1,),
        in_specs=[full(pu), full(buf_t), full(w), full(sc)],
        out_specs=full(pu),
        out_shape=jax.ShapeDtypeStruct(pu.shape, bf16),
        compiler_params=_cparams(("arbitrary",)),
        name="pool_sample",
    )(pu, buf_t, w, sc)


def _out_proj_kernel(a_ref, b_ref, c_ref, x_ref, w_ref, g_ref, x_o, h_o):
    y = (_dot(a_ref[...], w_ref[0:512, :]) + _dot(b_ref[...], w_ref[512:768, :])
         + _dot(c_ref[...], w_ref[768:1024, :]))
    x = x_ref[...] + y
    x_o[...] = x
    h_o[...] = (x * lax.rsqrt(jnp.mean(x * x, axis=-1, keepdims=True) + EPS) * g_ref[...]).astype(bf16)


def _out_proj(o_nsa, o_gla, o_pool, x, w, g, tm):
    m = x.shape[0]
    row = lambda n: pl.BlockSpec((tm, n), lambda i: (i, 0))
    full = lambda a: pl.BlockSpec(a.shape, lambda i: (0,) * a.ndim)
    return pl.pallas_call(
        _out_proj_kernel,
        grid=(m // tm,),
        in_specs=[row(512), row(256), row(256), row(D_MODEL), full(w), full(g)],
        out_specs=[row(D_MODEL), row(D_MODEL)],
        out_shape=[jax.ShapeDtypeStruct((m, D_MODEL), f32), jax.ShapeDtypeStruct((m, D_MODEL), bf16)],
        compiler_params=_cparams(("parallel",)),
        name="out_proj",
    )(o_nsa, o_gla, o_pool, x, w, g)


FF_CHUNK = 256


def _ffn_kernel(tiles_per_seq, decode, h_ref, p_ref, x_ref, wa_ref, wu_ref, cw_ref, cb_ref, wd_ref, x_o, a_o):
    hn = h_ref[...]
    tm = hn.shape[0]
    row = lax.broadcasted_iota(jnp.int32, (tm, FF_CHUNK), 0)
    if not decode:
        first = (pl.program_id(0) % tiles_per_seq) == 0
        halo = p_ref[...]
    acc = jnp.zeros((tm, D_MODEL), f32)
    for c in range(D_FF // FF_CHUNK):
        cs = slice(c * FF_CHUNK, (c + 1) * FF_CHUNK)
        a = _dot(hn, wa_ref[:, cs])
        u = _dot(hn, wu_ref[:, cs])
        if decode:
            a2 = p_ref[0, :, cs]
            a1 = p_ref[1, :, cs]
            a_o[:, cs] = a
        else:
            ah = jnp.where(first, 0.0, _dot(halo, wa_ref[:, cs]))
            p1 = ah[SUBLANES - 1:SUBLANES, :]
            p2 = ah[SUBLANES - 2:SUBLANES - 1, :]
            a1 = jnp.where(row == 0, p1, pltpu.roll(a, 1, 0))
            a2 = jnp.where(row == 0, p2, jnp.where(row == 1, p1, pltpu.roll(a, 2, 0)))
            a_o[:, cs] = a[tm - SUBLANES:, :]
        ac = cb_ref[:, cs] + a2 * cw_ref[0:1, cs] + a1 * cw_ref[1:2, cs] + a * cw_ref[2:3, cs]
        act = (jax.nn.silu(ac) * u).astype(bf16)
        acc = acc + _dot(act, wd_ref[cs, :])
    x_o[...] = x_ref[...] + acc


def _ffn(hn, prev, x, lw, tm, tiles_per_seq, decode):
    m = x.shape[0]
    nt = m // tm
    row = lambda n: pl.BlockSpec((tm, n), lambda i: (i, 0))
    const = lambda a: pl.BlockSpec(a.shape, lambda i: (0,) * a.ndim, pipeline_mode=pl.Buffered(1))
    if decode:
        prev_spec = pl.BlockSpec(prev.shape, lambda i: (0, 0, 0))
        a_spec = row(D_FF)
        a_shape = jax.ShapeDtypeStruct((m, D_FF), f32)
    else:
        per = tm // SUBLANES
        prev_spec = pl.BlockSpec((SUBLANES, D_MODEL), lambda i: (jnp.maximum(i * per - 1, 0), 0))
        a_spec = pl.BlockSpec((None, SUBLANES, D_FF), lambda i: (i, 0, 0))
        a_shape = jax.ShapeDtypeStruct((nt, SUBLANES, D_FF), f32)
    ws = (lw['w_up_a'], lw['w_up_u'], lw['conv_w'], lw['conv_b'], lw['w_down'])
    return pl.pallas_call(
        functools.partial(_ffn_kernel, tiles_per_seq, decode),
        grid=(nt,),
        in_specs=[row(D_MODEL), prev_spec, row(D_MODEL)] + [const(a) for a in ws],
        out_specs=[row(D_MODEL), a_spec],
        out_shape=[jax.ShapeDtypeStruct((m, D_MODEL), f32), a_shape],
        compiler_params=_cparams(("parallel",)),
        name="ffn_decode" if decode else "ffn_prompt",
    )(hn, prev, x, *ws)


def _rel_bucket_np(d):
    d = np.maximum(d, 0)
    exact = REL_BUCKETS // 2
    lg = np.log(np.maximum(d, 1).astype(np.float32) / np.float32(exact)) / np.float32(math.log(REL_MAX_DIST / exact))
    large = np.minimum(exact + (lg * np.float32(REL_BUCKETS - exact)).astype(np.int32), REL_BUCKETS - 1)
    return np.where(d < exact, d, large).astype(np.int32)


def _bias_lookup(rel_bias, dist):
    bucket = _rel_bucket_np(np.asarray(dist)).reshape(-1)
    onehot = jnp.asarray((bucket[:, None] == np.arange(REL_BUCKETS)[None, :]).astype(np.int8)).astype(f32)
    out = jnp.dot(onehot, rel_bias, precision=lax.Precision.HIGHEST)
    return out.reshape(tuple(np.shape(dist)) + (N_HEADS,))


def _bias_tables(rel_bias, seq, past_len):
    c = np.arange(QT)[:, None]
    r = np.arange(QT)[None, :]

    def key_major(t):
        rows = t.shape[0]
        return t.reshape(rows, QT, N_KV, GQA).transpose(2, 0, 3, 1).reshape(N_KV, rows, GQA * QT)

    def masked(t, keep):
        return jnp.where(jnp.asarray(np.tile(keep, (1, GQA)))[None], t, NEG)

    far = jnp.broadcast_to(rel_bias[REL_BUCKETS - 1].reshape(N_KV, 1, GQA, 1), (N_KV, QT, GQA, QT))
    far = far.reshape(N_KV, QT, GQA * QT)
    td = masked(key_major(_bias_lookup(rel_bias, r - c)), c <= r)
    ts = key_major(_bias_lookup(rel_bias, QT + r - c))
    we = masked(far, c > r)
    cf = far[:, :SUBLANES, :]
    j = np.arange(2 * QT)[:, None]
    dist_c = r - (CMP_STRIDE * (j - QT) + CMP_LEN - 1)
    bc = masked(key_major(_bias_lookup(rel_bias, dist_c)), dist_c >= 0)
    cc = np.arange(LANES)[:, None]
    n = np.arange(LANES)[None, :]
    ov = ((cc * CMP_STRIDE < n * SEL_BLOCK + SEL_BLOCK) & (cc * CMP_STRIDE + CMP_LEN > n * SEL_BLOCK) & (cc < N_CMP))
    ovt = jnp.asarray((ov & (n < seq // SEL_BLOCK)).T, dtype=bf16)
    ov_s = jnp.asarray(ov & (n < past_len // SEL_BLOCK + 1), dtype=bf16)
    eet = jnp.asarray((np.arange(seq)[:, None] // SEL_BLOCK) == np.arange(LANES)[None, :], dtype=bf16)

    def look_rows(dist):
        return _bias_lookup(rel_bias, dist).T

    bc_s = look_rows(past_len - (np.arange(LANES) * CMP_STRIDE + CMP_LEN - 1))
    bs_s = look_rows(past_len - np.arange(past_len))
    bw_s = look_rows(WINDOW - np.arange(WINDOW))
    b0_s = look_rows(np.zeros((LANES,), np.int64))
    return dict(td=td, ts=ts, cf=cf, we=we, bc=bc, ovt=ovt, eet=eet,
                ov_s=ov_s, bc_s=bc_s, bs_s=bs_s, bw_s=bw_s, b0_s=b0_s)


def _pair_order(w, axis):
    shape = w.shape
    w = w.reshape(shape[:axis] + (N_KV, GQA, HEAD_DIM) + shape[axis + 1:])
    return jnp.swapaxes(w, axis, axis + 1).reshape(shape)


def _repack_w_in(w):
    z = lambda n: jnp.zeros((w.shape[0], n), w.dtype)
    return jnp.concatenate([
        _pair_order(w[:, :512], 1), w[:, 512:1280],
        w[:, 1280:1304], z(LANES - 24),
        w[:, 1304:2072],
        w[:, 2072:2088], z(LANES - GLA_RANK),
        w[:, 2088:2344]], axis=1)


def _block_diag(blocks):
    n, r, c = blocks.shape
    eye = jnp.eye(n, dtype=blocks.dtype)
    return (eye[:, None, :, None] * blocks[:, :, None, :]).reshape(n * r, n * c)


def _layer_weights(l, p):
    tile2 = lambda v: jnp.tile(v, 2)[None, :]
    w_in = _repack_w_in(p['w_in'][l]).astype(bf16)
    w_out = jnp.concatenate([_pair_order(p['w_out'][l][:512], 0), p['w_out'][l][512:]], axis=0).astype(bf16)
    k_gain = jnp.concatenate([jnp.tile(p['k_gain'][l], (1, 2)), jnp.zeros((SUBLANES - 3, LANES), f32)], axis=0)
    w_alpha = jnp.concatenate([p['gla_w_alpha'][l], jnp.zeros((LANES - GLA_RANK, LANES), f32)], axis=0).astype(bf16)

    def cmp_w(kv):
        w1 = p['cmp_w1'][l, kv]
        two = jax.vmap(lambda w: _block_diag(jnp.stack([w, w])))(w1)
        wcat = jnp.concatenate([two[:CMP_STRIDE], two[CMP_STRIDE:]], axis=-1).astype(bf16)
        pe = jnp.tile(p['cmp_pe'][l, kv], (1, 2))
        pe = jnp.concatenate([jnp.broadcast_to(pe[:CMP_STRIDE, None, :], (CMP_STRIDE, SUBLANES, LANES)),
                              jnp.broadcast_to(pe[CMP_STRIDE:, None, :], (CMP_STRIDE, SUBLANES, LANES))], axis=1)
        w2 = _block_diag(jnp.stack([p['cmp_w2'][l, kv]] * 2)).astype(bf16)
        return wcat, pe, w2

    wk, pek, w2k = cmp_w(0)
    wv, pev, w2v = cmp_w(1)
    w_up = p['w_ffn_up'][l].astype(bf16)
    conv_w = jnp.concatenate([p['ffn_conv_w'][l], jnp.zeros((SUBLANES - 3, D_FF), f32)], axis=0)
    return dict(
        g_mix=p['g_mix'][l][None, :], w_in=w_in, q_gain=tile2(p['q_gain'][l]), k_gain=k_gain,
        w_alpha=w_alpha, b_alpha=p['gla_b_alpha'][l][None, :],
        cmp_wk=wk, cmp_wv=wv, cmp_pek=pek, cmp_pev=pev, cmp_w2k=w2k, cmp_w2v=w2v,
        gla_norm=p['gla_norm'][l][None, :],
        pool_w=_block_diag(p['pool_w'][l]).astype(bf16), pool_scale=p['pool_scale'][l][None, :],
        w_out=w_out,
        g_ffn=p['g_ffn'][l][None, :],
        w_up_a=w_up[:, :D_FF], w_up_u=w_up[:, D_FF:], conv_w=conv_w, conv_b=p['ffn_conv_b'][l][None, :],
        w_down=p['w_ffn_down'][l].astype(bf16),
    )


def _prompt_layer(x, lw, tabs, nb, seq):
    tm = 512
    (q, ckv, skv, wkv, gt, gq, gk, gv, gr, gl, pu) = _in_proj(x, lw, tm)
    kc, vc = _compress_prompt(ckv, lw, nb, seq)
    o_nsa = _nsa_prompt(q, kc, vc, skv, wkv, gt, tabs, nb, seq)
    o_gla, st = _gla_prompt(gq, gk, gl, gv, gr, lw['gla_norm'], nb, seq)
    o_pool = _pool_prompt(pu, lw['pool_w'], lw['pool_scale'], nb, seq)
    x_mid, hn = _out_proj(o_nsa, o_gla, o_pool, x, lw['w_out'], lw['g_ffn'], tm)
    x_out, a_tail = _ffn(hn, hn, x_mid, lw, tm, seq // tm, False)
    win_tail = wkv.reshape(nb, seq, 2 * LANES)[:, seq - WINDOW:]
    st = st.reshape(nb, GLA_HEADS, GLA_DV, GLA_HEADS, GLA_DK)
    gla_state = jnp.stack([st[:, h, :, h, :] for h in range(GLA_HEADS)], axis=1).transpose(0, 1, 3, 2)
    pool_state = pu.reshape(nb, seq, 2 * LANES)[:, seq - POOL_BUF:]
    a_tail = a_tail.reshape(nb, seq // tm, SUBLANES, D_FF)
    conv_state = a_tail[:, -1, SUBLANES - 2:, :]
    return x_out, (ckv, skv, win_tail, gla_state, pool_state, conv_state)


def _sample_layer(l, x, lw, tabs, page_table, cache_cmp, cache_slc, win_all, state_gla, state_pool, state_conv):
    nb = x.shape[0]
    (q, ckv, skv, wkv, gt, gq, gk, gv, gr, gl, pu) = _in_proj(x, lw, nb)
    lane_hi = (np.arange(LANES) >= HEAD_DIM)
    own = jnp.asarray((lane_hi[None, :] == (np.arange(N_KV)[:, None] == 1)).astype(np.float32))
    qrows = (q.astype(f32).reshape(nb, 1, GQA, LANES) * own[None, :, None, :]).reshape(nb, N_HEADS, LANES)
    grows = jnp.pad(gt[:, :N_HEADS * 3].reshape(nb, N_HEADS, 3), ((0, 0), (0, 0), (0, LANES - 3)))
    knew = jnp.pad(jnp.concatenate([skv, wkv], axis=1).reshape(nb, 4, LANES), ((0, 0), (0, SUBLANES - 4), (0, 0)))
    o, win_next = _nsa_sample(l, page_table, cache_cmp, cache_slc, win_all, qrows, grows, knew, lw, tabs)
    o = o.reshape(nb, N_KV, GQA, N_KV, HEAD_DIM)
    o_nsa = jnp.stack([o[:, h, :, h, :] for h in range(N_KV)], axis=2).reshape(nb, GQA * LANES).astype(bf16)
    cols = jnp.stack([gq, gk, gl], axis=-1)
    o_gla, s_new = _gla_sample(cols, gv.reshape(nb, GLA_HEADS, GLA_DV),
                               state_gla[l].reshape(nb, GLA_HEADS * GLA_DK, GLA_DV),
                               gr.reshape(nb, GLA_HEADS, GLA_DV), lw['gla_norm'].reshape(GLA_HEADS, GLA_DV))
    o_gla = o_gla.reshape(nb, GLA_HEADS * GLA_DV).astype(bf16)
    o_pool = _pool_sample(pu, state_pool[l].transpose(1, 0, 2), lw['pool_w'], lw['pool_scale'])
    x_mid, hn = _out_proj(o_nsa, o_gla, o_pool, x, lw['w_out'], lw['g_ffn'], nb)
    x_out, a = _ffn(hn, state_conv[l].transpose(1, 0, 2), x_mid, lw, nb, 1, True)
    gla_state = s_new.reshape(nb, GLA_HEADS, GLA_DK, GLA_DV)
    return x_out, (ckv, skv, win_next, gla_state, pu, a)


def kernel(x_prompt, x_sample, cache_cmp_kv, cache_slc_kv, page_table, state_win_kv, state_gla, state_pool,
           state_ffn_conv, rel_bias, g_mix, w_in, q_gain, k_gain, cmp_pe, cmp_w1, cmp_w2, gla_w_alpha, gla_b_alpha,
           gla_norm, pool_w, pool_scale, w_out, g_ffn, w_ffn_up, ffn_conv_w, ffn_conv_b, w_ffn_down):
    nb, seq, _ = x_prompt.shape
    db = x_sample.shape[0]
    depth = w_in.shape[0]
    n_phys = cache_cmp_kv.shape[1]
    past_len = page_table.shape[1] * cache_cmp_kv.shape[2]
    assert (seq, past_len, page_table.shape[1], cache_cmp_kv.shape[2]) == (2048, 2048, N_PAGES, PAGE)
    assert state_win_kv.shape[2] == WINDOW and x_sample.shape[1] == 1
    params = dict(g_mix=g_mix, w_in=w_in, q_gain=q_gain, k_gain=k_gain, cmp_pe=cmp_pe, cmp_w1=cmp_w1, cmp_w2=cmp_w2,
                  gla_w_alpha=gla_w_alpha, gla_b_alpha=gla_b_alpha, gla_norm=gla_norm, pool_w=pool_w,
                  pool_scale=pool_scale, w_out=w_out, g_ffn=g_ffn, w_ffn_up=w_ffn_up, ffn_conv_w=ffn_conv_w,
                  ffn_conv_b=ffn_conv_b, w_ffn_down=w_ffn_down)
    tabs = _bias_tables(rel_bias, seq, past_len)
    cache_cmp = cache_cmp_kv.reshape(depth, n_phys, PAGE, 2 * LANES)
    cache_slc = cache_slc_kv.transpose(0, 1, 3, 4, 5, 2).reshape(depth, n_phys, 2, LANES, PAGE)
    win_all = state_win_kv.transpose(0, 1, 3, 4, 5, 2).reshape(depth, db, 2, LANES, WINDOW)
    xp = x_prompt.reshape(nb * seq, D_MODEL)
    xs = x_sample.reshape(db, D_MODEL)
    st_p = [[] for _ in range(6)]
    st_s = [[] for _ in range(6)]
    for l in range(depth):
        lw = _layer_weights(l, params)
        xp, new_p = _prompt_layer(xp, lw, tabs, nb, seq)
        xs, new_s = _sample_layer(l, xs, lw, tabs, page_table, cache_cmp, cache_slc, win_all, state_gla,
                                  state_pool, state_ffn_conv)
        for i in range(6):
            st_p[i].append(new_p[i])
            st_s[i].append(new_s[i])
    stk = lambda a: jnp.stack(a, axis=0)
    kv = lambda a, rows: a.reshape(depth, a.shape[1] // rows, rows, 2, N_KV, HEAD_DIM)
    outs_p = (kv(stk(st_p[0]), seq), kv(stk(st_p[1]), seq), kv(stk(st_p[2]).reshape(depth, nb * WINDOW, -1), WINDOW),
              stk(st_p[3]), stk(st_p[4]), stk(st_p[5]))
    shift_in = lambda old, new: jnp.concatenate([old[:, :, 1:], stk(new)[:, :, None, :]], axis=2)
    win_s = stk(st_s[2]).reshape(depth, db, 2, N_KV, HEAD_DIM, WINDOW).transpose(0, 1, 5, 2, 3, 4)
    outs_s = (kv(stk(st_s[0]), 1), kv(stk(st_s[1]), 1), win_s, stk(st_s[3]),
              shift_in(state_pool, st_s[4]), shift_in(state_ffn_conv, st_s[5]))
    return (xp.reshape(nb, seq, D_MODEL), xs.reshape(db, 1, D_MODEL), *outs_p, *outs_s)
```

```python
import functools
import math

import numpy as np
import jax
import jax.numpy as jnp
from jax import lax
from jax.experimental import pallas as pl
from jax.experimental.pallas import tpu as pltpu

f32 = jnp.float32
bf16 = jnp.bfloat16

D_MODEL = 1024
HEAD_DIM = 64
N_KV = 2
GQA = 4
N_HEADS = 8
CMP_STRIDE = 16
CMP_LEN = 32
N_CMP = 127
SEL_BLOCK = 64
N_SEL = 16
WINDOW = 512
ATT_SCALE = HEAD_DIM ** -0.5
REL_BUCKETS = 32
REL_MAX_DIST = 128
GLA_HEADS = 4
GLA_DK = 32
GLA_DV = 64
GLA_RANK = 16
GLA_TAU = 16.0
GLA_CHUNK = 64
POOL_WINDOWS = (2, 4, 8, 16)
POOL_BUF = 15
D_FF = 2816
EPS = 1e-6
NEG = -1e30
BIG = 1e9
PAGE = 128

LANES = 128
SUBLANES = 8
QT = 128
PROJ_W = 2560
VMEM_LIMIT = 56 * 1024 * 1024


def _cparams(sem):
    return pltpu.CompilerParams(dimension_semantics=sem, vmem_limit_bytes=VMEM_LIMIT)


def _dot(a, b):
    return jnp.dot(a, b, preferred_element_type=f32)


def _dot_nt(a, b):
    return lax.dot_general(a, b, (((1,), (1,)), ((), ())), preferred_element_type=f32)


def _dot_tn(a, b):
    return lax.dot_general(a, b, (((0,), (0,)), ((), ())), preferred_element_type=f32)


def _dot_hilo(a, b_bf16):
    hi = a.astype(bf16)
    lo = (a - hi.astype(f32)).astype(bf16)
    return _dot(hi, b_bf16) + _dot(lo, b_bf16)


def _pair_rmsnorm(y, gain):
    lane = lax.broadcasted_iota(jnp.int32, y.shape, y.ndim - 1)
    lo = lane < HEAD_DIM
    y2 = y * y
    s_lo = jnp.sum(jnp.where(lo, y2, 0.0), axis=-1, keepdims=True)
    s_hi = jnp.sum(jnp.where(lo, 0.0, y2), axis=-1, keepdims=True)
    ms = jnp.where(lo, s_lo, s_hi) * (1.0 / HEAD_DIM)
    return y * lax.rsqrt(ms + EPS) * gain


_C_Q, _C_CK, _C_CV, _C_SK, _C_SV, _C_WK, _C_WV = 0, 512, 640, 768, 896, 1024, 1152
_C_GATE, _C_GQ, _C_GK, _C_GV, _C_GR, _C_GLR, _C_PU = 1280, 1408, 1536, 1664, 1920, 2176, 2304


def _proj_kernel(x_ref, g_ref, w_ref, qg_ref, kg_ref, wa_ref, ba_ref,
                 q_o, ckv_o, skv_o, wkv_o, gt_o, gq_o, gk_o, gv_o, gr_o, gl_o, pu_o):
    x = x_ref[...]
    hn = (x * lax.rsqrt(jnp.mean(x * x, axis=-1, keepdims=True) + EPS) * g_ref[...]).astype(bf16)

    def proj(c0, n):
        return _dot(hn, w_ref[:, c0:c0 + n])

    for j in range(4):
        q_o[:, j * LANES:(j + 1) * LANES] = _pair_rmsnorm(proj(_C_Q + j * LANES, LANES), qg_ref[...]).astype(bf16)
    ckv_o[...] = proj(_C_CK, 2 * LANES)
    skv_o[:, :LANES] = _pair_rmsnorm(proj(_C_SK, LANES), kg_ref[1:2, :])
    skv_o[:, LANES:] = proj(_C_SV, LANES)
    wkv_o[:, :LANES] = _pair_rmsnorm(proj(_C_WK, LANES), kg_ref[2:3, :])
    wkv_o[:, LANES:] = proj(_C_WV, LANES)
    gt_o[...] = jax.nn.sigmoid(proj(_C_GATE, LANES))
    gq_o[...] = proj(_C_GQ, LANES)
    gk_o[...] = proj(_C_GK, LANES)
    gv_o[...] = proj(_C_GV, 2 * LANES)
    gr_o[...] = proj(_C_GR, 2 * LANES)
    z = _dot(proj(_C_GLR, LANES).astype(bf16), wa_ref[...]) + ba_ref[...]
    gl_o[...] = jax.nn.log_sigmoid(z) * (1.0 / GLA_TAU)
    pu_o[...] = proj(_C_PU, 2 * LANES)


def _in_proj(x, lw, tm):
    m = x.shape[0]
    row = lambda n: pl.BlockSpec((tm, n), lambda i: (i, 0))
    full = lambda a: pl.BlockSpec(a.shape, lambda i: (0,) * a.ndim)
    widths = (512, 256, 256, 256, 128, 128, 128, 256, 256, 128, 256)
    dtypes = (bf16,) + (f32,) * 10
    args = (x, lw['g_mix'], lw['w_in'], lw['q_gain'], lw['k_gain'], lw['w_alpha'], lw['b_alpha'])
    return pl.pallas_call(
        _proj_kernel,
        grid=(m // tm,),
        in_specs=[row(D_MODEL)] + [full(a) for a in args[1:]],
        out_specs=[row(n) for n in widths],
        out_shape=[jax.ShapeDtypeStruct((m, n), dt) for n, dt in zip(widths, dtypes)],
        compiler_params=_cparams(("parallel",)),
        name="in_proj",
    )(*args)


def _compress(get_x, w_ref, pe_ref, w2_ref):
    rows = get_x(0).shape[0]
    acc = jnp.zeros((rows + 2 * SUBLANES, 4 * LANES), f32)
    for s in range(0, CMP_STRIDE, 2):
        xs = jnp.concatenate([get_x(s), get_x(s + 1)], axis=1)
        xs = jnp.concatenate([xs, pe_ref[s // 2]], axis=0).astype(bf16)
        acc = acc + _dot(xs, w_ref[s // 2])
    a0 = acc[:rows, :2 * LANES]
    a1 = acc[:rows, 2 * LANES:]
    c0 = acc[rows:rows + 1, :2 * LANES] + acc[rows + SUBLANES:rows + SUBLANES + 1, 2 * LANES:]
    h = a0 + pltpu.roll(a1, rows - 1, 0) + c0
    return _dot(jax.nn.gelu(h).astype(bf16), w2_ref[...])


def _compress_prompt_kernel(ck_ref, cv_ref, wk_ref, wv_ref, pek_ref, pev_ref, w2k_ref, w2v_ref, kg_ref, kc_o, vc_o):
    kx = lambda s: ck_ref[pl.ds(s, LANES, stride=CMP_STRIDE), :]
    vx = lambda s: cv_ref[pl.ds(s, LANES, stride=CMP_STRIDE), :]
    kc_o[...] = _pair_rmsnorm(_compress(kx, wk_ref, pek_ref, w2k_ref), kg_ref[0:1, :])
    vc_o[...] = _compress(vx, wv_ref, pev_ref, w2v_ref)


def _compress_prompt(ckv, lw, nb, seq):
    full = lambda a: pl.BlockSpec(a.shape, lambda b: (0,) * a.ndim)
    kblk = pl.BlockSpec((seq, LANES), lambda b: (b, 0))
    vblk = pl.BlockSpec((seq, LANES), lambda b: (b, 1))
    outblk = pl.BlockSpec((None, LANES, LANES), lambda b: (b, 0, 0))
    ws = (lw['cmp_wk'], lw['cmp_wv'], lw['cmp_pek'], lw['cmp_pev'], lw['cmp_w2k'], lw['cmp_w2v'], lw['k_gain'])
    return pl.pallas_call(
        _compress_prompt_kernel,
        grid=(nb,),
        in_specs=[kblk, vblk] + [full(a) for a in ws],
        out_specs=[outblk, outblk],
        out_shape=[jax.ShapeDtypeStruct((nb, LANES, LANES), f32)] * 2,
        compiler_params=_cparams(("parallel",)),
        name="compress_prompt",
    )(ckv, ckv, *ws)


def _rank_select(score, n_blocks):
    blk = lax.broadcasted_iota(jnp.int32, score.shape, 1)
    rank = jnp.zeros(score.shape, f32)
    for m in range(n_blocks):
        sm = score[:, m:m + 1]
        beats = (sm > score) | ((sm == score) & (m < blk))
        rank = rank + jnp.where(beats, 1.0, 0.0)
    return jnp.where(rank < float(N_SEL), 1.0, 0.0)


def _rank_select_rows(score, n_blocks):
    blk = lax.broadcasted_iota(jnp.int32, score.shape, 0)
    rank = jnp.zeros(score.shape, f32)
    for m in range(n_blocks):
        sm = score[m:m + 1, :]
        beats = (sm > score) | ((sm == score) & (m < blk))
        rank = rank + jnp.where(beats, 1.0, 0.0)
    return jnp.where(rank < float(N_SEL), 1.0, 0.0)


def _nsa_prompt_kernel(q_ref, kc_ref, vc_ref, sk_ref, sv_ref, wk_ref, wv_ref, gt_ref,
                       td_ref, ts_ref, cf_ref, we_ref, bc_ref, ovt_ref, eet_ref, o_ref,
                       selx_ref, m_ref, l_ref, acc_ref):
    i = pl.program_id(1)
    lane = lax.broadcasted_iota(jnp.int32, (QT, LANES), 1)
    row = lax.broadcasted_iota(jnp.int32, (QT, LANES), 0)
    n_blk = selx_ref.shape[1] // SEL_BLOCK
    blk = lax.broadcasted_iota(jnp.int32, (n_blk, QT), 0)
    cur = (i * QT + lax.broadcasted_iota(jnp.int32, (n_blk, QT), 1)) >> 6
    gates_t = gt_ref[...].T
    outs = [[None, None] for _ in range(GQA)]

    def tile4(x):
        return jnp.concatenate([x] * GQA, axis=1)

    def scores(q4, k_ref, kt, bias, sel_head=None, live=None):
        off = pl.multiple_of(kt * QT, QT)
        s = _dot_nt(k_ref[pl.ds(off, QT), :].astype(bf16), q4) + bias
        if live is not None:
            s = s + jnp.where(live, 0.0, NEG)
        if sel_head is not None:
            s = s + tile4(selx_ref[sel_head, pl.ds(off, QT), :])
        return s, off

    def attend(v_ref, tiles, state=None):
        m = state[0] if state is not None else None
        for s, _ in tiles:
            ms = jnp.max(s, axis=0, keepdims=True)
            m = ms if m is None else jnp.maximum(m, ms)
        l, acc = None, None
        for s, off in tiles:
            p = jnp.exp(s - m)
            pv = _dot_tn(v_ref[pl.ds(off, QT), :].astype(bf16), p.astype(bf16))
            ps = jnp.sum(p, axis=0, keepdims=True)
            l = ps if l is None else l + ps
            acc = pv if acc is None else acc + pv
        if state is not None:
            alpha = jnp.exp(state[0] - m)
            l = l + alpha * state[1]
            acc = acc + alpha * state[2]
        return m, l, acc

    m_ref[...] = jnp.full(m_ref.shape, NEG, f32)
    l_ref[...] = jnp.zeros(l_ref.shape, f32)
    acc_ref[...] = jnp.zeros(acc_ref.shape, f32)

    q4s, fars, o_cmps = [], [], []
    for h in range(N_KV):
        half = (lane >= HEAD_DIM) if h else (lane < HEAD_DIM)
        q4 = jnp.concatenate(
            [jnp.where(half, q_ref[:, g * LANES:(g + 1) * LANES] * jnp.asarray(ATT_SCALE, bf16), jnp.zeros((), bf16))
             for g in range(GQA)], axis=0)
        q4s.append(q4)
        fars.append(cf_ref[h, 0:1, :])
        bias_c = bc_ref[h, pl.ds(pl.multiple_of(LANES - SUBLANES * i, SUBLANES), LANES), :]
        s = _dot_nt(kc_ref[...].astype(bf16), q4) + bias_c
        e = jnp.exp(s - jnp.max(s, axis=0, keepdims=True))
        p = jnp.where(s > 0.5 * NEG, e / jnp.sum(e, axis=0, keepdims=True), 0.0)
        o_cmps.append(_dot_tn(vc_ref[...].astype(bf16), p.astype(bf16)))
        psum = p[:, 0:QT] + p[:, QT:2 * QT] + p[:, 2 * QT:3 * QT] + p[:, 3 * QT:4 * QT]
        hi = psum.astype(bf16)
        lo = (psum - hi.astype(f32)).astype(bf16)
        imp = (_dot(ovt_ref[...], hi) + _dot(ovt_ref[...], lo))[0:n_blk, :]
        forced = (blk == 0) | (blk == cur) | (blk == cur - 1)
        score = jnp.where(forced, BIG, jnp.where(blk <= cur, imp, -BIG))
        sel = _rank_select_rows(score, n_blk)
        sel = jnp.concatenate([sel, jnp.zeros((LANES - n_blk, QT), f32)], axis=0).astype(bf16)
        selx_ref[h] = (_dot(eet_ref[...], sel) - 1.0) * (-NEG)

    n_far = jnp.maximum(i - 1, 0)

    def far_pairs(j, carry):
        kt1 = 2 * j + 1
        tiles = [[scores(q4s[h], sk_ref, 2 * j, fars[h], sel_head=h),
                  scores(q4s[h], sk_ref, jnp.minimum(kt1, i), fars[h], sel_head=h, live=kt1 < n_far)]
                 for h in range(N_KV)]
        for h in range(N_KV):
            m_ref[h], l_ref[h], acc_ref[h] = attend(sv_ref, tiles[h], (m_ref[h], l_ref[h], acc_ref[h]))
        return carry

    lax.fori_loop(0, (n_far + 1) // 2, far_pairs, 0)

    def back_tile(back):
        return dict(kt=jnp.maximum(i - back, 0), live=(i >= back) if back else None)

    slc_tiles = [[scores(q4s[h], sk_ref, bias=tab_ref[h], sel_head=h, **back_tile(back))
                  for back, tab_ref in ((1, ts_ref), (0, td_ref))] for h in range(N_KV)]
    win_tiles = [[scores(q4s[h], wk_ref, bias=bias, **back_tile(back))
                  for back, bias in ((4, we_ref[h]), (3, fars[h]), (2, fars[h]), (1, ts_ref[h]), (0, td_ref[h]))]
                 for h in range(N_KV)]
    for h in range(N_KV):
        _, l_s, acc_s = attend(sv_ref, slc_tiles[h], (m_ref[h], l_ref[h], acc_ref[h]))
        _, l_w, acc_w = attend(wv_ref, win_tiles[h])
        o_slc = acc_s / l_s
        o_win = acc_w / l_w
        for g in range(GQA):
            c = (h * GQA + g) * 3
            sl = slice(g * QT, (g + 1) * QT)
            outs[g][h] = (gates_t[c:c + 1, :] * o_cmps[h][:, sl] + gates_t[c + 1:c + 2, :] * o_slc[:, sl]
                          + gates_t[c + 2:c + 3, :] * o_win[:, sl])
    for g in range(GQA):
        o_ref[:, g * LANES:(g + 1) * LANES] = jnp.where(row < HEAD_DIM, outs[g][0], outs[g][1]).T.astype(bf16)


def _nsa_prompt(q, kc, vc, skv, wkv, gt, tabs, nb, seq):
    nt = seq // QT
    tile = lambda n: pl.BlockSpec((QT, n), lambda b, i: (b * nt + i, 0))
    kblk = pl.BlockSpec((seq, LANES), lambda b, i: (b, 0))
    vblk = pl.BlockSpec((seq, LANES), lambda b, i: (b, 1))
    cblk = pl.BlockSpec((None, LANES, LANES), lambda b, i: (b, 0, 0))
    full = lambda a: pl.BlockSpec(a.shape, lambda b, i: (0,) * a.ndim)
    consts = (tabs['td'], tabs['ts'], tabs['cf'], tabs['we'], tabs['bc'], tabs['ovt'], tabs['eet'])
    return pl.pallas_call(
        _nsa_prompt_kernel,
        grid=(nb, nt),
        in_specs=[tile(512), cblk, cblk, kblk, vblk, kblk, vblk, tile(LANES)] + [full(a) for a in consts],
        out_specs=tile(512),
        out_shape=jax.ShapeDtypeStruct((nb * seq, 512), bf16),
        scratch_shapes=[pltpu.VMEM((N_KV, seq, QT), f32), pltpu.VMEM((N_KV, 1, GQA * QT), f32),
                        pltpu.VMEM((N_KV, 1, GQA * QT), f32), pltpu.VMEM((N_KV, LANES, GQA * QT), f32)],
        compiler_params=_cparams(("parallel", "arbitrary")),
        name="nsa_prompt",
    )(q, kc, vc, skv, skv, wkv, wkv, gt, *consts)


N_PAGES = 16
SAMPLE_SEQS_PER_STEP = 2


def _nsa_sample_kernel(n_seq, pt_ref, *refs):
    per_seq = 2 * N_PAGES
    seq_pages = [refs[t * per_seq:(t + 1) * per_seq] for t in range(n_seq)]
    (win_ref, qr_ref, gr_ref, knew_ref, wcat_k, wcat_v, pek_ref, pev_ref, w2k_ref, w2v_ref, kg_ref,
     bc_ref, bs_ref, bw_ref, b0_ref, ov_ref, o_ref, win_o, hist_ref) = refs[n_seq * per_seq:]

    for kind in range(2):
        for t in range(n_seq):
            for j in range(N_PAGES):
                hist_ref[kind, pl.ds((t * N_PAGES + j) * PAGE, PAGE), :] = seq_pages[t][j][kind].T

    def hist(kind):
        return lambda s: hist_ref[kind, pl.ds(s, n_seq * LANES, stride=CMP_STRIDE), :]

    kc_all = _pair_rmsnorm(_compress(hist(0), wcat_k, pek_ref, w2k_ref), kg_ref[0:1, :])
    vc_all = _compress(hist(1), wcat_v, pev_ref, w2v_ref)
    for t in range(n_seq):
        rows = slice(t * LANES, (t + 1) * LANES)
        _nsa_sample_one(t, kc_all[rows], vc_all[rows], seq_pages[t][N_PAGES:], win_ref, qr_ref, gr_ref,
                        knew_ref, bc_ref, bs_ref, bw_ref, b0_ref, ov_ref, o_ref, win_o)


def _nsa_sample_one(t, kc, vc, slc_pages, win_ref, qr_ref, gr_ref, knew_ref, bc_ref, bs_ref, bw_ref, b0_ref,
                    ov_ref, o_ref, win_o):
    lane = lax.broadcasted_iota(jnp.int32, (SUBLANES, LANES), 1)
    qf = qr_ref[t]
    qb = qf.astype(bf16)
    gates = gr_ref[t]
    knew = knew_ref[t]
    hi_rows = lax.broadcasted_iota(jnp.int32, (SUBLANES, LANES), 0) >= GQA
    own = (lane >= HEAD_DIM) == hi_rows

    s = _dot_nt(qb, kc.astype(bf16)) * ATT_SCALE + bc_ref[...]
    mskc = lane < N_CMP
    s = jnp.where(mskc, s, NEG)
    e = jnp.exp(s - jnp.max(s, axis=-1, keepdims=True))
    p = jnp.where(mskc, e / jnp.sum(e, axis=-1, keepdims=True), 0.0)
    o_cmp = _dot(p.astype(bf16), vc.astype(bf16))
    psum = jnp.broadcast_to(jnp.sum(p.reshape(N_KV, GQA, LANES), axis=1, keepdims=True), (N_KV, GQA, LANES))
    imp = _dot_hilo(psum.reshape(SUBLANES, LANES), ov_ref[...])
    n_blk = N_PAGES * PAGE // SEL_BLOCK + 1
    cur = n_blk - 1
    forced = (lane == 0) | (lane == cur) | (lane == cur - 1)
    score = jnp.where(forced, BIG, imp)
    score = jnp.where(lane < n_blk, score, -3e38)
    sel = _rank_select(score, n_blk)

    def softmax_av(s_parts, s_new, vt_parts, v_new):
        mx = s_new
        for sp in s_parts:
            mx = jnp.maximum(mx, jnp.max(sp, axis=-1, keepdims=True))
        p_new = jnp.exp(s_new - mx)
        den = p_new
        acc = p_new * v_new
        for sp, vt in zip(s_parts, vt_parts):
            pp = jnp.exp(sp - mx)
            den = den + jnp.sum(pp, axis=-1, keepdims=True)
            acc = acc + _dot_nt(pp.astype(bf16), vt.astype(bf16))
        return acc / den

    def new_row(r):
        return knew[r:r + 1, :]

    def score_new(krow):
        return jnp.sum(jnp.where(own, qf * krow, 0.0), axis=-1, keepdims=True) * ATT_SCALE + b0_ref[:, 0:1]

    s_parts = []
    for j in range(N_PAGES):
        sj = _dot(qb, slc_pages[j][0].astype(bf16)) * ATT_SCALE + bs_ref[:, j * PAGE:(j + 1) * PAGE]
        mj = jnp.where(lane < SEL_BLOCK, sel[:, 2 * j:2 * j + 1], sel[:, 2 * j + 1:2 * j + 2]) > 0.5
        s_parts.append(jnp.where(mj, sj, NEG))
    o_slc = softmax_av(s_parts, score_new(new_row(0)), [pg[1] for pg in slc_pages], new_row(1))

    w_parts = []
    for j in range(WINDOW // PAGE):
        kw = win_ref[t, 0, :, j * PAGE:(j + 1) * PAGE]
        sj = _dot(qb, kw.astype(bf16)) * ATT_SCALE + bw_ref[:, j * PAGE:(j + 1) * PAGE]
        if j == 0:
            sj = jnp.where(lane >= 1, sj, NEG)
        w_parts.append(sj)
    vt_parts = [win_ref[t, 1, :, j * PAGE:(j + 1) * PAGE] for j in range(WINDOW // PAGE)]
    o_win = softmax_av(w_parts, score_new(new_row(2)), vt_parts, new_row(3))

    o_ref[t] = gates[:, 0:1] * o_cmp + gates[:, 1:2] * o_slc + gates[:, 2:3] * o_win

    new_cols = knew.T
    last = lax.broadcasted_iota(jnp.int32, (LANES, WINDOW), 1) == WINDOW - 1
    for kv in range(2):
        win_o[t, kv] = jnp.where(last, new_cols[:, 2 + kv:3 + kv], pltpu.roll(win_ref[t, kv], WINDOW - 1, 1))


def _nsa_sample(layer, page_table, cache_cmp, cache_slc, win_state, qrows, grows, knew, lw, tabs):
    nb = page_table.shape[0]
    n_seq = SAMPLE_SEQS_PER_STEP

    def page_t(t, j):
        return pl.BlockSpec((None, None, 2, LANES, PAGE), lambda b, pt: (layer, pt[b * n_seq + t, j], 0, 0, 0))

    full = lambda a: pl.BlockSpec(a.shape, lambda b, pt: (0,) * a.ndim)
    per_b = pl.BlockSpec((n_seq, SUBLANES, LANES), lambda b, pt: (b, 0, 0))
    consts = (lw['cmp_wk'], lw['cmp_wv'], lw['cmp_pek'], lw['cmp_pev'], lw['cmp_w2k'], lw['cmp_w2v'], lw['k_gain'],
              tabs['bc_s'], tabs['bs_s'], tabs['bw_s'], tabs['b0_s'], tabs['ov_s'])
    page_specs, page_args = [], []
    for t in range(n_seq):
        page_specs += [page_t(t, j) for j in range(N_PAGES)] * 2
        page_args += [cache_cmp] * N_PAGES + [cache_slc] * N_PAGES
    in_specs = (page_specs
                + [pl.BlockSpec((None, n_seq, 2, LANES, WINDOW), lambda b, pt: (layer, b, 0, 0, 0)), per_b, per_b, per_b]
                + [full(a) for a in consts])
    return pl.pallas_call(
        functools.partial(_nsa_sample_kernel, n_seq),
        grid_spec=pltpu.PrefetchScalarGridSpec(
            num_scalar_prefetch=1, grid=(nb // n_seq,), in_specs=in_specs,
            out_specs=[per_b, pl.BlockSpec((n_seq, 2, LANES, WINDOW), lambda b, pt: (b, 0, 0, 0))],
            scratch_shapes=[pltpu.VMEM((2, n_seq * N_PAGES * PAGE, LANES), f32)]),
        out_shape=[jax.ShapeDtypeStruct((nb, SUBLANES, LANES), f32),
                   jax.ShapeDtypeStruct((nb, 2, LANES, WINDOW), f32)],
        compiler_params=_cparams(("parallel",)),
        name="nsa_sample",
    )(page_table, *page_args, win_state, qrows, grows, knew, *consts)


def _gla_out(o, gn_ref, gr):
    o = jnp.concatenate([_pair_rmsnorm(o[:, :LANES], gn_ref[:, :LANES]),
                         _pair_rmsnorm(o[:, LANES:], gn_ref[:, LANES:])], axis=1)
    return o * jax.nn.silu(gr)


def _gla_prompt_kernel(gq_ref, gk_ref, gl_ref, gv_ref, gr_ref, gn_ref, o_ref, st_o, st_ref):
    c_rows = GLA_CHUNK
    lane = lax.broadcasted_iota(jnp.int32, (c_rows, LANES), 1)
    row = lax.broadcasted_iota(jnp.int32, (c_rows, LANES), 0)
    tril = (lax.broadcasted_iota(jnp.int32, (c_rows, c_rows), 0)
            >= lax.broadcasted_iota(jnp.int32, (c_rows, c_rows), 1))
    lane_v = lax.broadcasted_iota(jnp.int32, (c_rows, GLA_HEADS * GLA_DV), 1)
    srow = lax.broadcasted_iota(jnp.int32, (GLA_HEADS * GLA_DV, LANES), 0)
    scol = lax.broadcasted_iota(jnp.int32, (GLA_HEADS * GLA_DV, LANES), 1)
    diag = (srow >> 6) == (scol >> 5)
    st_ref[...] = jnp.zeros(st_ref.shape, f32)

    def body(c, carry):
        off = pl.multiple_of(c * c_rows, c_rows)
        q = gq_ref[pl.ds(off, c_rows), :] * (GLA_DK ** -0.5)
        k = gk_ref[pl.ds(off, c_rows), :]
        v = gv_ref[pl.ds(off, c_rows), :]
        b = gl_ref[pl.ds(off, c_rows), :]
        sh = 1
        while sh < c_rows:
            b = b + jnp.where(row >= sh, pltpu.roll(b, sh, 0), 0.0)
            sh *= 2
        qd = q * jnp.exp(b)
        kd = (k * jnp.exp(-b)).astype(bf16)
        bl = b[c_rows - 1:c_rows, :]
        kk = (k * jnp.exp(bl - b)).astype(bf16)
        st = st_ref[...]
        vb = v.astype(bf16)
        o = _dot_nt(qd.astype(bf16), st.astype(bf16))
        for hh in range(GLA_HEADS):
            qm = jnp.where((lane >> 5) == hh, qd, 0.0).astype(bf16)
            a = jnp.where(tril, _dot_nt(qm, kd), 0.0)
            o = o + jnp.where((lane_v >> 6) == hh, _dot(a.astype(bf16), vb), 0.0)
        st_ref[...] = jnp.exp(bl) * st + jnp.where(diag, _dot_tn(vb, kk), 0.0)
        o_ref[pl.ds(off, c_rows), :] = _gla_out(o, gn_ref, gr_ref[pl.ds(off, c_rows), :]).astype(bf16)
        return carry

    lax.fori_loop(0, gq_ref.shape[0] // c_rows, body, 0)
    st_o[...] = st_ref[...]


def _gla_prompt(gq, gk, gl, gv, gr, gn, nb, seq):
    blk = lambda n: pl.BlockSpec((seq, n), lambda b: (b, 0))
    return pl.pallas_call(
        _gla_prompt_kernel,
        grid=(nb,),
        in_specs=[blk(LANES), blk(LANES), blk(LANES), blk(2 * LANES), blk(2 * LANES),
                  pl.BlockSpec(gn.shape, lambda b: (0, 0))],
        out_specs=[blk(2 * LANES), pl.BlockSpec((None, GLA_HEADS * GLA_DV, LANES), lambda b: (b, 0, 0))],
        out_shape=[jax.ShapeDtypeStruct((nb * seq, 2 * LANES), bf16),
                   jax.ShapeDtypeStruct((nb, GLA_HEADS * GLA_DV, LANES), f32)],
        scratch_shapes=[pltpu.VMEM((GLA_HEADS * GLA_DV, LANES), f32)],
        compiler_params=_cparams(("parallel",)),
        name="gla_prompt",
    )(gq, gk, gl, gv, gr, gn)


def _gla_sample_kernel(cols_ref, v_ref, s0_ref, gr_ref, gn_ref, o_ref, s_o):
    nb = s0_ref.shape[0]
    c = cols_ref[...]
    q = c[:, :, 0:1] * (GLA_DK ** -0.5)
    k = c[:, :, 1:2]
    g = c[:, :, 2:3]
    eg = jnp.exp(g)
    qd = q * eg
    kd = k * jnp.exp(-g)
    s0 = s0_ref[...]
    v = v_ref[...]
    a = jnp.sum((qd * kd).reshape(nb, GLA_HEADS, GLA_DK, 1), axis=2)
    o = jnp.sum((qd * s0).reshape(nb, GLA_HEADS, GLA_DK, GLA_DV), axis=2) + a * v
    vexp = jnp.broadcast_to(v[:, :, None, :], (nb, GLA_HEADS, GLA_DK, GLA_DV)).reshape(nb, GLA_HEADS * GLA_DK, GLA_DV)
    s_o[...] = eg * s0 + k * vexp
    y = o * lax.rsqrt(jnp.mean(o * o, axis=-1, keepdims=True) + EPS) * gn_ref[...]
    o_ref[...] = y * jax.nn.silu(gr_ref[...])


def _gla_sample(cols, v, s0, gr, gn, bt=8):
    nb = s0.shape[0]
    blk = lambda a: pl.BlockSpec((bt,) + a.shape[1:], lambda i: (i,) + (0,) * (a.ndim - 1))
    return pl.pallas_call(
        _gla_sample_kernel,
        grid=(nb // bt,),
        in_specs=[blk(cols), blk(v), blk(s0), blk(gr), pl.BlockSpec(gn.shape, lambda i: (0, 0))],
        out_specs=[blk(v), blk(s0)],
        out_shape=[jax.ShapeDtypeStruct(v.shape, f32), jax.ShapeDtypeStruct(s0.shape, f32)],
        compiler_params=_cparams(("parallel",)),
        name="gla_sample",
    )(cols, v, s0, gr, gn)


def _pool_finish(sums, u, cnt_inv, w_ref, sc_ref):
    lane = lax.broadcasted_iota(jnp.int32, u.shape, 1)
    grp = lane >> 6
    s = jnp.where(grp == 0, sums[0], jnp.where(grp == 1, sums[1], jnp.where(grp == 2, sums[2], sums[3])))
    dlt = s * cnt_inv - u
    return _dot(dlt.astype(bf16), w_ref[...]) * sc_ref[...]


def _pool_prompt_kernel(u_ref, w_ref, sc_ref, o_ref):
    u = u_ref[...]
    row = lax.broadcasted_iota(jnp.int32, u.shape, 0)
    lane = lax.broadcasted_iota(jnp.int32, u.shape, 1)
    sums = []
    s = u
    sh = 1
    while sh < POOL_WINDOWS[-1]:
        s = s + jnp.where(row >= sh, pltpu.roll(s, sh, 0), 0.0)
        sums.append(s)
        sh *= 2
    grp = lane >> 6
    win = jnp.where(grp == 0, POOL_WINDOWS[0], jnp.where(grp == 1, POOL_WINDOWS[1],
                    jnp.where(grp == 2, POOL_WINDOWS[2], POOL_WINDOWS[3])))
    cnt = jnp.minimum(win, row + 1).astype(f32)
    o_ref[...] = _pool_finish(sums, u, 1.0 / cnt, w_ref, sc_ref).astype(bf16)


def _pool_prompt(pu, w, sc, nb, seq):
    blk = pl.BlockSpec((seq, 2 * LANES), lambda b: (b, 0))
    full = lambda a: pl.BlockSpec(a.shape, lambda b: (0,) * a.ndim)
    return pl.pallas_call(
        _pool_prompt_kernel,
        grid=(nb,),
        in_specs=[blk, full(w), full(sc)],
        out_specs=blk,
        out_shape=jax.ShapeDtypeStruct((nb * seq, 2 * LANES), bf16),
        compiler_params=_cparams(("parallel",)),
        name="pool_prompt",
    )(pu, w, sc)


def _pool_sample_kernel(u_ref, buf_ref, w_ref, sc_ref, o_ref):
    u = u_ref[...]
    lane = lax.broadcasted_iota(jnp.int32, u.shape, 1)
    sums = []
    s = u
    nxt = POOL_BUF - 1
    for win in POOL_WINDOWS:
        while POOL_BUF - nxt < win:
            s = s + buf_ref[nxt]
            nxt -= 1
        sums.append(s)
    grp = lane >> 6
    cnt_inv = jnp.where(grp == 0, 1.0 / POOL_WINDOWS[0], jnp.where(grp == 1, 1.0 / POOL_WINDOWS[1],
                        jnp.where(grp == 2, 1.0 / POOL_WINDOWS[2], 1.0 / POOL_WINDOWS[3])))
    o_ref[...] = _pool_finish(sums, u, cnt_inv, w_ref, sc_ref).astype(bf16)


def _pool_sample(pu, buf_t, w, sc):
    full = lambda a: pl.BlockSpec(a.shape, lambda i: (0,) * a.ndim)
    return pl.pallas_call(
        _pool_sample_kernel,
        grid=(1,),
        in_specs=[full(pu), full(buf_t), full(w), full(sc)],
        out_specs=full(pu),
        out_shape=jax.ShapeDtypeStruct(pu.shape, bf16),
        compiler_params=_cparams(("arbitrary",)),
        name="pool_sample",
    )(pu, buf_t, w, sc)


def _out_proj_kernel(a_ref, b_ref, c_ref, x_ref, w_ref, g_ref, x_o, h_o):
    y = (_dot(a_ref[...], w_ref[0:512, :]) + _dot(b_ref[...], w_ref[512:768, :])
         + _dot(c_ref[...], w_ref[768:1024, :]))
    x = x_ref[...] + y
    x_o[...] = x
    h_o[...] = (x * lax.rsqrt(jnp.mean(x * x, axis=-1, keepdims=True) + EPS) * g_ref[...]).astype(bf16)


def _out_proj(o_nsa, o_gla, o_pool, x, w, g, tm):
    m = x.shape[0]
    row = lambda n: pl.BlockSpec((tm, n), lambda i: (i, 0))
    full = lambda a: pl.BlockSpec(a.shape, lambda i: (0,) * a.ndim)
    return pl.pallas_call(
        _out_proj_kernel,
        grid=(m // tm,),
        in_specs=[row(512), row(256), row(256), row(D_MODEL), full(w), full(g)],
        out_specs=[row(D_MODEL), row(D_MODEL)],
        out_shape=[jax.ShapeDtypeStruct((m, D_MODEL), f32), jax.ShapeDtypeStruct((m, D_MODEL), bf16)],
        compiler_params=_cparams(("parallel",)),
        name="out_proj",
    )(o_nsa, o_gla, o_pool, x, w, g)


FF_CHUNK = 2816
HALO = 16


def _ffn_kernel(tiles_per_seq, decode, h_ref, p_ref, x_ref, wa_ref, wu_ref, cw_ref, cb_ref, wd_ref, x_o, a_o):
    hn = h_ref[...]
    tm = hn.shape[0]
    row = lax.broadcasted_iota(jnp.int32, (tm, FF_CHUNK), 0)
    if not decode:
        first = (pl.program_id(0) % tiles_per_seq) == 0
        hx = jnp.concatenate([hn, p_ref[...]], axis=0)
    acc = jnp.zeros((tm, D_MODEL), f32)
    for c in range(D_FF // FF_CHUNK):
        cs = slice(c * FF_CHUNK, (c + 1) * FF_CHUNK)
        u = _dot(hn, wu_ref[:, cs])
        if decode:
            a = _dot(hn, wa_ref[:, cs])
            a2 = p_ref[0, :, cs]
            a1 = p_ref[1, :, cs]
            a_o[:, cs] = a
        else:
            ax = _dot(hx, wa_ref[:, cs])
            a = ax[:tm]
            ah = jnp.where(first, 0.0, ax[tm:])
            p1 = ah[HALO - 1:HALO, :]
            p2 = ah[HALO - 2:HALO - 1, :]
            a1 = jnp.where(row == 0, p1, pltpu.roll(a, 1, 0))
            a2 = jnp.where(row == 0, p2, jnp.where(row == 1, p1, pltpu.roll(a, 2, 0)))
            a_o[:, cs] = a[tm - SUBLANES:, :]
        ac = cb_ref[:, cs] + a2 * cw_ref[0:1, cs] + a1 * cw_ref[1:2, cs] + a * cw_ref[2:3, cs]
        act = (jax.nn.silu(ac) * u).astype(bf16)
        acc = acc + _dot(act, wd_ref[cs, :])
    x_o[...] = x_ref[...] + acc


def _ffn(hn, prev, x, lw, tm, tiles_per_seq, decode):
    m = x.shape[0]
    nt = m // tm
    row = lambda n: pl.BlockSpec((tm, n), lambda i: (i, 0))
    const = lambda a: pl.BlockSpec(a.shape, lambda i: (0,) * a.ndim, pipeline_mode=pl.Buffered(1))
    if decode:
        prev_spec = pl.BlockSpec(prev.shape, lambda i: (0, 0, 0))
        a_spec = row(D_FF)
        a_shape = jax.ShapeDtypeStruct((m, D_FF), f32)
    else:
        per = tm // HALO
        prev_spec = pl.BlockSpec((HALO, D_MODEL), lambda i: (jnp.maximum(i * per - 1, 0), 0))
        a_spec = pl.BlockSpec((None, SUBLANES, D_FF), lambda i: (i, 0, 0))
        a_shape = jax.ShapeDtypeStruct((nt, SUBLANES, D_FF), f32)
    ws = (lw['w_up_a'], lw['w_up_u'], lw['conv_w'], lw['conv_b'], lw['w_down'])
    return pl.pallas_call(
        functools.partial(_ffn_kernel, tiles_per_seq, decode),
        grid=(nt,),
        in_specs=[row(D_MODEL), prev_spec, row(D_MODEL)] + [const(a) for a in ws],
        out_specs=[row(D_MODEL), a_spec],
        out_shape=[jax.ShapeDtypeStruct((m, D_MODEL), f32), a_shape],
        compiler_params=_cparams(("parallel",)),
        name="ffn_decode" if decode else "ffn_prompt",
    )(hn, prev, x, *ws)


def _rel_bucket_np(d):
    d = np.maximum(d, 0)
    exact = REL_BUCKETS // 2
    lg = np.log(np.maximum(d, 1).astype(np.float32) / np.float32(exact)) / np.float32(math.log(REL_MAX_DIST / exact))
    large = np.minimum(exact + (lg * np.float32(REL_BUCKETS - exact)).astype(np.int32), REL_BUCKETS - 1)
    return np.where(d < exact, d, large).astype(np.int32)


def _bias_lookup(rel_bias, dist):
    bucket = _rel_bucket_np(np.asarray(dist)).reshape(-1)
    onehot = jnp.asarray((bucket[:, None] == np.arange(REL_BUCKETS)[None, :]).astype(np.int8)).astype(f32)
    out = jnp.dot(onehot, rel_bias, precision=lax.Precision.HIGHEST)
    return out.reshape(tuple(np.shape(dist)) + (N_HEADS,))


def _bias_tables(rel_bias, seq, past_len):
    c = np.arange(QT)[:, None]
    r = np.arange(QT)[None, :]

    def key_major(t):
        rows = t.shape[0]
        return t.reshape(rows, QT, N_KV, GQA).transpose(2, 0, 3, 1).reshape(N_KV, rows, GQA * QT)

    def masked(t, keep):
        return jnp.where(jnp.asarray(np.tile(keep, (1, GQA)))[None], t, NEG)

    far = jnp.broadcast_to(rel_bias[REL_BUCKETS - 1].reshape(N_KV, 1, GQA, 1), (N_KV, QT, GQA, QT))
    far = far.reshape(N_KV, QT, GQA * QT)
    td = masked(key_major(_bias_lookup(rel_bias, r - c)), c <= r)
    ts = key_major(_bias_lookup(rel_bias, QT + r - c))
    we = masked(far, c > r)
    cf = far[:, :SUBLANES, :]
    j = np.arange(2 * QT)[:, None]
    dist_c = r - (CMP_STRIDE * (j - QT) + CMP_LEN - 1)
    bc = masked(key_major(_bias_lookup(rel_bias, dist_c)), dist_c >= 0)
    cc = np.arange(LANES)[:, None]
    n = np.arange(LANES)[None, :]
    ov = ((cc * CMP_STRIDE < n * SEL_BLOCK + SEL_BLOCK) & (cc * CMP_STRIDE + CMP_LEN > n * SEL_BLOCK) & (cc < N_CMP))
    ovt = jnp.asarray((ov & (n < seq // SEL_BLOCK)).T, dtype=bf16)
    ov_s = jnp.asarray(ov & (n < past_len // SEL_BLOCK + 1), dtype=bf16)
    eet = jnp.asarray((np.arange(seq)[:, None] // SEL_BLOCK) == np.arange(LANES)[None, :], dtype=bf16)

    def look_rows(dist):
        return _bias_lookup(rel_bias, dist).T

    bc_s = look_rows(past_len - (np.arange(LANES) * CMP_STRIDE + CMP_LEN - 1))
    bs_s = look_rows(past_len - np.arange(past_len))
    bw_s = look_rows(WINDOW - np.arange(WINDOW))
    b0_s = look_rows(np.zeros((LANES,), np.int64))
    return dict(td=td, ts=ts, cf=cf, we=we, bc=bc, ovt=ovt, eet=eet,
                ov_s=ov_s, bc_s=bc_s, bs_s=bs_s, bw_s=bw_s, b0_s=b0_s)


def _pair_order(w, axis):
    shape = w.shape
    w = w.reshape(shape[:axis] + (N_KV, GQA, HEAD_DIM) + shape[axis + 1:])
    return jnp.swapaxes(w, axis, axis + 1).reshape(shape)


def _repack_w_in(w):
    z = lambda n: jnp.zeros((w.shape[0], n), w.dtype)
    return jnp.concatenate([
        _pair_order(w[:, :512], 1), w[:, 512:1280],
        w[:, 1280:1304], z(LANES - 24),
        w[:, 1304:2072],
        w[:, 2072:2088], z(LANES - GLA_RANK),
        w[:, 2088:2344]], axis=1)


def _block_diag(blocks):
    n, r, c = blocks.shape
    eye = jnp.eye(n, dtype=blocks.dtype)
    return (eye[:, None, :, None] * blocks[:, :, None, :]).reshape(n * r, n * c)


def _layer_weights(l, p):
    tile2 = lambda v: jnp.tile(v, 2)[None, :]
    w_in = _repack_w_in(p['w_in'][l]).astype(bf16)
    w_out = jnp.concatenate([_pair_order(p['w_out'][l][:512], 0), p['w_out'][l][512:]], axis=0).astype(bf16)
    k_gain = jnp.concatenate([jnp.tile(p['k_gain'][l], (1, 2)), jnp.zeros((SUBLANES - 3, LANES), f32)], axis=0)
    w_alpha = jnp.concatenate([p['gla_w_alpha'][l], jnp.zeros((LANES - GLA_RANK, LANES), f32)], axis=0).astype(bf16)

    def cmp_w(kv):
        w1 = p['cmp_w1'][l, kv]
        two = jax.vmap(lambda w: _block_diag(jnp.stack([w, w])))(w1)
        wcat = jnp.concatenate([two[:CMP_STRIDE], two[CMP_STRIDE:]], axis=-1).astype(bf16)
        pe = jnp.tile(p['cmp_pe'][l, kv], (1, 2))
        pe = jnp.concatenate([jnp.broadcast_to(pe[:CMP_STRIDE, None, :], (CMP_STRIDE, SUBLANES, LANES)),
                              jnp.broadcast_to(pe[CMP_STRIDE:, None, :], (CMP_STRIDE, SUBLANES, LANES))], axis=1)
        w2 = _block_diag(jnp.stack([p['cmp_w2'][l, kv]] * 2)).astype(bf16)
        wcat = wcat.reshape(CMP_STRIDE // 2, 2 * LANES, 4 * LANES)
        pe = pe.reshape(CMP_STRIDE // 2, 2, 2 * SUBLANES, LANES).transpose(0, 2, 1, 3)
        pe = pe.reshape(CMP_STRIDE // 2, 2 * SUBLANES, 2 * LANES)
        return wcat, pe, w2

    wk, pek, w2k = cmp_w(0)
    wv, pev, w2v = cmp_w(1)
    w_up = p['w_ffn_up'][l].astype(bf16)
    conv_w = jnp.concatenate([p['ffn_conv_w'][l], jnp.zeros((SUBLANES - 3, D_FF), f32)], axis=0)
    return dict(
        g_mix=p['g_mix'][l][None, :], w_in=w_in, q_gain=tile2(p['q_gain'][l]), k_gain=k_gain,
        w_alpha=w_alpha, b_alpha=p['gla_b_alpha'][l][None, :],
        cmp_wk=wk, cmp_wv=wv, cmp_pek=pek, cmp_pev=pev, cmp_w2k=w2k, cmp_w2v=w2v,
        gla_norm=p['gla_norm'][l][None, :],
        pool_w=_block_diag(p['pool_w'][l]).astype(bf16), pool_scale=p['pool_scale'][l][None, :],
        w_out=w_out,
        g_ffn=p['g_ffn'][l][None, :],
        w_up_a=w_up[:, :D_FF], w_up_u=w_up[:, D_FF:], conv_w=conv_w, conv_b=p['ffn_conv_b'][l][None, :],
        w_down=p['w_ffn_down'][l].astype(bf16),
    )


def _prompt_layer(x, lw, tabs, nb, seq):
    tm = 512
    (q, ckv, skv, wkv, gt, gq, gk, gv, gr, gl, pu) = _in_proj(x, lw, tm)
    kc, vc = _compress_prompt(ckv, lw, nb, seq)
    o_nsa = _nsa_prompt(q, kc, vc, skv, wkv, gt, tabs, nb, seq)
    o_gla, st = _gla_prompt(gq, gk, gl, gv, gr, lw['gla_norm'], nb, seq)
    o_pool = _pool_prompt(pu, lw['pool_w'], lw['pool_scale'], nb, seq)
    x_mid, hn = _out_proj(o_nsa, o_gla, o_pool, x, lw['w_out'], lw['g_ffn'], tm)
    x_out, a_tail = _ffn(hn, hn, x_mid, lw, tm, seq // tm, False)
    win_tail = wkv.reshape(nb, seq, 2 * LANES)[:, seq - WINDOW:]
    st = st.reshape(nb, GLA_HEADS, GLA_DV, GLA_HEADS, GLA_DK)
    gla_state = jnp.stack([st[:, h, :, h, :] for h in range(GLA_HEADS)], axis=1).transpose(0, 1, 3, 2)
    pool_state = pu.reshape(nb, seq, 2 * LANES)[:, seq - POOL_BUF:]
    a_tail = a_tail.reshape(nb, seq // tm, SUBLANES, D_FF)
    conv_state = a_tail[:, -1, SUBLANES - 2:, :]
    return x_out, (ckv, skv, win_tail, gla_state, pool_state, conv_state)


def _sample_layer(l, x, lw, tabs, page_table, cache_cmp, cache_slc, win_all, state_gla, state_pool, state_conv):
    nb = x.shape[0]
    (q, ckv, skv, wkv, gt, gq, gk, gv, gr, gl, pu) = _in_proj(x, lw, nb)
    lane_hi = (np.arange(LANES) >= HEAD_DIM)
    own = jnp.asarray((lane_hi[None, :] == (np.arange(N_KV)[:, None] == 1)).astype(np.float32))
    qrows = (q.astype(f32).reshape(nb, 1, GQA, LANES) * own[None, :, None, :]).reshape(nb, N_HEADS, LANES)
    grows = jnp.pad(gt[:, :N_HEADS * 3].reshape(nb, N_HEADS, 3), ((0, 0), (0, 0), (0, LANES - 3)))
    knew = jnp.pad(jnp.concatenate([skv, wkv], axis=1).reshape(nb, 4, LANES), ((0, 0), (0, SUBLANES - 4), (0, 0)))
    o, win_next = _nsa_sample(l, page_table, cache_cmp, cache_slc, win_all, qrows, grows, knew, lw, tabs)
    o = o.reshape(nb, N_KV, GQA, N_KV, HEAD_DIM)
    o_nsa = jnp.stack([o[:, h, :, h, :] for h in range(N_KV)], axis=2).reshape(nb, GQA * LANES).astype(bf16)
    cols = jnp.stack([gq, gk, gl], axis=-1)
    o_gla, s_new = _gla_sample(cols, gv.reshape(nb, GLA_HEADS, GLA_DV),
                               state_gla[l].reshape(nb, GLA_HEADS * GLA_DK, GLA_DV),
                               gr.reshape(nb, GLA_HEADS, GLA_DV), lw['gla_norm'].reshape(GLA_HEADS, GLA_DV))
    o_gla = o_gla.reshape(nb, GLA_HEADS * GLA_DV).astype(bf16)
    o_pool = _pool_sample(pu, state_pool[l].transpose(1, 0, 2), lw['pool_w'], lw['pool_scale'])
    x_mid, hn = _out_proj(o_nsa, o_gla, o_pool, x, lw['w_out'], lw['g_ffn'], nb)
    x_out, a = _ffn(hn, state_conv[l].transpose(1, 0, 2), x_mid, lw, nb, 1, True)
    gla_state = s_new.reshape(nb, GLA_HEADS, GLA_DK, GLA_DV)
    return x_out, (ckv, skv, win_next, gla_state, pu, a)


def kernel(x_prompt, x_sample, cache_cmp_kv, cache_slc_kv, page_table, state_win_kv, state_gla, state_pool,
           state_ffn_conv, rel_bias, g_mix, w_in, q_gain, k_gain, cmp_pe, cmp_w1, cmp_w2, gla_w_alpha, gla_b_alpha,
           gla_norm, pool_w, pool_scale, w_out, g_ffn, w_ffn_up, ffn_conv_w, ffn_conv_b, w_ffn_down):
    nb, seq, _ = x_prompt.shape
    db = x_sample.shape[0]
    depth = w_in.shape[0]
    n_phys = cache_cmp_kv.shape[1]
    past_len = page_table.shape[1] * cache_cmp_kv.shape[2]
    assert (seq, past_len, page_table.shape[1], cache_cmp_kv.shape[2]) == (2048, 2048, N_PAGES, PAGE)
    assert state_win_kv.shape[2] == WINDOW and x_sample.shape[1] == 1
    params = dict(g_mix=g_mix, w_in=w_in, q_gain=q_gain, k_gain=k_gain, cmp_pe=cmp_pe, cmp_w1=cmp_w1, cmp_w2=cmp_w2,
                  gla_w_alpha=gla_w_alpha, gla_b_alpha=gla_b_alpha, gla_norm=gla_norm, pool_w=pool_w,
                  pool_scale=pool_scale, w_out=w_out, g_ffn=g_ffn, w_ffn_up=w_ffn_up, ffn_conv_w=ffn_conv_w,
                  ffn_conv_b=ffn_conv_b, w_ffn_down=w_ffn_down)
    tabs = _bias_tables(rel_bias, seq, past_len)
    cache_cmp = cache_cmp_kv.transpose(0, 1, 3, 4, 5, 2).reshape(depth, n_phys, 2, LANES, PAGE)
    cache_slc = cache_slc_kv.transpose(0, 1, 3, 4, 5, 2).reshape(depth, n_phys, 2, LANES, PAGE)
    win_all = state_win_kv.transpose(0, 1, 3, 4, 5, 2).reshape(depth, db, 2, LANES, WINDOW)
    xp = x_prompt.reshape(nb * seq, D_MODEL)
    xs = x_sample.reshape(db, D_MODEL)
    st_p = [[] for _ in range(6)]
    st_s = [[] for _ in range(6)]
    for l in range(depth):
        lw = _layer_weights(l, params)
        xp, new_p = _prompt_layer(xp, lw, tabs, nb, seq)
        xs, new_s = _sample_layer(l, xs, lw, tabs, page_table, cache_cmp, cache_slc, win_all, state_gla,
                                  state_pool, state_ffn_conv)
        for i in range(6):
            st_p[i].append(new_p[i])
            st_s[i].append(new_s[i])
    stk = lambda a: jnp.stack(a, axis=0)
    kv = lambda a, rows: a.reshape(depth, a.shape[1] // rows, rows, 2, N_KV, HEAD_DIM)
    outs_p = (kv(stk(st_p[0]), seq), kv(stk(st_p[1]), seq), kv(stk(st_p[2]).reshape(depth, nb * WINDOW, -1), WINDOW),
              stk(st_p[3]), stk(st_p[4]), stk(st_p[5]))
    shift_in = lambda old, new: jnp.concatenate([old[:, :, 1:], stk(new)[:, :, None, :]], axis=2)
    win_s = stk(st_s[2]).reshape(depth, db, 2, N_KV, HEAD_DIM, WINDOW).transpose(0, 1, 5, 2, 3, 4)
    outs_s = (kv(stk(st_s[0]), 1), kv(stk(st_s[1]), 1), win_s, stk(st_s[3]),
              shift_in(state_pool, st_s[4]), shift_in(state_ffn_conv, st_s[5]))
    return (xp.reshape(nb, seq, D_MODEL), xs.reshape(db, 1, D_MODEL), *outs_p, *outs_s)
```

```python
import functools
import math

import numpy as np
import jax
import jax.numpy as jnp
from jax import lax
from jax.experimental import pallas as pl
from jax.experimental.pallas import tpu as pltpu

f32 = jnp.float32
bf16 = jnp.bfloat16

D_MODEL = 1024
HEAD_DIM = 64
N_KV = 2
GQA = 4
N_HEADS = 8
CMP_STRIDE = 16
CMP_LEN = 32
N_CMP = 127
SEL_BLOCK = 64
N_SEL = 16
WINDOW = 512
ATT_SCALE = HEAD_DIM ** -0.5
REL_BUCKETS = 32
REL_MAX_DIST = 128
GLA_HEADS = 4
GLA_DK = 32
GLA_DV = 64
GLA_RANK = 16
GLA_TAU = 16.0
GLA_CHUNK = 64
POOL_WINDOWS = (2, 4, 8, 16)
POOL_BUF = 15
D_FF = 2816
EPS = 1e-6
NEG = -1e30
BIG = 1e9
PAGE = 128

LANES = 128
SUBLANES = 8
QT = 128
PROJ_W = 2560
VMEM_LIMIT = 56 * 1024 * 1024


def _cparams(sem):
    return pltpu.CompilerParams(dimension_semantics=sem, vmem_limit_bytes=VMEM_LIMIT)


def _dot(a, b):
    return jnp.dot(a, b, preferred_element_type=f32)


def _dot_nt(a, b):
    return lax.dot_general(a, b, (((1,), (1,)), ((), ())), preferred_element_type=f32)


def _dot_tn(a, b):
    return lax.dot_general(a, b, (((0,), (0,)), ((), ())), preferred_element_type=f32)


def _dot_hilo(a, b_bf16):
    hi = a.astype(bf16)
    lo = (a - hi.astype(f32)).astype(bf16)
    return _dot(hi, b_bf16) + _dot(lo, b_bf16)


def _pair_rmsnorm(y, gain):
    lane = lax.broadcasted_iota(jnp.int32, y.shape, y.ndim - 1)
    lo = lane < HEAD_DIM
    y2 = y * y
    s_lo = jnp.sum(jnp.where(lo, y2, 0.0), axis=-1, keepdims=True)
    s_hi = jnp.sum(jnp.where(lo, 0.0, y2), axis=-1, keepdims=True)
    ms = jnp.where(lo, s_lo, s_hi) * (1.0 / HEAD_DIM)
    return y * lax.rsqrt(ms + EPS) * gain


_C_Q, _C_CK, _C_CV, _C_SK, _C_SV, _C_WK, _C_WV = 0, 512, 640, 768, 896, 1024, 1152
_C_GATE, _C_GQ, _C_GK, _C_GV, _C_GR, _C_GLR, _C_PU = 1280, 1408, 1536, 1664, 1920, 2176, 2304


def _proj_kernel(x_ref, g_ref, w_ref, qg_ref, kg_ref, wa_ref, ba_ref,
                 q_o, ckv_o, skv_o, wkv_o, gt_o, gq_o, gk_o, gv_o, gr_o, gl_o, pu_o):
    x = x_ref[...]
    hn = (x * lax.rsqrt(jnp.mean(x * x, axis=-1, keepdims=True) + EPS) * g_ref[...]).astype(bf16)

    def proj(c0):
        y = _dot(hn, w_ref[:, c0:c0 + 2 * LANES])
        return y[:, :LANES], y[:, LANES:]

    for j in range(2):
        for half, y in enumerate(proj(_C_Q + 2 * j * LANES)):
            c = (2 * j + half) * LANES
            q_o[:, c:c + LANES] = _pair_rmsnorm(y, qg_ref[...]).astype(bf16)
    ckv_o[:, :LANES], ckv_o[:, LANES:] = proj(_C_CK)
    k, skv_o[:, LANES:] = proj(_C_SK)
    skv_o[:, :LANES] = _pair_rmsnorm(k, kg_ref[1:2, :])
    k, wkv_o[:, LANES:] = proj(_C_WK)
    wkv_o[:, :LANES] = _pair_rmsnorm(k, kg_ref[2:3, :])
    gate, gq_o[...] = proj(_C_GATE)
    gt_o[...] = jax.nn.sigmoid(gate)
    gk_o[...], gv_o[:, :LANES] = proj(_C_GK)
    gv_o[:, LANES:], gr_o[:, :LANES] = proj(_C_GV + LANES)
    gr_o[:, LANES:], glr = proj(_C_GR + LANES)
    z = _dot(glr.astype(bf16), wa_ref[...]) + ba_ref[...]
    gl_o[...] = jax.nn.log_sigmoid(z) * (1.0 / GLA_TAU)
    pu_o[:, :LANES], pu_o[:, LANES:] = proj(_C_PU)


def _in_proj(x, lw, tm):
    m = x.shape[0]
    row = lambda n: pl.BlockSpec((tm, n), lambda i: (i, 0))
    full = lambda a: pl.BlockSpec(a.shape, lambda i: (0,) * a.ndim)
    widths = (512, 256, 256, 256, 128, 128, 128, 256, 256, 128, 256)
    dtypes = (bf16,) + (f32,) * 10
    args = (x, lw['g_mix'], lw['w_in'], lw['q_gain'], lw['k_gain'], lw['w_alpha'], lw['b_alpha'])
    return pl.pallas_call(
        _proj_kernel,
        grid=(m // tm,),
        in_specs=[row(D_MODEL)] + [full(a) for a in args[1:]],
        out_specs=[row(n) for n in widths],
        out_shape=[jax.ShapeDtypeStruct((m, n), dt) for n, dt in zip(widths, dtypes)],
        compiler_params=_cparams(("parallel",)),
        name="in_proj",
    )(*args)


def _compress(get_x, w_ref, pe_ref, w2_ref):
    rows = get_x(0).shape[0]
    acc = jnp.zeros((rows + 2 * SUBLANES, 4 * LANES), f32)
    for s in range(0, CMP_STRIDE, 2):
        xs = jnp.concatenate([get_x(s), get_x(s + 1)], axis=1)
        xs = jnp.concatenate([xs, pe_ref[s // 2]], axis=0).astype(bf16)
        acc = acc + _dot(xs, w_ref[s // 2])
    a0 = acc[:rows, :2 * LANES]
    a1 = acc[:rows, 2 * LANES:]
    c0 = acc[rows:rows + 1, :2 * LANES] + acc[rows + SUBLANES:rows + SUBLANES + 1, 2 * LANES:]
    h = a0 + pltpu.roll(a1, rows - 1, 0) + c0
    return _dot(jax.nn.gelu(h).astype(bf16), w2_ref[...])


def _compress_prompt_kernel(ck_ref, cv_ref, wk_ref, wv_ref, pek_ref, pev_ref, w2k_ref, w2v_ref, kg_ref, kc_o, vc_o):
    kx = lambda s: ck_ref[pl.ds(s, LANES, stride=CMP_STRIDE), :]
    vx = lambda s: cv_ref[pl.ds(s, LANES, stride=CMP_STRIDE), :]
    kc_o[...] = _pair_rmsnorm(_compress(kx, wk_ref, pek_ref, w2k_ref), kg_ref[0:1, :])
    vc_o[...] = _compress(vx, wv_ref, pev_ref, w2v_ref)


def _compress_prompt(ckv, lw, nb, seq):
    full = lambda a: pl.BlockSpec(a.shape, lambda b: (0,) * a.ndim)
    kblk = pl.BlockSpec((seq, LANES), lambda b: (b, 0))
    vblk = pl.BlockSpec((seq, LANES), lambda b: (b, 1))
    outblk = pl.BlockSpec((None, LANES, LANES), lambda b: (b, 0, 0))
    ws = (lw['cmp_wk'], lw['cmp_wv'], lw['cmp_pek'], lw['cmp_pev'], lw['cmp_w2k'], lw['cmp_w2v'], lw['k_gain'])
    return pl.pallas_call(
        _compress_prompt_kernel,
        grid=(nb,),
        in_specs=[kblk, vblk] + [full(a) for a in ws],
        out_specs=[outblk, outblk],
        out_shape=[jax.ShapeDtypeStruct((nb, LANES, LANES), f32)] * 2,
        compiler_params=_cparams(("parallel",)),
        name="compress_prompt",
    )(ckv, ckv, *ws)


def _rank_select(score, n_blocks):
    blk = lax.broadcasted_iota(jnp.int32, score.shape, 1)
    rank = jnp.zeros(score.shape, f32)
    for m in range(n_blocks):
        sm = score[:, m:m + 1]
        beats = (sm > score) | ((sm == score) & (m < blk))
        rank = rank + jnp.where(beats, 1.0, 0.0)
    return jnp.where(rank < float(N_SEL), 1.0, 0.0)


def _rank_select_rows(score, n_blocks):
    blk = lax.broadcasted_iota(jnp.int32, score.shape, 0)
    rank = jnp.zeros(score.shape, f32)
    for m in range(n_blocks):
        sm = score[m:m + 1, :]
        beats = (sm > score) | ((sm == score) & (m < blk))
        rank = rank + jnp.where(beats, 1.0, 0.0)
    return jnp.where(rank < float(N_SEL), 1.0, 0.0)


def _nsa_prompt_kernel(q_ref, kc_ref, vc_ref, sk_ref, sv_ref, wk_ref, wv_ref, gt_ref,
                       td_ref, ts_ref, we_ref, bc_ref, ovt_ref, eet_ref, o_ref,
                       selx_ref, m_ref, l_ref, acc_ref):
    i = pl.program_id(1)
    lane = lax.broadcasted_iota(jnp.int32, (QT, LANES), 1)
    row = lax.broadcasted_iota(jnp.int32, (QT, LANES), 0)
    n_blk = selx_ref.shape[1] // SEL_BLOCK
    blk = lax.broadcasted_iota(jnp.int32, (n_blk, QT), 0)
    cur = (i * QT + lax.broadcasted_iota(jnp.int32, (n_blk, QT), 1)) >> 6
    gates_t = gt_ref[...].T
    outs = [[None, None] for _ in range(GQA)]

    def tile4(x):
        return jnp.concatenate([x] * GQA, axis=1)

    def scores(q4, k_ref, kt, bias, sel_head=None, live=None):
        off = pl.multiple_of(kt * QT, QT)
        s = _dot_nt(k_ref[pl.ds(off, QT), :].astype(bf16), q4)
        if bias is not None:
            s = s + bias
        if live is not None:
            s = s + jnp.where(live, 0.0, NEG)
        if sel_head is not None:
            s = s + tile4(selx_ref[sel_head, pl.ds(off, QT), :])
        return s, off

    def attend(v_ref, tiles, state=None):
        m = state[0] if state is not None else None
        for s, _ in tiles:
            ms = jnp.max(s, axis=0, keepdims=True)
            m = ms if m is None else jnp.maximum(m, ms)
        p = jnp.concatenate([jnp.exp(s - m) for s, _ in tiles], axis=0)
        v = jnp.concatenate([v_ref[pl.ds(off, QT), :] for _, off in tiles], axis=0)
        l = jnp.sum(p, axis=0, keepdims=True)
        acc = _dot_tn(v.astype(bf16), p.astype(bf16))
        if state is not None:
            alpha = jnp.exp(state[0] - m)
            l = l + alpha * state[1]
            acc = acc + alpha * state[2]
        return m, l, acc

    m_ref[...] = jnp.full(m_ref.shape, NEG, f32)
    l_ref[...] = jnp.zeros(l_ref.shape, f32)
    acc_ref[...] = jnp.zeros(acc_ref.shape, f32)

    q4s, o_cmps = [], []
    for h in range(N_KV):
        half = (lane >= HEAD_DIM) if h else (lane < HEAD_DIM)
        q4 = jnp.concatenate(
            [jnp.where(half, q_ref[:, g * LANES:(g + 1) * LANES] * jnp.asarray(ATT_SCALE, bf16), jnp.zeros((), bf16))
             for g in range(GQA)], axis=0)
        q4s.append(q4)
        bias_c = bc_ref[h, pl.ds(pl.multiple_of(LANES - SUBLANES * i, SUBLANES), LANES), :]
        s = _dot_nt(kc_ref[...].astype(bf16), q4) + bias_c
        e = jnp.exp(s - jnp.max(s, axis=0, keepdims=True))
        p = jnp.where(s > 0.5 * NEG, e / jnp.sum(e, axis=0, keepdims=True), 0.0)
        o_cmps.append(_dot_tn(vc_ref[...].astype(bf16), p.astype(bf16)))
        psum = p[:, 0:QT] + p[:, QT:2 * QT] + p[:, 2 * QT:3 * QT] + p[:, 3 * QT:4 * QT]
        hi = psum.astype(bf16)
        lo = (psum - hi.astype(f32)).astype(bf16)
        imp = (_dot(ovt_ref[...], hi) + _dot(ovt_ref[...], lo))[0:n_blk, :]
        forced = (blk == 0) | (blk == cur) | (blk == cur - 1)
        score = jnp.where(forced, BIG, jnp.where(blk <= cur, imp, -BIG))
        sel = _rank_select_rows(score, n_blk)
        sel = jnp.concatenate([sel, jnp.zeros((LANES - n_blk, QT), f32)], axis=0).astype(bf16)
        selx_ref[h] = (_dot(eet_ref[...], sel) - 1.0) * (-NEG)

    n_far = jnp.maximum(i - 1, 0)

    def far_pairs(j, carry):
        kt1 = 2 * j + 1
        tiles = [[scores(q4s[h], sk_ref, 2 * j, None, sel_head=h),
                  scores(q4s[h], sk_ref, jnp.minimum(kt1, i), None, sel_head=h, live=kt1 < n_far)]
                 for h in range(N_KV)]
        for h in range(N_KV):
            m_ref[h], l_ref[h], acc_ref[h] = attend(sv_ref, tiles[h], (m_ref[h], l_ref[h], acc_ref[h]))
        return carry

    lax.fori_loop(0, (n_far + 1) // 2, far_pairs, 0)

    def back_tile(back):
        return dict(kt=jnp.maximum(i - back, 0), live=(i >= back) if back else None)

    slc_tiles = [[scores(q4s[h], sk_ref, bias=tab_ref[h], sel_head=h, **back_tile(back))
                  for back, tab_ref in ((1, ts_ref), (0, td_ref))] for h in range(N_KV)]
    win_tiles = [[scores(q4s[h], wk_ref, bias=bias, **back_tile(back))
                  for back, bias in ((4, we_ref[h]), (3, None), (2, None), (1, ts_ref[h]), (0, td_ref[h]))]
                 for h in range(N_KV)]
    for h in range(N_KV):
        _, l_s, acc_s = attend(sv_ref, slc_tiles[h], (m_ref[h], l_ref[h], acc_ref[h]))
        _, l_w, acc_w = attend(wv_ref, win_tiles[h])
        o_slc = acc_s / l_s
        o_win = acc_w / l_w
        for g in range(GQA):
            c = (h * GQA + g) * 3
            sl = slice(g * QT, (g + 1) * QT)
            outs[g][h] = (gates_t[c:c + 1, :] * o_cmps[h][:, sl] + gates_t[c + 1:c + 2, :] * o_slc[:, sl]
                          + gates_t[c + 2:c + 3, :] * o_win[:, sl])
    for g in range(GQA):
        o_ref[:, g * LANES:(g + 1) * LANES] = jnp.where(row < HEAD_DIM, outs[g][0], outs[g][1]).T.astype(bf16)


def _nsa_prompt(q, kc, vc, skv, wkv, gt, tabs, nb, seq):
    nt = seq // QT
    tile = lambda n: pl.BlockSpec((QT, n), lambda b, i: (b * nt + i, 0))
    kblk = pl.BlockSpec((seq, LANES), lambda b, i: (b, 0))
    vblk = pl.BlockSpec((seq, LANES), lambda b, i: (b, 1))
    cblk = pl.BlockSpec((None, LANES, LANES), lambda b, i: (b, 0, 0))
    full = lambda a: pl.BlockSpec(a.shape, lambda b, i: (0,) * a.ndim)
    consts = (tabs['td'], tabs['ts'], tabs['we'], tabs['bc'], tabs['ovt'], tabs['eet'])
    return pl.pallas_call(
        _nsa_prompt_kernel,
        grid=(nb, nt),
        in_specs=[tile(512), cblk, cblk, kblk, vblk, kblk, vblk, tile(LANES)] + [full(a) for a in consts],
        out_specs=tile(512),
        out_shape=jax.ShapeDtypeStruct((nb * seq, 512), bf16),
        scratch_shapes=[pltpu.VMEM((N_KV, seq, QT), f32), pltpu.VMEM((N_KV, 1, GQA * QT), f32),
                        pltpu.VMEM((N_KV, 1, GQA * QT), f32), pltpu.VMEM((N_KV, LANES, GQA * QT), f32)],
        compiler_params=_cparams(("parallel", "arbitrary")),
        name="nsa_prompt",
    )(q, kc, vc, skv, skv, wkv, wkv, gt, *consts)


N_PAGES = 16
SAMPLE_SEQS_PER_STEP = 2


def _nsa_sample_kernel(n_seq, pt_ref, *refs):
    per_seq = 2 * N_PAGES
    seq_pages = [refs[t * per_seq:(t + 1) * per_seq] for t in range(n_seq)]
    (win_ref, qr_ref, gr_ref, knew_ref, wcat_k, wcat_v, pek_ref, pev_ref, w2k_ref, w2v_ref, kg_ref,
     bc_ref, bs_ref, bw_ref, b0_ref, ov_ref, o_ref, win_o, hist_ref) = refs[n_seq * per_seq:]

    for kind in range(2):
        for t in range(n_seq):
            for j in range(N_PAGES):
                hist_ref[kind, pl.ds((t * N_PAGES + j) * PAGE, PAGE), :] = seq_pages[t][j][kind].T

    def hist(kind):
        return lambda s: hist_ref[kind, pl.ds(s, n_seq * LANES, stride=CMP_STRIDE), :]

    kc_all = _pair_rmsnorm(_compress(hist(0), wcat_k, pek_ref, w2k_ref), kg_ref[0:1, :])
    vc_all = _compress(hist(1), wcat_v, pev_ref, w2v_ref)
    for t in range(n_seq):
        rows = slice(t * LANES, (t + 1) * LANES)
        _nsa_sample_one(t, kc_all[rows], vc_all[rows], seq_pages[t][N_PAGES:], win_ref, qr_ref, gr_ref,
                        knew_ref, bc_ref, bs_ref, bw_ref, b0_ref, ov_ref, o_ref, win_o)


def _nsa_sample_one(t, kc, vc, slc_pages, win_ref, qr_ref, gr_ref, knew_ref, bc_ref, bs_ref, bw_ref, b0_ref,
                    ov_ref, o_ref, win_o):
    lane = lax.broadcasted_iota(jnp.int32, (SUBLANES, LANES), 1)
    qf = qr_ref[t]
    qb = qf.astype(bf16)
    gates = gr_ref[t]
    knew = knew_ref[t]
    hi_rows = lax.broadcasted_iota(jnp.int32, (SUBLANES, LANES), 0) >= GQA
    own = (lane >= HEAD_DIM) == hi_rows

    s = _dot_nt(qb, kc.astype(bf16)) * ATT_SCALE + bc_ref[...]
    mskc = lane < N_CMP
    s = jnp.where(mskc, s, NEG)
    e = jnp.exp(s - jnp.max(s, axis=-1, keepdims=True))
    p = jnp.where(mskc, e / jnp.sum(e, axis=-1, keepdims=True), 0.0)
    o_cmp = _dot(p.astype(bf16), vc.astype(bf16))
    psum = jnp.broadcast_to(jnp.sum(p.reshape(N_KV, GQA, LANES), axis=1, keepdims=True), (N_KV, GQA, LANES))
    imp = _dot_hilo(psum.reshape(SUBLANES, LANES), ov_ref[...])
    n_blk = N_PAGES * PAGE // SEL_BLOCK + 1
    cur = n_blk - 1
    forced = (lane == 0) | (lane == cur) | (lane == cur - 1)
    score = jnp.where(forced, BIG, imp)
    score = jnp.where(lane < n_blk, score, -3e38)
    sel = _rank_select(score, n_blk)

    def softmax_av(s_parts, s_new, vt_parts, v_new):
        mx = s_new
        for sp in s_parts:
            mx = jnp.maximum(mx, jnp.max(sp, axis=-1, keepdims=True))
        p_new = jnp.exp(s_new - mx)
        den = p_new
        acc = p_new * v_new
        for sp, vt in zip(s_parts, vt_parts):
            pp = jnp.exp(sp - mx)
            den = den + jnp.sum(pp, axis=-1, keepdims=True)
            acc = acc + _dot_nt(pp.astype(bf16), vt.astype(bf16))
        return acc / den

    def new_row(r):
        return knew[r:r + 1, :]

    def score_new(krow):
        return jnp.sum(jnp.where(own, qf * krow, 0.0), axis=-1, keepdims=True) * ATT_SCALE + b0_ref[:, 0:1]

    s_parts = []
    for j in range(N_PAGES):
        sj = _dot(qb, slc_pages[j][0].astype(bf16)) * ATT_SCALE + bs_ref[:, j * PAGE:(j + 1) * PAGE]
        mj = jnp.where(lane < SEL_BLOCK, sel[:, 2 * j:2 * j + 1], sel[:, 2 * j + 1:2 * j + 2]) > 0.5
        s_parts.append(jnp.where(mj, sj, NEG))
    o_slc = softmax_av(s_parts, score_new(new_row(0)), [pg[1] for pg in slc_pages], new_row(1))

    w_parts = []
    for j in range(WINDOW // PAGE):
        kw = win_ref[t, 0, :, j * PAGE:(j + 1) * PAGE]
        sj = _dot(qb, kw.astype(bf16)) * ATT_SCALE + bw_ref[:, j * PAGE:(j + 1) * PAGE]
        if j == 0:
            sj = jnp.where(lane >= 1, sj, NEG)
        w_parts.append(sj)
    vt_parts = [win_ref[t, 1, :, j * PAGE:(j + 1) * PAGE] for j in range(WINDOW // PAGE)]
    o_win = softmax_av(w_parts, score_new(new_row(2)), vt_parts, new_row(3))

    o_ref[t] = gates[:, 0:1] * o_cmp + gates[:, 1:2] * o_slc + gates[:, 2:3] * o_win

    new_cols = knew.T
    last = lax.broadcasted_iota(jnp.int32, (LANES, WINDOW), 1) == WINDOW - 1
    for kv in range(2):
        win_o[t, kv] = jnp.where(last, new_cols[:, 2 + kv:3 + kv], pltpu.roll(win_ref[t, kv], WINDOW - 1, 1))


def _nsa_sample(layer, page_table, cache_cmp, cache_slc, win_state, qrows, grows, knew, lw, tabs):
    nb = page_table.shape[0]
    n_seq = SAMPLE_SEQS_PER_STEP

    def page_t(t, j):
        return pl.BlockSpec((None, None, 2, LANES, PAGE), lambda b, pt: (layer, pt[b * n_seq + t, j], 0, 0, 0))

    full = lambda a: pl.BlockSpec(a.shape, lambda b, pt: (0,) * a.ndim)
    per_b = pl.BlockSpec((n_seq, SUBLANES, LANES), lambda b, pt: (b, 0, 0))
    consts = (lw['cmp_wk'], lw['cmp_wv'], lw['cmp_pek'], lw['cmp_pev'], lw['cmp_w2k'], lw['cmp_w2v'], lw['k_gain'],
              tabs['bc_s'], tabs['bs_s'], tabs['bw_s'], tabs['b0_s'], tabs['ov_s'])
    page_specs, page_args = [], []
    for t in range(n_seq):
        page_specs += [page_t(t, j) for j in range(N_PAGES)] * 2
        page_args += [cache_cmp] * N_PAGES + [cache_slc] * N_PAGES
    in_specs = (page_specs
                + [pl.BlockSpec((None, n_seq, 2, LANES, WINDOW), lambda b, pt: (layer, b, 0, 0, 0)), per_b, per_b, per_b]
                + [full(a) for a in consts])
    return pl.pallas_call(
        functools.partial(_nsa_sample_kernel, n_seq),
        grid_spec=pltpu.PrefetchScalarGridSpec(
            num_scalar_prefetch=1, grid=(nb // n_seq,), in_specs=in_specs,
            out_specs=[per_b, pl.BlockSpec((n_seq, 2, LANES, WINDOW), lambda b, pt: (b, 0, 0, 0))],
            scratch_shapes=[pltpu.VMEM((2, n_seq * N_PAGES * PAGE, LANES), f32)]),
        out_shape=[jax.ShapeDtypeStruct((nb, SUBLANES, LANES), f32),
                   jax.ShapeDtypeStruct((nb, 2, LANES, WINDOW), f32)],
        compiler_params=_cparams(("parallel",)),
        name="nsa_sample",
    )(page_table, *page_args, win_state, qrows, grows, knew, *consts)


def _gla_out(o, gn_ref, gr):
    o = jnp.concatenate([_pair_rmsnorm(o[:, :LANES], gn_ref[:, :LANES]),
                         _pair_rmsnorm(o[:, LANES:], gn_ref[:, LANES:])], axis=1)
    return o * jax.nn.silu(gr)


def _gla_prompt_kernel(gq_ref, gk_ref, gl_ref, gv_ref, gr_ref, gn_ref, o_ref, st_o, st_ref):
    c_rows = GLA_CHUNK
    lane = lax.broadcasted_iota(jnp.int32, (c_rows, LANES), 1)
    row = lax.broadcasted_iota(jnp.int32, (c_rows, LANES), 0)
    tril = (lax.broadcasted_iota(jnp.int32, (c_rows, c_rows), 0)
            >= lax.broadcasted_iota(jnp.int32, (c_rows, c_rows), 1))
    lane_v = lax.broadcasted_iota(jnp.int32, (c_rows, GLA_HEADS * GLA_DV), 1)
    srow = lax.broadcasted_iota(jnp.int32, (GLA_HEADS * GLA_DV, LANES), 0)
    scol = lax.broadcasted_iota(jnp.int32, (GLA_HEADS * GLA_DV, LANES), 1)
    diag = (srow >> 6) == (scol >> 5)
    st_ref[...] = jnp.zeros(st_ref.shape, f32)
    n_seqs = st_ref.shape[0]
    seq = gq_ref.shape[0] // n_seqs

    def body(c, carry):
        for sq in range(n_seqs):
            chunk(sq, pl.multiple_of(sq * seq + c * c_rows, c_rows))
        return carry

    def chunk(sq, off):
        q = gq_ref[pl.ds(off, c_rows), :] * (GLA_DK ** -0.5)
        k = gk_ref[pl.ds(off, c_rows), :]
        v = gv_ref[pl.ds(off, c_rows), :]
        b = gl_ref[pl.ds(off, c_rows), :]
        sh = 1
        while sh < c_rows:
            b = b + jnp.where(row >= sh, pltpu.roll(b, sh, 0), 0.0)
            sh *= 2
        qd = q * jnp.exp(b)
        kd = (k * jnp.exp(-b)).astype(bf16)
        bl = b[c_rows - 1:c_rows, :]
        kk = (k * jnp.exp(bl - b)).astype(bf16)
        st = st_ref[sq]
        vb = v.astype(bf16)
        o = _dot_nt(qd.astype(bf16), st.astype(bf16))
        for hh in range(GLA_HEADS):
            qm = jnp.where((lane >> 5) == hh, qd, 0.0).astype(bf16)
            a = jnp.where(tril, _dot_nt(qm, kd), 0.0)
            o = o + jnp.where((lane_v >> 6) == hh, _dot(a.astype(bf16), vb), 0.0)
        st_ref[sq] = jnp.exp(bl) * st + jnp.where(diag, _dot_tn(vb, kk), 0.0)
        o_ref[pl.ds(off, c_rows), :] = _gla_out(o, gn_ref, gr_ref[pl.ds(off, c_rows), :]).astype(bf16)

    lax.fori_loop(0, seq // c_rows, body, 0)
    st_o[...] = st_ref[...]


GLA_SEQS_PER_STEP = 2


def _gla_prompt(gq, gk, gl, gv, gr, gn, nb, seq):
    ns = GLA_SEQS_PER_STEP
    blk = lambda n: pl.BlockSpec((ns * seq, n), lambda b: (b, 0))
    return pl.pallas_call(
        _gla_prompt_kernel,
        grid=(nb // ns,),
        in_specs=[blk(LANES), blk(LANES), blk(LANES), blk(2 * LANES), blk(2 * LANES),
                  pl.BlockSpec(gn.shape, lambda b: (0, 0))],
        out_specs=[blk(2 * LANES), pl.BlockSpec((ns, GLA_HEADS * GLA_DV, LANES), lambda b: (b, 0, 0))],
        out_shape=[jax.ShapeDtypeStruct((nb * seq, 2 * LANES), bf16),
                   jax.ShapeDtypeStruct((nb, GLA_HEADS * GLA_DV, LANES), f32)],
        scratch_shapes=[pltpu.VMEM((ns, GLA_HEADS * GLA_DV, LANES), f32)],
        compiler_params=_cparams(("parallel",)),
        name="gla_prompt",
    )(gq, gk, gl, gv, gr, gn)


def _gla_sample_kernel(cols_ref, v_ref, s0_ref, gr_ref, gn_ref, o_ref, s_o):
    nb = s0_ref.shape[0]
    c = cols_ref[...]
    q = c[:, :, 0:1] * (GLA_DK ** -0.5)
    k = c[:, :, 1:2]
    g = c[:, :, 2:3]
    eg = jnp.exp(g)
    qd = q * eg
    kd = k * jnp.exp(-g)
    s0 = s0_ref[...]
    v = v_ref[...]
    a = jnp.sum((qd * kd).reshape(nb, GLA_HEADS, GLA_DK, 1), axis=2)
    o = jnp.sum((qd * s0).reshape(nb, GLA_HEADS, GLA_DK, GLA_DV), axis=2) + a * v
    vexp = jnp.broadcast_to(v[:, :, None, :], (nb, GLA_HEADS, GLA_DK, GLA_DV)).reshape(nb, GLA_HEADS * GLA_DK, GLA_DV)
    s_o[...] = eg * s0 + k * vexp
    y = o * lax.rsqrt(jnp.mean(o * o, axis=-1, keepdims=True) + EPS) * gn_ref[...]
    o_ref[...] = y * jax.nn.silu(gr_ref[...])


def _gla_sample(cols, v, s0, gr, gn, bt=8):
    nb = s0.shape[0]
    blk = lambda a: pl.BlockSpec((bt,) + a.shape[1:], lambda i: (i,) + (0,) * (a.ndim - 1))
    return pl.pallas_call(
        _gla_sample_kernel,
        grid=(nb // bt,),
        in_specs=[blk(cols), blk(v), blk(s0), blk(gr), pl.BlockSpec(gn.shape, lambda i: (0, 0))],
        out_specs=[blk(v), blk(s0)],
        out_shape=[jax.ShapeDtypeStruct(v.shape, f32), jax.ShapeDtypeStruct(s0.shape, f32)],
        compiler_params=_cparams(("parallel",)),
        name="gla_sample",
    )(cols, v, s0, gr, gn)


def _pool_finish(sums, u, cnt_inv, w_ref, sc_ref):
    lane = lax.broadcasted_iota(jnp.int32, u.shape, 1)
    grp = lane >> 6
    s = jnp.where(grp == 0, sums[0], jnp.where(grp == 1, sums[1], jnp.where(grp == 2, sums[2], sums[3])))
    dlt = s * cnt_inv - u
    return _dot(dlt.astype(bf16), w_ref[...]) * sc_ref[...]


def _pool_prompt_kernel(u_ref, w_ref, sc_ref, o_ref):
    u = u_ref[...]
    row = lax.broadcasted_iota(jnp.int32, u.shape, 0)
    lane = lax.broadcasted_iota(jnp.int32, u.shape, 1)
    sums = []
    s = u
    sh = 1
    while sh < POOL_WINDOWS[-1]:
        s = s + jnp.where(row >= sh, pltpu.roll(s, sh, 0), 0.0)
        sums.append(s)
        sh *= 2
    grp = lane >> 6
    win = jnp.where(grp == 0, POOL_WINDOWS[0], jnp.where(grp == 1, POOL_WINDOWS[1],
                    jnp.where(grp == 2, POOL_WINDOWS[2], POOL_WINDOWS[3])))
    cnt = jnp.minimum(win, row + 1).astype(f32)
    o_ref[...] = _pool_finish(sums, u, 1.0 / cnt, w_ref, sc_ref).astype(bf16)


def _pool_prompt(pu, w, sc, nb, seq):
    blk = pl.BlockSpec((seq, 2 * LANES), lambda b: (b, 0))
    full = lambda a: pl.BlockSpec(a.shape, lambda b: (0,) * a.ndim)
    return pl.pallas_call(
        _pool_prompt_kernel,
        grid=(nb,),
        in_specs=[blk, full(w), full(sc)],
        out_specs=blk,
        out_shape=jax.ShapeDtypeStruct((nb * seq, 2 * LANES), bf16),
        compiler_params=_cparams(("parallel",)),
        name="pool_prompt",
    )(pu, w, sc)


def _pool_sample_kernel(u_ref, buf_ref, w_ref, sc_ref, o_ref):
    u = u_ref[...]
    lane = lax.broadcasted_iota(jnp.int32, u.shape, 1)
    sums = []
    s = u
    nxt = POOL_BUF - 1
    for win in POOL_WINDOWS:
        while POOL_BUF - nxt < win:
            s = s + buf_ref[nxt]
            nxt -= 1
        sums.append(s)
    grp = lane >> 6
    cnt_inv = jnp.where(grp == 0, 1.0 / POOL_WINDOWS[0], jnp.where(grp == 1, 1.0 / POOL_WINDOWS[1],
                        jnp.where(grp == 2, 1.0 / POOL_WINDOWS[2], 1.0 / POOL_WINDOWS[3])))
    o_ref[...] = _pool_finish(sums, u, cnt_inv, w_ref, sc_ref).astype(bf16)


def _pool_sample(pu, buf_t, w, sc):
    full = lambda a: pl.BlockSpec(a.shape, lambda i: (0,) * a.ndim)
    return pl.pallas_call(
        _pool_sample_kernel,
        grid=(1,),
        in_specs=[full(pu), full(buf_t), full(w), full(sc)],
        out_specs=full(pu),
        out_shape=jax.ShapeDtypeStruct(pu.shape, bf16),
        compiler_params=_cparams(("arbitrary",)),
        name="pool_sample",
    )(pu, buf_t, w, sc)


def _out_proj_kernel(a_ref, b_ref, c_ref, x_ref, w_ref, g_ref, x_o, h_o):
    y = (_dot(a_ref[...], w_ref[0:512, :]) + _dot(b_ref[...], w_ref[512:768, :])
         + _dot(c_ref[...], w_ref[768:1024, :]))
    x = x_ref[...] + y
    x_o[...] = x
    h_o[...] = (x * lax.rsqrt(jnp.mean(x * x, axis=-1, keepdims=True) + EPS) * g_ref[...]).astype(bf16)


def _out_proj(o_nsa, o_gla, o_pool, x, w, g, tm):
    m = x.shape[0]
    row = lambda n: pl.BlockSpec((tm, n), lambda i: (i, 0))
    full = lambda a: pl.BlockSpec(a.shape, lambda i: (0,) * a.ndim)
    return pl.pallas_call(
        _out_proj_kernel,
        grid=(m // tm,),
        in_specs=[row(512), row(256), row(256), row(D_MODEL), full(w), full(g)],
        out_specs=[row(D_MODEL), row(D_MODEL)],
        out_shape=[jax.ShapeDtypeStruct((m, D_MODEL), f32), jax.ShapeDtypeStruct((m, D_MODEL), bf16)],
        compiler_params=_cparams(("parallel",)),
        name="out_proj",
    )(o_nsa, o_gla, o_pool, x, w, g)


FF_CHUNK = 2816
HALO = 16


def _ffn_kernel(tiles_per_seq, decode, h_ref, p_ref, x_ref, wa_ref, wu_ref, cw_ref, cb_ref, wd_ref, x_o, a_o):
    hn = h_ref[...]
    tm = hn.shape[0]
    row = lax.broadcasted_iota(jnp.int32, (tm, FF_CHUNK), 0)
    if not decode:
        first = (pl.program_id(0) % tiles_per_seq) == 0
        hx = jnp.concatenate([hn, p_ref[...]], axis=0)
    acc = jnp.zeros((tm, D_MODEL), f32)
    for c in range(D_FF // FF_CHUNK):
        cs = slice(c * FF_CHUNK, (c + 1) * FF_CHUNK)
        u = _dot(hn, wu_ref[:, cs])
        if decode:
            a = _dot(hn, wa_ref[:, cs])
            a2 = p_ref[0, :, cs]
            a1 = p_ref[1, :, cs]
            a_o[:, cs] = a
        else:
            ax = _dot(hx, wa_ref[:, cs])
            a = ax[:tm]
            ah = jnp.where(first, 0.0, ax[tm:])
            p1 = ah[HALO - 1:HALO, :]
            p2 = ah[HALO - 2:HALO - 1, :]
            a1 = jnp.where(row == 0, p1, pltpu.roll(a, 1, 0))
            a2 = jnp.where(row == 0, p2, jnp.where(row == 1, p1, pltpu.roll(a, 2, 0)))
            a_o[:, cs] = a[tm - SUBLANES:, :]
        ac = cb_ref[:, cs] + a2 * cw_ref[0:1, cs] + a1 * cw_ref[1:2, cs] + a * cw_ref[2:3, cs]
        act = (jax.nn.silu(ac) * u).astype(bf16)
        acc = acc + _dot(act, wd_ref[cs, :])
    x_o[...] = x_ref[...] + acc


def _ffn(hn, prev, x, lw, tm, tiles_per_seq, decode):
    m = x.shape[0]
    nt = m // tm
    row = lambda n: pl.BlockSpec((tm, n), lambda i: (i, 0))
    const = lambda a: pl.BlockSpec(a.shape, lambda i: (0,) * a.ndim, pipeline_mode=pl.Buffered(1))
    if decode:
        prev_spec = pl.BlockSpec(prev.shape, lambda i: (0, 0, 0))
        a_spec = row(D_FF)
        a_shape = jax.ShapeDtypeStruct((m, D_FF), f32)
    else:
        per = tm // HALO
        prev_spec = pl.BlockSpec((HALO, D_MODEL), lambda i: (jnp.maximum(i * per - 1, 0), 0))
        a_spec = pl.BlockSpec((None, SUBLANES, D_FF), lambda i: (i, 0, 0))
        a_shape = jax.ShapeDtypeStruct((nt, SUBLANES, D_FF), f32)
    ws = (lw['w_up_a'], lw['w_up_u'], lw['conv_w'], lw['conv_b'], lw['w_down'])
    return pl.pallas_call(
        functools.partial(_ffn_kernel, tiles_per_seq, decode),
        grid=(nt,),
        in_specs=[row(D_MODEL), prev_spec, row(D_MODEL)] + [const(a) for a in ws],
        out_specs=[row(D_MODEL), a_spec],
        out_shape=[jax.ShapeDtypeStruct((m, D_MODEL), f32), a_shape],
        compiler_params=_cparams(("parallel",)),
        name="ffn_decode" if decode else "ffn_prompt",
    )(hn, prev, x, *ws)


def _rel_bucket_np(d):
    d = np.maximum(d, 0)
    exact = REL_BUCKETS // 2
    lg = np.log(np.maximum(d, 1).astype(np.float32) / np.float32(exact)) / np.float32(math.log(REL_MAX_DIST / exact))
    large = np.minimum(exact + (lg * np.float32(REL_BUCKETS - exact)).astype(np.int32), REL_BUCKETS - 1)
    return np.where(d < exact, d, large).astype(np.int32)


def _bias_lookup(rel_bias, dist):
    bucket = _rel_bucket_np(np.asarray(dist)).reshape(-1)
    onehot = jnp.asarray((bucket[:, None] == np.arange(REL_BUCKETS)[None, :]).astype(np.int8)).astype(f32)
    out = jnp.dot(onehot, rel_bias, precision=lax.Precision.HIGHEST)
    return out.reshape(tuple(np.shape(dist)) + (N_HEADS,))


def _bias_tables(rel_bias, seq, past_len):
    c = np.arange(QT)[:, None]
    r = np.arange(QT)[None, :]

    def key_major(t):
        rows = t.shape[0]
        return t.reshape(rows, QT, N_KV, GQA).transpose(2, 0, 3, 1).reshape(N_KV, rows, GQA * QT)

    def masked(t, keep):
        return jnp.where(jnp.asarray(np.tile(keep, (1, GQA)))[None], t, NEG)

    far = jnp.broadcast_to(rel_bias[REL_BUCKETS - 1].reshape(N_KV, 1, GQA, 1), (N_KV, QT, GQA, QT))
    far = far.reshape(N_KV, QT, GQA * QT)
    td = masked(key_major(_bias_lookup(rel_bias, r - c)) - far, c <= r)
    ts = key_major(_bias_lookup(rel_bias, QT + r - c)) - far
    we = masked(jnp.zeros_like(far), c > r)
    j = np.arange(2 * QT)[:, None]
    dist_c = r - (CMP_STRIDE * (j - QT) + CMP_LEN - 1)
    bc = masked(key_major(_bias_lookup(rel_bias, dist_c)), dist_c >= 0)
    cc = np.arange(LANES)[:, None]
    n = np.arange(LANES)[None, :]
    ov = ((cc * CMP_STRIDE < n * SEL_BLOCK + SEL_BLOCK) & (cc * CMP_STRIDE + CMP_LEN > n * SEL_BLOCK) & (cc < N_CMP))
    ovt = jnp.asarray((ov & (n < seq // SEL_BLOCK)).T, dtype=bf16)
    ov_s = jnp.asarray(ov & (n < past_len // SEL_BLOCK + 1), dtype=bf16)
    eet = jnp.asarray((np.arange(seq)[:, None] // SEL_BLOCK) == np.arange(LANES)[None, :], dtype=bf16)

    def look_rows(dist):
        return _bias_lookup(rel_bias, dist).T

    bc_s = look_rows(past_len - (np.arange(LANES) * CMP_STRIDE + CMP_LEN - 1))
    bs_s = look_rows(past_len - np.arange(past_len))
    bw_s = look_rows(WINDOW - np.arange(WINDOW))
    b0_s = look_rows(np.zeros((LANES,), np.int64))
    return dict(td=td, ts=ts, we=we, bc=bc, ovt=ovt, eet=eet,
                ov_s=ov_s, bc_s=bc_s, bs_s=bs_s, bw_s=bw_s, b0_s=b0_s)


def _pair_order(w, axis):
    shape = w.shape
    w = w.reshape(shape[:axis] + (N_KV, GQA, HEAD_DIM) + shape[axis + 1:])
    return jnp.swapaxes(w, axis, axis + 1).reshape(shape)


def _repack_w_in(w):
    z = lambda n: jnp.zeros((w.shape[0], n), w.dtype)
    return jnp.concatenate([
        _pair_order(w[:, :512], 1), w[:, 512:1280],
        w[:, 1280:1304], z(LANES - 24),
        w[:, 1304:2072],
        w[:, 2072:2088], z(LANES - GLA_RANK),
        w[:, 2088:2344]], axis=1)


def _block_diag(blocks):
    n, r, c = blocks.shape
    eye = jnp.eye(n, dtype=blocks.dtype)
    return (eye[:, None, :, None] * blocks[:, :, None, :]).reshape(n * r, n * c)


def _layer_weights(l, p):
    tile2 = lambda v: jnp.tile(v, 2)[None, :]
    w_in = _repack_w_in(p['w_in'][l]).astype(bf16)
    w_out = jnp.concatenate([_pair_order(p['w_out'][l][:512], 0), p['w_out'][l][512:]], axis=0).astype(bf16)
    k_gain = jnp.concatenate([jnp.tile(p['k_gain'][l], (1, 2)), jnp.zeros((SUBLANES - 3, LANES), f32)], axis=0)
    w_alpha = jnp.concatenate([p['gla_w_alpha'][l], jnp.zeros((LANES - GLA_RANK, LANES), f32)], axis=0).astype(bf16)

    def cmp_w(kv):
        w1 = p['cmp_w1'][l, kv]
        two = jax.vmap(lambda w: _block_diag(jnp.stack([w, w])))(w1)
        wcat = jnp.concatenate([two[:CMP_STRIDE], two[CMP_STRIDE:]], axis=-1).astype(bf16)
        pe = jnp.tile(p['cmp_pe'][l, kv], (1, 2))
        pe = jnp.concatenate([jnp.broadcast_to(pe[:CMP_STRIDE, None, :], (CMP_STRIDE, SUBLANES, LANES)),
                              jnp.broadcast_to(pe[CMP_STRIDE:, None, :], (CMP_STRIDE, SUBLANES, LANES))], axis=1)
        w2 = _block_diag(jnp.stack([p['cmp_w2'][l, kv]] * 2)).astype(bf16)
        wcat = wcat.reshape(CMP_STRIDE // 2, 2 * LANES, 4 * LANES)
        pe = pe.reshape(CMP_STRIDE // 2, 2, 2 * SUBLANES, LANES).transpose(0, 2, 1, 3)
        pe = pe.reshape(CMP_STRIDE // 2, 2 * SUBLANES, 2 * LANES)
        return wcat, pe, w2

    wk, pek, w2k = cmp_w(0)
    wv, pev, w2v = cmp_w(1)
    w_up = p['w_ffn_up'][l].astype(bf16)
    conv_w = jnp.concatenate([p['ffn_conv_w'][l], jnp.zeros((SUBLANES - 3, D_FF), f32)], axis=0)
    return dict(
        g_mix=p['g_mix'][l][None, :], w_in=w_in, q_gain=tile2(p['q_gain'][l]), k_gain=k_gain,
        w_alpha=w_alpha, b_alpha=p['gla_b_alpha'][l][None, :],
        cmp_wk=wk, cmp_wv=wv, cmp_pek=pek, cmp_pev=pev, cmp_w2k=w2k, cmp_w2v=w2v,
        gla_norm=p['gla_norm'][l][None, :],
        pool_w=_block_diag(p['pool_w'][l]).astype(bf16), pool_scale=p['pool_scale'][l][None, :],
        w_out=w_out,
        g_ffn=p['g_ffn'][l][None, :],
        w_up_a=w_up[:, :D_FF], w_up_u=w_up[:, D_FF:], conv_w=conv_w, conv_b=p['ffn_conv_b'][l][None, :],
        w_down=p['w_ffn_down'][l].astype(bf16),
    )


def _prompt_layer(x, lw, tabs, nb, seq):
    tm = 512
    (q, ckv, skv, wkv, gt, gq, gk, gv, gr, gl, pu) = _in_proj(x, lw, tm)
    kc, vc = _compress_prompt(ckv, lw, nb, seq)
    o_nsa = _nsa_prompt(q, kc, vc, skv, wkv, gt, tabs, nb, seq)
    o_gla, st = _gla_prompt(gq, gk, gl, gv, gr, lw['gla_norm'], nb, seq)
    o_pool = _pool_prompt(pu, lw['pool_w'], lw['pool_scale'], nb, seq)
    x_mid, hn = _out_proj(o_nsa, o_gla, o_pool, x, lw['w_out'], lw['g_ffn'], tm)
    x_out, a_tail = _ffn(hn, hn, x_mid, lw, tm, seq // tm, False)
    win_tail = wkv.reshape(nb, seq, 2 * LANES)[:, seq - WINDOW:]
    st = st.reshape(nb, GLA_HEADS, GLA_DV, GLA_HEADS, GLA_DK)
    gla_state = jnp.stack([st[:, h, :, h, :] for h in range(GLA_HEADS)], axis=1).transpose(0, 1, 3, 2)
    pool_state = pu.reshape(nb, seq, 2 * LANES)[:, seq - POOL_BUF:]
    a_tail = a_tail.reshape(nb, seq // tm, SUBLANES, D_FF)
    conv_state = a_tail[:, -1, SUBLANES - 2:, :]
    return x_out, (ckv, skv, win_tail, gla_state, pool_state, conv_state)


def _sample_layer(l, x, lw, tabs, page_table, cache_cmp, cache_slc, win_all, state_gla, state_pool, state_conv):
    nb = x.shape[0]
    (q, ckv, skv, wkv, gt, gq, gk, gv, gr, gl, pu) = _in_proj(x, lw, nb)
    lane_hi = (np.arange(LANES) >= HEAD_DIM)
    own = jnp.asarray((lane_hi[None, :] == (np.arange(N_KV)[:, None] == 1)).astype(np.float32))
    qrows = (q.astype(f32).reshape(nb, 1, GQA, LANES) * own[None, :, None, :]).reshape(nb, N_HEADS, LANES)
    grows = jnp.pad(gt[:, :N_HEADS * 3].reshape(nb, N_HEADS, 3), ((0, 0), (0, 0), (0, LANES - 3)))
    knew = jnp.pad(jnp.concatenate([skv, wkv], axis=1).reshape(nb, 4, LANES), ((0, 0), (0, SUBLANES - 4), (0, 0)))
    o, win_next = _nsa_sample(l, page_table, cache_cmp, cache_slc, win_all, qrows, grows, knew, lw, tabs)
    o = o.reshape(nb, N_KV, GQA, N_KV, HEAD_DIM)
    o_nsa = jnp.stack([o[:, h, :, h, :] for h in range(N_KV)], axis=2).reshape(nb, GQA * LANES).astype(bf16)
    cols = jnp.stack([gq, gk, gl], axis=-1)
    o_gla, s_new = _gla_sample(cols, gv.reshape(nb, GLA_HEADS, GLA_DV),
                               state_gla[l].reshape(nb, GLA_HEADS * GLA_DK, GLA_DV),
                               gr.reshape(nb, GLA_HEADS, GLA_DV), lw['gla_norm'].reshape(GLA_HEADS, GLA_DV))
    o_gla = o_gla.reshape(nb, GLA_HEADS * GLA_DV).astype(bf16)
    o_pool = _pool_sample(pu, state_pool[l].transpose(1, 0, 2), lw['pool_w'], lw['pool_scale'])
    x_mid, hn = _out_proj(o_nsa, o_gla, o_pool, x, lw['w_out'], lw['g_ffn'], nb)
    x_out, a = _ffn(hn, state_conv[l].transpose(1, 0, 2), x_mid, lw, nb, 1, True)
    gla_state = s_new.reshape(nb, GLA_HEADS, GLA_DK, GLA_DV)
    return x_out, (ckv, skv, win_next, gla_state, pu, a)


def kernel(x_prompt, x_sample, cache_cmp_kv, cache_slc_kv, page_table, state_win_kv, state_gla, state_pool,
           state_ffn_conv, rel_bias, g_mix, w_in, q_gain, k_gain, cmp_pe, cmp_w1, cmp_w2, gla_w_alpha, gla_b_alpha,
           gla_norm, pool_w, pool_scale, w_out, g_ffn, w_ffn_up, ffn_conv_w, ffn_conv_b, w_ffn_down):
    nb, seq, _ = x_prompt.shape
    db = x_sample.shape[0]
    depth = w_in.shape[0]
    n_phys = cache_cmp_kv.shape[1]
    past_len = page_table.shape[1] * cache_cmp_kv.shape[2]
    assert (seq, past_len, page_table.shape[1], cache_cmp_kv.shape[2]) == (2048, 2048, N_PAGES, PAGE)
    assert state_win_kv.shape[2] == WINDOW and x_sample.shape[1] == 1
    params = dict(g_mix=g_mix, w_in=w_in, q_gain=q_gain, k_gain=k_gain, cmp_pe=cmp_pe, cmp_w1=cmp_w1, cmp_w2=cmp_w2,
                  gla_w_alpha=gla_w_alpha, gla_b_alpha=gla_b_alpha, gla_norm=gla_norm, pool_w=pool_w,
                  pool_scale=pool_scale, w_out=w_out, g_ffn=g_ffn, w_ffn_up=w_ffn_up, ffn_conv_w=ffn_conv_w,
                  ffn_conv_b=ffn_conv_b, w_ffn_down=w_ffn_down)
    tabs = _bias_tables(rel_bias, seq, past_len)
    cache_cmp = cache_cmp_kv.transpose(0, 1, 3, 4, 5, 2).reshape(depth, n_phys, 2, LANES, PAGE)
    cache_slc = cache_slc_kv.transpose(0, 1, 3, 4, 5, 2).reshape(depth, n_phys, 2, LANES, PAGE)
    win_all = state_win_kv.transpose(0, 1, 3, 4, 5, 2).reshape(depth, db, 2, LANES, WINDOW)
    xp = x_prompt.reshape(nb * seq, D_MODEL)
    xs = x_sample.reshape(db, D_MODEL)
    st_p = [[] for _ in range(6)]
    st_s = [[] for _ in range(6)]
    for l in range(depth):
        lw = _layer_weights(l, params)
        xp, new_p = _prompt_layer(xp, lw, tabs, nb, seq)
        xs, new_s = _sample_layer(l, xs, lw, tabs, page_table, cache_cmp, cache_slc, win_all, state_gla,
                                  state_pool, state_ffn_conv)
        for i in range(6):
            st_p[i].append(new_p[i])
            st_s[i].append(new_s[i])
    stk = lambda a: jnp.stack(a, axis=0)
    kv = lambda a, rows: a.reshape(depth, a.shape[1] // rows, rows, 2, N_KV, HEAD_DIM)
    outs_p = (kv(stk(st_p[0]), seq), kv(stk(st_p[1]), seq), kv(stk(st_p[2]).reshape(depth, nb * WINDOW, -1), WINDOW),
              stk(st_p[3]), stk(st_p[4]), stk(st_p[5]))
    shift_in = lambda old, new: jnp.concatenate([old[:, :, 1:], stk(new)[:, :, None, :]], axis=2)
    win_s = stk(st_s[2]).reshape(depth, db, 2, N_KV, HEAD_DIM, WINDOW).transpose(0, 1, 5, 2, 3, 4)
    outs_s = (kv(stk(st_s[0]), 1), kv(stk(st_s[1]), 1), win_s, stk(st_s[3]),
              shift_in(state_pool, st_s[4]), shift_in(state_ffn_conv, st_s[5]))
    return (xp.reshape(nb, seq, D_MODEL), xs.reshape(db, 1, D_MODEL), *outs_p, *outs_s)
```

```python
import functools
import math

import numpy as np
import jax
import jax.numpy as jnp
from jax import lax
from jax.experimental import pallas as pl
from jax.experimental.pallas import tpu as pltpu

f32 = jnp.float32
bf16 = jnp.bfloat16

D_MODEL = 1024
HEAD_DIM = 64
N_KV = 2
GQA = 4
N_HEADS = 8
CMP_STRIDE = 16
CMP_LEN = 32
N_CMP = 127
SEL_BLOCK = 64
N_SEL = 16
WINDOW = 512
ATT_SCALE = HEAD_DIM ** -0.5
REL_BUCKETS = 32
REL_MAX_DIST = 128
GLA_HEADS = 4
GLA_DK = 32
GLA_DV = 64
GLA_RANK = 16
GLA_TAU = 16.0
GLA_CHUNK = 64
POOL_WINDOWS = (2, 4, 8, 16)
POOL_BUF = 15
D_FF = 2816
EPS = 1e-6
NEG = -1e30
BIG = 1e9
PAGE = 128

LANES = 128
SUBLANES = 8
QT = 128
PROJ_W = 2560
VMEM_LIMIT = 56 * 1024 * 1024


def _cparams(sem):
    return pltpu.CompilerParams(dimension_semantics=sem, vmem_limit_bytes=VMEM_LIMIT)


def _dot(a, b):
    return jnp.dot(a, b, preferred_element_type=f32)


def _dot_nt(a, b):
    return lax.dot_general(a, b, (((1,), (1,)), ((), ())), preferred_element_type=f32)


def _dot_tn(a, b):
    return lax.dot_general(a, b, (((0,), (0,)), ((), ())), preferred_element_type=f32)


def _dot_hilo(a, b_bf16):
    hi = a.astype(bf16)
    lo = (a - hi.astype(f32)).astype(bf16)
    return _dot(hi, b_bf16) + _dot(lo, b_bf16)


def _pair_rmsnorm(y, gain):
    lane = lax.broadcasted_iota(jnp.int32, y.shape, y.ndim - 1)
    lo = lane < HEAD_DIM
    y2 = y * y
    s_lo = jnp.sum(jnp.where(lo, y2, 0.0), axis=-1, keepdims=True)
    s_hi = jnp.sum(jnp.where(lo, 0.0, y2), axis=-1, keepdims=True)
    ms = jnp.where(lo, s_lo, s_hi) * (1.0 / HEAD_DIM)
    return y * lax.rsqrt(ms + EPS) * gain


_C_Q, _C_CK, _C_CV, _C_SK, _C_SV, _C_WK, _C_WV = 0, 512, 640, 768, 896, 1024, 1152
_C_GATE, _C_GQ, _C_GK, _C_GV, _C_GR, _C_GLR, _C_PU = 1280, 1408, 1536, 1664, 1920, 2176, 2304


def _proj_kernel(x_ref, g_ref, w_ref, qg_ref, kg_ref, wa_ref, ba_ref,
                 q_o, ckv_o, skv_o, wkv_o, gt_o, gq_o, gk_o, gv_o, gr_o, gl_o, pu_o):
    x = x_ref[...]
    hn = (x * lax.rsqrt(jnp.mean(x * x, axis=-1, keepdims=True) + EPS) * g_ref[...]).astype(bf16)

    def proj(c0):
        y = _dot(hn, w_ref[:, c0:c0 + 2 * LANES])
        return y[:, :LANES], y[:, LANES:]

    for j in range(2):
        for half, y in enumerate(proj(_C_Q + 2 * j * LANES)):
            c = (2 * j + half) * LANES
            q_o[:, c:c + LANES] = _pair_rmsnorm(y, qg_ref[...]).astype(bf16)
    ckv_o[:, :LANES], ckv_o[:, LANES:] = proj(_C_CK)
    k, skv_o[:, LANES:] = proj(_C_SK)
    skv_o[:, :LANES] = _pair_rmsnorm(k, kg_ref[1:2, :])
    k, wkv_o[:, LANES:] = proj(_C_WK)
    wkv_o[:, :LANES] = _pair_rmsnorm(k, kg_ref[2:3, :])
    gate, gq_o[...] = proj(_C_GATE)
    gt_o[...] = jax.nn.sigmoid(gate)
    gk_o[...], gv_o[:, :LANES] = proj(_C_GK)
    gv_o[:, LANES:], gr_o[:, :LANES] = proj(_C_GV + LANES)
    gr_o[:, LANES:], glr = proj(_C_GR + LANES)
    z = _dot(glr.astype(bf16), wa_ref[...]) + ba_ref[...]
    gl_o[...] = jax.nn.log_sigmoid(z) * (1.0 / GLA_TAU)
    pu_o[:, :LANES], pu_o[:, LANES:] = proj(_C_PU)


def _in_proj(x, lw, tm):
    m = x.shape[0]
    row = lambda n: pl.BlockSpec((tm, n), lambda i: (i, 0))
    full = lambda a: pl.BlockSpec(a.shape, lambda i: (0,) * a.ndim)
    widths = (512, 256, 256, 256, 128, 128, 128, 256, 256, 128, 256)
    dtypes = (bf16,) + (f32,) * 10
    args = (x, lw['g_mix'], lw['w_in'], lw['q_gain'], lw['k_gain'], lw['w_alpha'], lw['b_alpha'])
    return pl.pallas_call(
        _proj_kernel,
        grid=(m // tm,),
        in_specs=[row(D_MODEL)] + [full(a) for a in args[1:]],
        out_specs=[row(n) for n in widths],
        out_shape=[jax.ShapeDtypeStruct((m, n), dt) for n, dt in zip(widths, dtypes)],
        compiler_params=_cparams(("parallel",)),
        name="in_proj",
    )(*args)


def _compress(get_x, w_ref, pe_ref, w2_ref):
    rows = get_x(0).shape[0]
    acc = jnp.zeros((rows + 2 * SUBLANES, 4 * LANES), f32)
    for s in range(0, CMP_STRIDE, 2):
        xs = jnp.concatenate([get_x(s), get_x(s + 1)], axis=1)
        xs = jnp.concatenate([xs, pe_ref[s // 2]], axis=0).astype(bf16)
        acc = acc + _dot(xs, w_ref[s // 2])
    a0 = acc[:rows, :2 * LANES]
    a1 = acc[:rows, 2 * LANES:]
    c0 = acc[rows:rows + 1, :2 * LANES] + acc[rows + SUBLANES:rows + SUBLANES + 1, 2 * LANES:]
    h = a0 + pltpu.roll(a1, rows - 1, 0) + c0
    return _dot(jax.nn.gelu(h).astype(bf16), w2_ref[...])


def _compress_prompt_kernel(ck_ref, cv_ref, wk_ref, wv_ref, pek_ref, pev_ref, w2k_ref, w2v_ref, kg_ref, kc_o, vc_o):
    kx = lambda s: ck_ref[pl.ds(s, LANES, stride=CMP_STRIDE), :]
    vx = lambda s: cv_ref[pl.ds(s, LANES, stride=CMP_STRIDE), :]
    kc_o[...] = _pair_rmsnorm(_compress(kx, wk_ref, pek_ref, w2k_ref), kg_ref[0:1, :])
    vc_o[...] = _compress(vx, wv_ref, pev_ref, w2v_ref)


def _compress_prompt(ckv, lw, nb, seq):
    full = lambda a: pl.BlockSpec(a.shape, lambda b: (0,) * a.ndim)
    kblk = pl.BlockSpec((seq, LANES), lambda b: (b, 0))
    vblk = pl.BlockSpec((seq, LANES), lambda b: (b, 1))
    outblk = pl.BlockSpec((None, LANES, LANES), lambda b: (b, 0, 0))
    ws = (lw['cmp_wk'], lw['cmp_wv'], lw['cmp_pek'], lw['cmp_pev'], lw['cmp_w2k'], lw['cmp_w2v'], lw['k_gain'])
    return pl.pallas_call(
        _compress_prompt_kernel,
        grid=(nb,),
        in_specs=[kblk, vblk] + [full(a) for a in ws],
        out_specs=[outblk, outblk],
        out_shape=[jax.ShapeDtypeStruct((nb, LANES, LANES), f32)] * 2,
        compiler_params=_cparams(("parallel",)),
        name="compress_prompt",
    )(ckv, ckv, *ws)


def _rank_select(score, n_blocks):
    blk = lax.broadcasted_iota(jnp.int32, score.shape, 1)
    rank = jnp.zeros(score.shape, f32)
    for m in range(n_blocks):
        sm = score[:, m:m + 1]
        beats = (sm > score) | ((sm == score) & (m < blk))
        rank = rank + jnp.where(beats, 1.0, 0.0)
    return jnp.where(rank < float(N_SEL), 1.0, 0.0)


def _rank_select_rows(score, n_blocks):
    blk = lax.broadcasted_iota(jnp.int32, score.shape, 0)
    rank = jnp.zeros(score.shape, f32)
    for m in range(n_blocks):
        sm = score[m:m + 1, :]
        beats = (sm > score) | ((sm == score) & (m < blk))
        rank = rank + jnp.where(beats, 1.0, 0.0)
    return jnp.where(rank < float(N_SEL), 1.0, 0.0)


def _nsa_prompt_kernel(q_ref, kc_ref, vc_ref, sk_ref, sv_ref, wk_ref, wv_ref, gt_ref,
                       td_ref, ts_ref, we_ref, bc_ref, ovt_ref, eet_ref, o_ref,
                       selx_ref, m_ref, l_ref, acc_ref):
    i = pl.program_id(1)
    lane = lax.broadcasted_iota(jnp.int32, (QT, LANES), 1)
    row = lax.broadcasted_iota(jnp.int32, (QT, LANES), 0)
    n_blk = selx_ref.shape[1] // SEL_BLOCK
    blk = lax.broadcasted_iota(jnp.int32, (n_blk, QT), 0)
    cur = (i * QT + lax.broadcasted_iota(jnp.int32, (n_blk, QT), 1)) >> 6
    gates_t = gt_ref[...].T
    outs = [[None, None] for _ in range(GQA)]

    def tile4(x):
        return jnp.concatenate([x] * GQA, axis=1)

    def scores(q4, k_ref, kt, bias, sel_head=None, live=None):
        off = pl.multiple_of(kt * QT, QT)
        s = _dot_nt(k_ref[pl.ds(off, QT), :].astype(bf16), q4)
        if bias is not None:
            s = s + bias
        if live is not None:
            s = s + jnp.where(live, 0.0, NEG)
        if sel_head is not None:
            s = s + tile4(selx_ref[sel_head, pl.ds(off, QT), :])
        return s, off

    def attend(v_ref, tiles, state=None):
        m = state[0] if state is not None else None
        for s, _ in tiles:
            ms = jnp.max(s, axis=0, keepdims=True)
            m = ms if m is None else jnp.maximum(m, ms)
        p = jnp.concatenate([jnp.exp(s - m) for s, _ in tiles], axis=0)
        v = jnp.concatenate([v_ref[pl.ds(off, QT), :] for _, off in tiles], axis=0)
        l = jnp.sum(p, axis=0, keepdims=True)
        acc = _dot_tn(v.astype(bf16), p.astype(bf16))
        if state is not None:
            alpha = jnp.exp(state[0] - m)
            l = l + alpha * state[1]
            acc = acc + alpha * state[2]
        return m, l, acc

    m_ref[...] = jnp.full(m_ref.shape, NEG, f32)
    l_ref[...] = jnp.zeros(l_ref.shape, f32)
    acc_ref[...] = jnp.zeros(acc_ref.shape, f32)

    q4s, o_cmps = [], []
    for h in range(N_KV):
        half = (lane >= HEAD_DIM) if h else (lane < HEAD_DIM)
        q4 = jnp.concatenate(
            [jnp.where(half, q_ref[:, g * LANES:(g + 1) * LANES] * jnp.asarray(ATT_SCALE, bf16), jnp.zeros((), bf16))
             for g in range(GQA)], axis=0)
        q4s.append(q4)
        bias_c = bc_ref[h, pl.ds(pl.multiple_of(LANES - SUBLANES * i, SUBLANES), LANES), :]
        s = _dot_nt(kc_ref[...].astype(bf16), q4) + bias_c
        e = jnp.exp(s - jnp.max(s, axis=0, keepdims=True))
        p = jnp.where(s > 0.5 * NEG, e / jnp.sum(e, axis=0, keepdims=True), 0.0)
        o_cmps.append(_dot_tn(vc_ref[...].astype(bf16), p.astype(bf16)))
        psum = p[:, 0:QT] + p[:, QT:2 * QT] + p[:, 2 * QT:3 * QT] + p[:, 3 * QT:4 * QT]
        hi = psum.astype(bf16)
        lo = (psum - hi.astype(f32)).astype(bf16)
        imp = (_dot(ovt_ref[...], hi) + _dot(ovt_ref[...], lo))[0:n_blk, :]
        forced = (blk == 0) | (blk == cur) | (blk == cur - 1)
        score = jnp.where(forced, BIG, jnp.where(blk <= cur, imp, -BIG))
        sel = _rank_select_rows(score, n_blk)
        neg = jnp.concatenate([(sel - 1.0) * (-NEG), jnp.zeros((LANES - n_blk, QT), f32)], axis=0).astype(bf16)
        selx_ref[h] = _dot(eet_ref[...], neg)

    n_far = jnp.maximum(i - 1, 0)

    def far_pairs(j, carry):
        kt1 = 2 * j + 1
        tiles = [[scores(q4s[h], sk_ref, 2 * j, None, sel_head=h),
                  scores(q4s[h], sk_ref, jnp.minimum(kt1, i), None, sel_head=h, live=kt1 < n_far)]
                 for h in range(N_KV)]
        for h in range(N_KV):
            m_ref[h], l_ref[h], acc_ref[h] = attend(sv_ref, tiles[h], (m_ref[h], l_ref[h], acc_ref[h]))
        return carry

    lax.fori_loop(0, (n_far + 1) // 2, far_pairs, 0)

    def back_tile(back):
        return dict(kt=jnp.maximum(i - back, 0), live=(i >= back) if back else None)

    slc_tiles = [[scores(q4s[h], sk_ref, bias=tab_ref[h], sel_head=h, **back_tile(back))
                  for back, tab_ref in ((1, ts_ref), (0, td_ref))] for h in range(N_KV)]
    win_tiles = [[scores(q4s[h], wk_ref, bias=bias, **back_tile(back))
                  for back, bias in ((4, we_ref[h]), (3, None), (2, None), (1, ts_ref[h]), (0, td_ref[h]))]
                 for h in range(N_KV)]
    for h in range(N_KV):
        _, l_s, acc_s = attend(sv_ref, slc_tiles[h], (m_ref[h], l_ref[h], acc_ref[h]))
        _, l_w, acc_w = attend(wv_ref, win_tiles[h])
        o_slc = acc_s / l_s
        o_win = acc_w / l_w
        for g in range(GQA):
            c = (h * GQA + g) * 3
            sl = slice(g * QT, (g + 1) * QT)
            outs[g][h] = (gates_t[c:c + 1, :] * o_cmps[h][:, sl] + gates_t[c + 1:c + 2, :] * o_slc[:, sl]
                          + gates_t[c + 2:c + 3, :] * o_win[:, sl])
    for g in range(GQA):
        o_ref[:, g * LANES:(g + 1) * LANES] = jnp.where(row < HEAD_DIM, outs[g][0], outs[g][1]).T.astype(bf16)


def _nsa_prompt(q, kc, vc, skv, wkv, gt, tabs, nb, seq):
    nt = seq // QT
    tile = lambda n: pl.BlockSpec((QT, n), lambda b, i: (b * nt + i, 0))
    kblk = pl.BlockSpec((seq, LANES), lambda b, i: (b, 0))
    vblk = pl.BlockSpec((seq, LANES), lambda b, i: (b, 1))
    cblk = pl.BlockSpec((None, LANES, LANES), lambda b, i: (b, 0, 0))
    full = lambda a: pl.BlockSpec(a.shape, lambda b, i: (0,) * a.ndim)
    consts = (tabs['td'], tabs['ts'], tabs['we'], tabs['bc'], tabs['ovt'], tabs['eet'])
    return pl.pallas_call(
        _nsa_prompt_kernel,
        grid=(nb, nt),
        in_specs=[tile(512), cblk, cblk, kblk, vblk, kblk, vblk, tile(LANES)] + [full(a) for a in consts],
        out_specs=tile(512),
        out_shape=jax.ShapeDtypeStruct((nb * seq, 512), bf16),
        scratch_shapes=[pltpu.VMEM((N_KV, seq, QT), f32), pltpu.VMEM((N_KV, 1, GQA * QT), f32),
                        pltpu.VMEM((N_KV, 1, GQA * QT), f32), pltpu.VMEM((N_KV, LANES, GQA * QT), f32)],
        compiler_params=_cparams(("parallel", "arbitrary")),
        name="nsa_prompt",
    )(q, kc, vc, skv, skv, wkv, wkv, gt, *consts)


N_PAGES = 16
SAMPLE_SEQS_PER_STEP = 2


def _nsa_sample_kernel(n_seq, pt_ref, *refs):
    per_seq = 2 * N_PAGES
    seq_pages = [refs[t * per_seq:(t + 1) * per_seq] for t in range(n_seq)]
    (win_ref, qr_ref, gr_ref, knew_ref, wcat_k, wcat_v, pek_ref, pev_ref, w2k_ref, w2v_ref, kg_ref,
     bc_ref, bs_ref, bw_ref, b0_ref, ov_ref, o_ref, win_o, hist_ref) = refs[n_seq * per_seq:]

    for kind in range(2):
        for t in range(n_seq):
            for j in range(N_PAGES):
                hist_ref[kind, pl.ds((t * N_PAGES + j) * PAGE, PAGE), :] = seq_pages[t][j][kind].T

    def hist(kind):
        return lambda s: hist_ref[kind, pl.ds(s, n_seq * LANES, stride=CMP_STRIDE), :]

    kc_all = _pair_rmsnorm(_compress(hist(0), wcat_k, pek_ref, w2k_ref), kg_ref[0:1, :])
    vc_all = _compress(hist(1), wcat_v, pev_ref, w2v_ref)
    for t in range(n_seq):
        rows = slice(t * LANES, (t + 1) * LANES)
        _nsa_sample_one(t, kc_all[rows], vc_all[rows], seq_pages[t][N_PAGES:], win_ref, qr_ref, gr_ref,
                        knew_ref, bc_ref, bs_ref, bw_ref, b0_ref, ov_ref, o_ref, win_o)


def _nsa_sample_one(t, kc, vc, slc_pages, win_ref, qr_ref, gr_ref, knew_ref, bc_ref, bs_ref, bw_ref, b0_ref,
                    ov_ref, o_ref, win_o):
    lane = lax.broadcasted_iota(jnp.int32, (SUBLANES, LANES), 1)
    qf = qr_ref[t]
    qb = qf.astype(bf16)
    gates = gr_ref[t]
    knew = knew_ref[t]
    hi_rows = lax.broadcasted_iota(jnp.int32, (SUBLANES, LANES), 0) >= GQA
    own = (lane >= HEAD_DIM) == hi_rows

    s = _dot_nt(qb, kc.astype(bf16)) * ATT_SCALE + bc_ref[...]
    mskc = lane < N_CMP
    s = jnp.where(mskc, s, NEG)
    e = jnp.exp(s - jnp.max(s, axis=-1, keepdims=True))
    p = jnp.where(mskc, e / jnp.sum(e, axis=-1, keepdims=True), 0.0)
    o_cmp = _dot(p.astype(bf16), vc.astype(bf16))
    psum = jnp.broadcast_to(jnp.sum(p.reshape(N_KV, GQA, LANES), axis=1, keepdims=True), (N_KV, GQA, LANES))
    imp = _dot_hilo(psum.reshape(SUBLANES, LANES), ov_ref[...])
    n_blk = N_PAGES * PAGE // SEL_BLOCK + 1
    cur = n_blk - 1
    forced = (lane == 0) | (lane == cur) | (lane == cur - 1)
    score = jnp.where(forced, BIG, imp)
    score = jnp.where(lane < n_blk, score, -3e38)
    sel = _rank_select(score, n_blk)

    def softmax_av(s_parts, s_new, vt_parts, v_new):
        mx = s_new
        for sp in s_parts:
            mx = jnp.maximum(mx, jnp.max(sp, axis=-1, keepdims=True))
        p_new = jnp.exp(s_new - mx)
        den = p_new
        acc = p_new * v_new
        for sp, vt in zip(s_parts, vt_parts):
            pp = jnp.exp(sp - mx)
            den = den + jnp.sum(pp, axis=-1, keepdims=True)
            acc = acc + _dot_nt(pp.astype(bf16), vt.astype(bf16))
        return acc / den

    def new_row(r):
        return knew[r:r + 1, :]

    def score_new(krow):
        return jnp.sum(jnp.where(own, qf * krow, 0.0), axis=-1, keepdims=True) * ATT_SCALE + b0_ref[:, 0:1]

    s_parts = []
    for j in range(N_PAGES):
        sj = _dot(qb, slc_pages[j][0].astype(bf16)) * ATT_SCALE + bs_ref[:, j * PAGE:(j + 1) * PAGE]
        mj = jnp.where(lane < SEL_BLOCK, sel[:, 2 * j:2 * j + 1], sel[:, 2 * j + 1:2 * j + 2]) > 0.5
        s_parts.append(jnp.where(mj, sj, NEG))
    o_slc = softmax_av(s_parts, score_new(new_row(0)), [pg[1] for pg in slc_pages], new_row(1))

    w_parts = []
    for j in range(WINDOW // PAGE):
        kw = win_ref[t, 0, :, j * PAGE:(j + 1) * PAGE]
        sj = _dot(qb, kw.astype(bf16)) * ATT_SCALE + bw_ref[:, j * PAGE:(j + 1) * PAGE]
        if j == 0:
            sj = jnp.where(lane >= 1, sj, NEG)
        w_parts.append(sj)
    vt_parts = [win_ref[t, 1, :, j * PAGE:(j + 1) * PAGE] for j in range(WINDOW // PAGE)]
    o_win = softmax_av(w_parts, score_new(new_row(2)), vt_parts, new_row(3))

    o_ref[t] = gates[:, 0:1] * o_cmp + gates[:, 1:2] * o_slc + gates[:, 2:3] * o_win

    new_cols = knew.T
    last = lax.broadcasted_iota(jnp.int32, (LANES, WINDOW), 1) == WINDOW - 1
    for kv in range(2):
        win_o[t, kv] = jnp.where(last, new_cols[:, 2 + kv:3 + kv], pltpu.roll(win_ref[t, kv], WINDOW - 1, 1))


def _nsa_sample(layer, page_table, cache_cmp, cache_slc, win_state, qrows, grows, knew, lw, tabs):
    nb = page_table.shape[0]
    n_seq = SAMPLE_SEQS_PER_STEP

    def page_t(t, j):
        return pl.BlockSpec((None, None, 2, LANES, PAGE), lambda b, pt: (layer, pt[b * n_seq + t, j], 0, 0, 0))

    full = lambda a: pl.BlockSpec(a.shape, lambda b, pt: (0,) * a.ndim)
    per_b = pl.BlockSpec((n_seq, SUBLANES, LANES), lambda b, pt: (b, 0, 0))
    consts = (lw['cmp_wk'], lw['cmp_wv'], lw['cmp_pek'], lw['cmp_pev'], lw['cmp_w2k'], lw['cmp_w2v'], lw['k_gain'],
              tabs['bc_s'], tabs['bs_s'], tabs['bw_s'], tabs['b0_s'], tabs['ov_s'])
    page_specs, page_args = [], []
    for t in range(n_seq):
        page_specs += [page_t(t, j) for j in range(N_PAGES)] * 2
        page_args += [cache_cmp] * N_PAGES + [cache_slc] * N_PAGES
    in_specs = (page_specs
                + [pl.BlockSpec((None, n_seq, 2, LANES, WINDOW), lambda b, pt: (layer, b, 0, 0, 0)), per_b, per_b, per_b]
                + [full(a) for a in consts])
    return pl.pallas_call(
        functools.partial(_nsa_sample_kernel, n_seq),
        grid_spec=pltpu.PrefetchScalarGridSpec(
            num_scalar_prefetch=1, grid=(nb // n_seq,), in_specs=in_specs,
            out_specs=[per_b, pl.BlockSpec((n_seq, 2, LANES, WINDOW), lambda b, pt: (b, 0, 0, 0))],
            scratch_shapes=[pltpu.VMEM((2, n_seq * N_PAGES * PAGE, LANES), f32)]),
        out_shape=[jax.ShapeDtypeStruct((nb, SUBLANES, LANES), f32),
                   jax.ShapeDtypeStruct((nb, 2, LANES, WINDOW), f32)],
        compiler_params=_cparams(("parallel",)),
        name="nsa_sample",
    )(page_table, *page_args, win_state, qrows, grows, knew, *consts)


def _gla_out(o, gn_ref, gr):
    o = jnp.concatenate([_pair_rmsnorm(o[:, :LANES], gn_ref[:, :LANES]),
                         _pair_rmsnorm(o[:, LANES:], gn_ref[:, LANES:])], axis=1)
    return o * jax.nn.silu(gr)


def _gla_prompt_kernel(gq_ref, gk_ref, gl_ref, gv_ref, gr_ref, gn_ref, o_ref, st_o, st_ref):
    c_rows = GLA_CHUNK
    lane = lax.broadcasted_iota(jnp.int32, (c_rows, LANES), 1)
    row = lax.broadcasted_iota(jnp.int32, (c_rows, LANES), 0)
    tril = (lax.broadcasted_iota(jnp.int32, (c_rows, c_rows), 0)
            >= lax.broadcasted_iota(jnp.int32, (c_rows, c_rows), 1))
    lane_v = lax.broadcasted_iota(jnp.int32, (c_rows, GLA_HEADS * GLA_DV), 1)
    srow = lax.broadcasted_iota(jnp.int32, (GLA_HEADS * GLA_DV, LANES), 0)
    scol = lax.broadcasted_iota(jnp.int32, (GLA_HEADS * GLA_DV, LANES), 1)
    diag = (srow >> 6) == (scol >> 5)
    st_ref[...] = jnp.zeros(st_ref.shape, f32)
    n_seqs = st_ref.shape[0]
    seq = gq_ref.shape[0] // n_seqs

    def body(c, carry):
        for sq in range(n_seqs):
            chunk(sq, pl.multiple_of(sq * seq + c * c_rows, c_rows))
        return carry

    def chunk(sq, off):
        q = gq_ref[pl.ds(off, c_rows), :] * (GLA_DK ** -0.5)
        k = gk_ref[pl.ds(off, c_rows), :]
        v = gv_ref[pl.ds(off, c_rows), :]
        b = gl_ref[pl.ds(off, c_rows), :]
        sh = 1
        while sh < c_rows:
            b = b + jnp.where(row >= sh, pltpu.roll(b, sh, 0), 0.0)
            sh *= 2
        qd = q * jnp.exp(b)
        kd = (k * jnp.exp(-b)).astype(bf16)
        bl = b[c_rows - 1:c_rows, :]
        kk = (k * jnp.exp(bl - b)).astype(bf16)
        st = st_ref[sq]
        vb = v.astype(bf16)
        o = _dot_nt(qd.astype(bf16), st.astype(bf16))
        for hh in range(GLA_HEADS):
            qm = jnp.where((lane >> 5) == hh, qd, 0.0).astype(bf16)
            a = jnp.where(tril, _dot_nt(qm, kd), 0.0)
            o = o + jnp.where((lane_v >> 6) == hh, _dot(a.astype(bf16), vb), 0.0)
        st_ref[sq] = jnp.exp(bl) * st + jnp.where(diag, _dot_tn(vb, kk), 0.0)
        o_ref[pl.ds(off, c_rows), :] = _gla_out(o, gn_ref, gr_ref[pl.ds(off, c_rows), :]).astype(bf16)

    lax.fori_loop(0, seq // c_rows, body, 0)
    st_o[...] = st_ref[...]


GLA_SEQS_PER_STEP = 2


def _gla_prompt(gq, gk, gl, gv, gr, gn, nb, seq):
    ns = GLA_SEQS_PER_STEP
    blk = lambda n: pl.BlockSpec((ns * seq, n), lambda b: (b, 0))
    return pl.pallas_call(
        _gla_prompt_kernel,
        grid=(nb // ns,),
        in_specs=[blk(LANES), blk(LANES), blk(LANES), blk(2 * LANES), blk(2 * LANES),
                  pl.BlockSpec(gn.shape, lambda b: (0, 0))],
        out_specs=[blk(2 * LANES), pl.BlockSpec((ns, GLA_HEADS * GLA_DV, LANES), lambda b: (b, 0, 0))],
        out_shape=[jax.ShapeDtypeStruct((nb * seq, 2 * LANES), bf16),
                   jax.ShapeDtypeStruct((nb, GLA_HEADS * GLA_DV, LANES), f32)],
        scratch_shapes=[pltpu.VMEM((ns, GLA_HEADS * GLA_DV, LANES), f32)],
        compiler_params=_cparams(("parallel",)),
        name="gla_prompt",
    )(gq, gk, gl, gv, gr, gn)


def _gla_sample_kernel(cols_ref, v_ref, s0_ref, gr_ref, gn_ref, o_ref, s_o):
    nb = s0_ref.shape[0]
    c = cols_ref[...]
    q = c[:, :, 0:1] * (GLA_DK ** -0.5)
    k = c[:, :, 1:2]
    g = c[:, :, 2:3]
    eg = jnp.exp(g)
    qd = q * eg
    kd = k * jnp.exp(-g)
    s0 = s0_ref[...]
    v = v_ref[...]
    a = jnp.sum((qd * kd).reshape(nb, GLA_HEADS, GLA_DK, 1), axis=2)
    o = jnp.sum((qd * s0).reshape(nb, GLA_HEADS, GLA_DK, GLA_DV), axis=2) + a * v
    vexp = jnp.broadcast_to(v[:, :, None, :], (nb, GLA_HEADS, GLA_DK, GLA_DV)).reshape(nb, GLA_HEADS * GLA_DK, GLA_DV)
    s_o[...] = eg * s0 + k * vexp
    y = o * lax.rsqrt(jnp.mean(o * o, axis=-1, keepdims=True) + EPS) * gn_ref[...]
    o_ref[...] = y * jax.nn.silu(gr_ref[...])


def _gla_sample(cols, v, s0, gr, gn, bt=8):
    nb = s0.shape[0]
    blk = lambda a: pl.BlockSpec((bt,) + a.shape[1:], lambda i: (i,) + (0,) * (a.ndim - 1))
    return pl.pallas_call(
        _gla_sample_kernel,
        grid=(nb // bt,),
        in_specs=[blk(cols), blk(v), blk(s0), blk(gr), pl.BlockSpec(gn.shape, lambda i: (0, 0))],
        out_specs=[blk(v), blk(s0)],
        out_shape=[jax.ShapeDtypeStruct(v.shape, f32), jax.ShapeDtypeStruct(s0.shape, f32)],
        compiler_params=_cparams(("parallel",)),
        name="gla_sample",
    )(cols, v, s0, gr, gn)


def _pool_finish(sums, u, cnt_inv, w_ref, sc_ref):
    lane = lax.broadcasted_iota(jnp.int32, u.shape, 1)
    grp = lane >> 6
    s = jnp.where(grp == 0, sums[0], jnp.where(grp == 1, sums[1], jnp.where(grp == 2, sums[2], sums[3])))
    dlt = s * cnt_inv - u
    return _dot(dlt.astype(bf16), w_ref[...]) * sc_ref[...]


def _pool_prompt_kernel(u_ref, w_ref, sc_ref, o_ref):
    u = u_ref[...]
    row = lax.broadcasted_iota(jnp.int32, u.shape, 0)
    lane = lax.broadcasted_iota(jnp.int32, u.shape, 1)
    sums = []
    s = u
    sh = 1
    while sh < POOL_WINDOWS[-1]:
        s = s + jnp.where(row >= sh, pltpu.roll(s, sh, 0), 0.0)
        sums.append(s)
        sh *= 2
    grp = lane >> 6
    win = jnp.where(grp == 0, POOL_WINDOWS[0], jnp.where(grp == 1, POOL_WINDOWS[1],
                    jnp.where(grp == 2, POOL_WINDOWS[2], POOL_WINDOWS[3])))
    cnt = jnp.minimum(win, row + 1).astype(f32)
    o_ref[...] = _pool_finish(sums, u, 1.0 / cnt, w_ref, sc_ref).astype(bf16)


def _pool_prompt(pu, w, sc, nb, seq):
    blk = pl.BlockSpec((seq, 2 * LANES), lambda b: (b, 0))
    full = lambda a: pl.BlockSpec(a.shape, lambda b: (0,) * a.ndim)
    return pl.pallas_call(
        _pool_prompt_kernel,
        grid=(nb,),
        in_specs=[blk, full(w), full(sc)],
        out_specs=blk,
        out_shape=jax.ShapeDtypeStruct((nb * seq, 2 * LANES), bf16),
        compiler_params=_cparams(("parallel",)),
        name="pool_prompt",
    )(pu, w, sc)


def _pool_sample_kernel(u_ref, buf_ref, w_ref, sc_ref, o_ref):
    u = u_ref[...]
    lane = lax.broadcasted_iota(jnp.int32, u.shape, 1)
    sums = []
    s = u
    nxt = POOL_BUF - 1
    for win in POOL_WINDOWS:
        while POOL_BUF - nxt < win:
            s = s + buf_ref[nxt]
            nxt -= 1
        sums.append(s)
    grp = lane >> 6
    cnt_inv = jnp.where(grp == 0, 1.0 / POOL_WINDOWS[0], jnp.where(grp == 1, 1.0 / POOL_WINDOWS[1],
                        jnp.where(grp == 2, 1.0 / POOL_WINDOWS[2], 1.0 / POOL_WINDOWS[3])))
    o_ref[...] = _pool_finish(sums, u, cnt_inv, w_ref, sc_ref).astype(bf16)


def _pool_sample(pu, buf_t, w, sc):
    full = lambda a: pl.BlockSpec(a.shape, lambda i: (0,) * a.ndim)
    return pl.pallas_call(
        _pool_sample_kernel,
        grid=(1,),
        in_specs=[full(pu), full(buf_t), full(w), full(sc)],
        out_specs=full(pu),
        out_shape=jax.ShapeDtypeStruct(pu.shape, bf16),
        compiler_params=_cparams(("arbitrary",)),
        name="pool_sample",
    )(pu, buf_t, w, sc)


def _out_proj_kernel(a_ref, b_ref, c_ref, x_ref, w_ref, g_ref, x_o, h_o):
    y = (_dot(a_ref[...], w_ref[0:512, :]) + _dot(b_ref[...], w_ref[512:768, :])
         + _dot(c_ref[...], w_ref[768:1024, :]))
    x = x_ref[...] + y
    x_o[...] = x
    h_o[...] = (x * lax.rsqrt(jnp.mean(x * x, axis=-1, keepdims=True) + EPS) * g_ref[...]).astype(bf16)


def _out_proj(o_nsa, o_gla, o_pool, x, w, g, tm):
    m = x.shape[0]
    row = lambda n: pl.BlockSpec((tm, n), lambda i: (i, 0))
    full = lambda a: pl.BlockSpec(a.shape, lambda i: (0,) * a.ndim)
    return pl.pallas_call(
        _out_proj_kernel,
        grid=(m // tm,),
        in_specs=[row(512), row(256), row(256), row(D_MODEL), full(w), full(g)],
        out_specs=[row(D_MODEL), row(D_MODEL)],
        out_shape=[jax.ShapeDtypeStruct((m, D_MODEL), f32), jax.ShapeDtypeStruct((m, D_MODEL), bf16)],
        compiler_params=_cparams(("parallel",)),
        name="out_proj",
    )(o_nsa, o_gla, o_pool, x, w, g)


FFN_ROW_PARTS = 4
HALO = 16


def _ffn_kernel(tiles_per_seq, decode, h_ref, p_ref, x_ref, wa_ref, wu_ref, cw_ref, cb_ref, wd_ref, x_o, a_o):
    hn = h_ref[...]
    tm = hn.shape[0]
    if decode:
        a = _dot(hn, wa_ref[...])
        a2 = p_ref[0]
        a1 = p_ref[1]
        a_o[...] = a
        n_parts = 1
    else:
        first = (pl.program_id(0) % tiles_per_seq) == 0
        ax = _dot(jnp.concatenate([hn, p_ref[...]], axis=0), wa_ref[...])
        a = ax[:tm]
        ah = jnp.where(first, 0.0, ax[tm:])
        p1 = ah[HALO - 1:HALO, :]
        p2 = ah[HALO - 2:HALO - 1, :]
        row = lax.broadcasted_iota(jnp.int32, a.shape, 0)
        a1 = jnp.where(row == 0, p1, pltpu.roll(a, 1, 0))
        a2 = jnp.where(row == 0, p2, jnp.where(row == 1, p1, pltpu.roll(a, 2, 0)))
        a_o[...] = a[tm - SUBLANES:, :]
        n_parts = FFN_ROW_PARTS
    for part in range(n_parts):
        rs = slice(part * tm // n_parts, (part + 1) * tm // n_parts)
        u = _dot(hn[rs], wu_ref[...])
        ac = cb_ref[...] + a2[rs] * cw_ref[0:1, :] + a1[rs] * cw_ref[1:2, :] + a[rs] * cw_ref[2:3, :]
        act = (jax.nn.silu(ac) * u).astype(bf16)
        x_o[rs, :] = x_ref[rs, :] + _dot(act, wd_ref[...])


def _ffn(hn, prev, x, lw, tm, tiles_per_seq, decode):
    m = x.shape[0]
    nt = m // tm
    row = lambda n: pl.BlockSpec((tm, n), lambda i: (i, 0))
    const = lambda a: pl.BlockSpec(a.shape, lambda i: (0,) * a.ndim, pipeline_mode=pl.Buffered(1))
    if decode:
        prev_spec = pl.BlockSpec(prev.shape, lambda i: (0, 0, 0))
        a_spec = row(D_FF)
        a_shape = jax.ShapeDtypeStruct((m, D_FF), f32)
    else:
        per = tm // HALO
        prev_spec = pl.BlockSpec((HALO, D_MODEL), lambda i: (jnp.maximum(i * per - 1, 0), 0))
        a_spec = pl.BlockSpec((None, SUBLANES, D_FF), lambda i: (i, 0, 0))
        a_shape = jax.ShapeDtypeStruct((nt, SUBLANES, D_FF), f32)
    ws = (lw['w_up_a'], lw['w_up_u'], lw['conv_w'], lw['conv_b'], lw['w_down'])
    return pl.pallas_call(
        functools.partial(_ffn_kernel, tiles_per_seq, decode),
        grid=(nt,),
        in_specs=[row(D_MODEL), prev_spec, row(D_MODEL)] + [const(a) for a in ws],
        out_specs=[row(D_MODEL), a_spec],
        out_shape=[jax.ShapeDtypeStruct((m, D_MODEL), f32), a_shape],
        compiler_params=_cparams(("parallel",)),
        name="ffn_decode" if decode else "ffn_prompt",
    )(hn, prev, x, *ws)


def _rel_bucket_np(d):
    d = np.maximum(d, 0)
    exact = REL_BUCKETS // 2
    lg = np.log(np.maximum(d, 1).astype(np.float32) / np.float32(exact)) / np.float32(math.log(REL_MAX_DIST / exact))
    large = np.minimum(exact + (lg * np.float32(REL_BUCKETS - exact)).astype(np.int32), REL_BUCKETS - 1)
    return np.where(d < exact, d, large).astype(np.int32)


def _bias_lookup(rel_bias, dist):
    bucket = _rel_bucket_np(np.asarray(dist)).reshape(-1)
    onehot = jnp.asarray((bucket[:, None] == np.arange(REL_BUCKETS)[None, :]).astype(np.int8)).astype(f32)
    out = jnp.dot(onehot, rel_bias, precision=lax.Precision.HIGHEST)
    return out.reshape(tuple(np.shape(dist)) + (N_HEADS,))


def _bias_tables(rel_bias, seq, past_len):
    c = np.arange(QT)[:, None]
    r = np.arange(QT)[None, :]

    def key_major(t):
        rows = t.shape[0]
        return t.reshape(rows, QT, N_KV, GQA).transpose(2, 0, 3, 1).reshape(N_KV, rows, GQA * QT)

    def masked(t, keep):
        return jnp.where(jnp.asarray(np.tile(keep, (1, GQA)))[None], t, NEG)

    far = jnp.broadcast_to(rel_bias[REL_BUCKETS - 1].reshape(N_KV, 1, GQA, 1), (N_KV, QT, GQA, QT))
    far = far.reshape(N_KV, QT, GQA * QT)
    td = masked(key_major(_bias_lookup(rel_bias, r - c)) - far, c <= r)
    ts = key_major(_bias_lookup(rel_bias, QT + r - c)) - far
    we = masked(jnp.zeros_like(far), c > r)
    j = np.arange(2 * QT)[:, None]
    dist_c = r - (CMP_STRIDE * (j - QT) + CMP_LEN - 1)
    bc = masked(key_major(_bias_lookup(rel_bias, dist_c)), dist_c >= 0)
    cc = np.arange(LANES)[:, None]
    n = np.arange(LANES)[None, :]
    ov = ((cc * CMP_STRIDE < n * SEL_BLOCK + SEL_BLOCK) & (cc * CMP_STRIDE + CMP_LEN > n * SEL_BLOCK) & (cc < N_CMP))
    ovt = jnp.asarray((ov & (n < seq // SEL_BLOCK)).T, dtype=bf16)
    ov_s = jnp.asarray(ov & (n < past_len // SEL_BLOCK + 1), dtype=bf16)
    eet = jnp.asarray((np.arange(seq)[:, None] // SEL_BLOCK) == np.arange(LANES)[None, :], dtype=bf16)

    def look_rows(dist):
        return _bias_lookup(rel_bias, dist).T

    bc_s = look_rows(past_len - (np.arange(LANES) * CMP_STRIDE + CMP_LEN - 1))
    bs_s = look_rows(past_len - np.arange(past_len))
    bw_s = look_rows(WINDOW - np.arange(WINDOW))
    b0_s = look_rows(np.zeros((LANES,), np.int64))
    return dict(td=td, ts=ts, we=we, bc=bc, ovt=ovt, eet=eet,
                ov_s=ov_s, bc_s=bc_s, bs_s=bs_s, bw_s=bw_s, b0_s=b0_s)


def _pair_order(w, axis):
    shape = w.shape
    w = w.reshape(shape[:axis] + (N_KV, GQA, HEAD_DIM) + shape[axis + 1:])
    return jnp.swapaxes(w, axis, axis + 1).reshape(shape)


def _repack_w_in(w):
    z = lambda n: jnp.zeros((w.shape[0], n), w.dtype)
    return jnp.concatenate([
        _pair_order(w[:, :512], 1), w[:, 512:1280],
        w[:, 1280:1304], z(LANES - 24),
        w[:, 1304:2072],
        w[:, 2072:2088], z(LANES - GLA_RANK),
        w[:, 2088:2344]], axis=1)


def _block_diag(blocks):
    n, r, c = blocks.shape
    eye = jnp.eye(n, dtype=blocks.dtype)
    return (eye[:, None, :, None] * blocks[:, :, None, :]).reshape(n * r, n * c)


def _layer_weights(l, p):
    tile2 = lambda v: jnp.tile(v, 2)[None, :]
    w_in = _repack_w_in(p['w_in'][l]).astype(bf16)
    w_out = jnp.concatenate([_pair_order(p['w_out'][l][:512], 0), p['w_out'][l][512:]], axis=0).astype(bf16)
    k_gain = jnp.concatenate([jnp.tile(p['k_gain'][l], (1, 2)), jnp.zeros((SUBLANES - 3, LANES), f32)], axis=0)
    w_alpha = jnp.concatenate([p['gla_w_alpha'][l], jnp.zeros((LANES - GLA_RANK, LANES), f32)], axis=0).astype(bf16)

    def cmp_w(kv):
        w1 = p['cmp_w1'][l, kv]
        two = jax.vmap(lambda w: _block_diag(jnp.stack([w, w])))(w1)
        wcat = jnp.concatenate([two[:CMP_STRIDE], two[CMP_STRIDE:]], axis=-1).astype(bf16)
        pe = jnp.tile(p['cmp_pe'][l, kv], (1, 2))
        pe = jnp.concatenate([jnp.broadcast_to(pe[:CMP_STRIDE, None, :], (CMP_STRIDE, SUBLANES, LANES)),
                              jnp.broadcast_to(pe[CMP_STRIDE:, None, :], (CMP_STRIDE, SUBLANES, LANES))], axis=1)
        w2 = _block_diag(jnp.stack([p['cmp_w2'][l, kv]] * 2)).astype(bf16)
        wcat = wcat.reshape(CMP_STRIDE // 2, 2 * LANES, 4 * LANES)
        pe = pe.reshape(CMP_STRIDE // 2, 2, 2 * SUBLANES, LANES).transpose(0, 2, 1, 3)
        pe = pe.reshape(CMP_STRIDE // 2, 2 * SUBLANES, 2 * LANES)
        return wcat, pe, w2

    wk, pek, w2k = cmp_w(0)
    wv, pev, w2v = cmp_w(1)
    w_up = p['w_ffn_up'][l].astype(bf16)
    conv_w = jnp.concatenate([p['ffn_conv_w'][l], jnp.zeros((SUBLANES - 3, D_FF), f32)], axis=0)
    return dict(
        g_mix=p['g_mix'][l][None, :], w_in=w_in, q_gain=tile2(p['q_gain'][l]), k_gain=k_gain,
        w_alpha=w_alpha, b_alpha=p['gla_b_alpha'][l][None, :],
        cmp_wk=wk, cmp_wv=wv, cmp_pek=pek, cmp_pev=pev, cmp_w2k=w2k, cmp_w2v=w2v,
        gla_norm=p['gla_norm'][l][None, :],
        pool_w=_block_diag(p['pool_w'][l]).astype(bf16), pool_scale=p['pool_scale'][l][None, :],
        w_out=w_out,
        g_ffn=p['g_ffn'][l][None, :],
        w_up_a=w_up[:, :D_FF], w_up_u=w_up[:, D_FF:], conv_w=conv_w, conv_b=p['ffn_conv_b'][l][None, :],
        w_down=p['w_ffn_down'][l].astype(bf16),
    )


def _prompt_layer(x, lw, tabs, nb, seq):
    tm = 512
    (q, ckv, skv, wkv, gt, gq, gk, gv, gr, gl, pu) = _in_proj(x, lw, tm)
    kc, vc = _compress_prompt(ckv, lw, nb, seq)
    o_nsa = _nsa_prompt(q, kc, vc, skv, wkv, gt, tabs, nb, seq)
    o_gla, st = _gla_prompt(gq, gk, gl, gv, gr, lw['gla_norm'], nb, seq)
    o_pool = _pool_prompt(pu, lw['pool_w'], lw['pool_scale'], nb, seq)
    x_mid, hn = _out_proj(o_nsa, o_gla, o_pool, x, lw['w_out'], lw['g_ffn'], tm)
    x_out, a_tail = _ffn(hn, hn, x_mid, lw, tm, seq // tm, False)
    win_tail = wkv.reshape(nb, seq, 2 * LANES)[:, seq - WINDOW:]
    st = st.reshape(nb, GLA_HEADS, GLA_DV, GLA_HEADS, GLA_DK)
    gla_state = jnp.stack([st[:, h, :, h, :] for h in range(GLA_HEADS)], axis=1).transpose(0, 1, 3, 2)
    pool_state = pu.reshape(nb, seq, 2 * LANES)[:, seq - POOL_BUF:]
    a_tail = a_tail.reshape(nb, seq // tm, SUBLANES, D_FF)
    conv_state = a_tail[:, -1, SUBLANES - 2:, :]
    return x_out, (ckv, skv, win_tail, gla_state, pool_state, conv_state)


def _sample_layer(l, x, lw, tabs, page_table, cache_cmp, cache_slc, win_all, state_gla, state_pool, state_conv):
    nb = x.shape[0]
    (q, ckv, skv, wkv, gt, gq, gk, gv, gr, gl, pu) = _in_proj(x, lw, nb)
    lane_hi = (np.arange(LANES) >= HEAD_DIM)
    own = jnp.asarray((lane_hi[None, :] == (np.arange(N_KV)[:, None] == 1)).astype(np.float32))
    qrows = (q.astype(f32).reshape(nb, 1, GQA, LANES) * own[None, :, None, :]).reshape(nb, N_HEADS, LANES)
    grows = jnp.pad(gt[:, :N_HEADS * 3].reshape(nb, N_HEADS, 3), ((0, 0), (0, 0), (0, LANES - 3)))
    knew = jnp.pad(jnp.concatenate([skv, wkv], axis=1).reshape(nb, 4, LANES), ((0, 0), (0, SUBLANES - 4), (0, 0)))
    o, win_next = _nsa_sample(l, page_table, cache_cmp, cache_slc, win_all, qrows, grows, knew, lw, tabs)
    o = o.reshape(nb, N_KV, GQA, N_KV, HEAD_DIM)
    o_nsa = jnp.stack([o[:, h, :, h, :] for h in range(N_KV)], axis=2).reshape(nb, GQA * LANES).astype(bf16)
    cols = jnp.stack([gq, gk, gl], axis=-1)
    o_gla, s_new = _gla_sample(cols, gv.reshape(nb, GLA_HEADS, GLA_DV),
                               state_gla[l].reshape(nb, GLA_HEADS * GLA_DK, GLA_DV),
                               gr.reshape(nb, GLA_HEADS, GLA_DV), lw['gla_norm'].reshape(GLA_HEADS, GLA_DV))
    o_gla = o_gla.reshape(nb, GLA_HEADS * GLA_DV).astype(bf16)
    o_pool = _pool_sample(pu, state_pool[l].transpose(1, 0, 2), lw['pool_w'], lw['pool_scale'])
    x_mid, hn = _out_proj(o_nsa, o_gla, o_pool, x, lw['w_out'], lw['g_ffn'], nb)
    x_out, a = _ffn(hn, state_conv[l].transpose(1, 0, 2), x_mid, lw, nb, 1, True)
    gla_state = s_new.reshape(nb, GLA_HEADS, GLA_DK, GLA_DV)
    return x_out, (ckv, skv, win_next, gla_state, pu, a)


def kernel(x_prompt, x_sample, cache_cmp_kv, cache_slc_kv, page_table, state_win_kv, state_gla, state_pool,
           state_ffn_conv, rel_bias, g_mix, w_in, q_gain, k_gain, cmp_pe, cmp_w1, cmp_w2, gla_w_alpha, gla_b_alpha,
           gla_norm, pool_w, pool_scale, w_out, g_ffn, w_ffn_up, ffn_conv_w, ffn_conv_b, w_ffn_down):
    nb, seq, _ = x_prompt.shape
    db = x_sample.shape[0]
    depth = w_in.shape[0]
    n_phys = cache_cmp_kv.shape[1]
    past_len = page_table.shape[1] * cache_cmp_kv.shape[2]
    assert (seq, past_len, page_table.shape[1], cache_cmp_kv.shape[2]) == (2048, 2048, N_PAGES, PAGE)
    assert state_win_kv.shape[2] == WINDOW and x_sample.shape[1] == 1
    params = dict(g_mix=g_mix, w_in=w_in, q_gain=q_gain, k_gain=k_gain, cmp_pe=cmp_pe, cmp_w1=cmp_w1, cmp_w2=cmp_w2,
                  gla_w_alpha=gla_w_alpha, gla_b_alpha=gla_b_alpha, gla_norm=gla_norm, pool_w=pool_w,
                  pool_scale=pool_scale, w_out=w_out, g_ffn=g_ffn, w_ffn_up=w_ffn_up, ffn_conv_w=ffn_conv_w,
                  ffn_conv_b=ffn_conv_b, w_ffn_down=w_ffn_down)
    tabs = _bias_tables(rel_bias, seq, past_len)
    cache_cmp = cache_cmp_kv.transpose(0, 1, 3, 4, 5, 2).reshape(depth, n_phys, 2, LANES, PAGE)
    cache_slc = cache_slc_kv.transpose(0, 1, 3, 4, 5, 2).reshape(depth, n_phys, 2, LANES, PAGE)
    win_all = state_win_kv.transpose(0, 1, 3, 4, 5, 2).reshape(depth, db, 2, LANES, WINDOW)
    xp = x_prompt.reshape(nb * seq, D_MODEL)
    xs = x_sample.reshape(db, D_MODEL)
    st_p = [[] for _ in range(6)]
    st_s = [[] for _ in range(6)]
    for l in range(depth):
        lw = _layer_weights(l, params)
        xp, new_p = _prompt_layer(xp, lw, tabs, nb, seq)
        xs, new_s = _sample_layer(l, xs, lw, tabs, page_table, cache_cmp, cache_slc, win_all, state_gla,
                                  state_pool, state_ffn_conv)
        for i in range(6):
            st_p[i].append(new_p[i])
            st_s[i].append(new_s[i])
    stk = lambda a: jnp.stack(a, axis=0)
    kv = lambda a, rows: a.reshape(depth, a.shape[1] // rows, rows, 2, N_KV, HEAD_DIM)
    outs_p = (kv(stk(st_p[0]), seq), kv(stk(st_p[1]), seq), kv(stk(st_p[2]).reshape(depth, nb * WINDOW, -1), WINDOW),
              stk(st_p[3]), stk(st_p[4]), stk(st_p[5]))
    shift_in = lambda old, new: jnp.concatenate([old[:, :, 1:], stk(new)[:, :, None, :]], axis=2)
    win_s = stk(st_s[2]).reshape(depth, db, 2, N_KV, HEAD_DIM, WINDOW).transpose(0, 1, 5, 2, 3, 4)
    outs_s = (kv(stk(st_s[0]), 1), kv(stk(st_s[1]), 1), win_s, stk(st_s[3]),
              shift_in(state_pool, st_s[4]), shift_in(state_ffn_conv, st_s[5]))
    return (xp.reshape(nb, seq, D_MODEL), xs.reshape(db, 1, D_MODEL), *outs_p, *outs_s)
```

```python
import functools
import math

import numpy as np
import jax
import jax.numpy as jnp
from jax import lax
from jax.experimental import pallas as pl
from jax.experimental.pallas import tpu as pltpu

f32 = jnp.float32
bf16 = jnp.bfloat16

D_MODEL = 1024
HEAD_DIM = 64
N_KV = 2
GQA = 4
N_HEADS = 8
CMP_STRIDE = 16
CMP_LEN = 32
N_CMP = 127
SEL_BLOCK = 64
N_SEL = 16
WINDOW = 512
ATT_SCALE = HEAD_DIM ** -0.5
REL_BUCKETS = 32
REL_MAX_DIST = 128
GLA_HEADS = 4
GLA_DK = 32
GLA_DV = 64
GLA_RANK = 16
GLA_TAU = 16.0
GLA_CHUNK = 64
POOL_WINDOWS = (2, 4, 8, 16)
POOL_BUF = 15
D_FF = 2816
EPS = 1e-6
NEG = -1e30
BIG = 1e9
PAGE = 128

LANES = 128
SUBLANES = 8
QT = 128
PROJ_W = 2560
VMEM_LIMIT = 56 * 1024 * 1024


def _cparams(sem):
    return pltpu.CompilerParams(dimension_semantics=sem, vmem_limit_bytes=VMEM_LIMIT)


def _dot(a, b):
    return jnp.dot(a, b, preferred_element_type=f32)


def _dot_nt(a, b):
    return lax.dot_general(a, b, (((1,), (1,)), ((), ())), preferred_element_type=f32)


def _dot_tn(a, b):
    return lax.dot_general(a, b, (((0,), (0,)), ((), ())), preferred_element_type=f32)


def _dot_hilo(a, b_bf16):
    hi = a.astype(bf16)
    lo = (a - hi.astype(f32)).astype(bf16)
    return _dot(hi, b_bf16) + _dot(lo, b_bf16)


def _pair_rmsnorm(y, gain):
    lane = lax.broadcasted_iota(jnp.int32, y.shape, y.ndim - 1)
    lo = lane < HEAD_DIM
    y2 = y * y
    s_lo = jnp.sum(jnp.where(lo, y2, 0.0), axis=-1, keepdims=True)
    s_hi = jnp.sum(jnp.where(lo, 0.0, y2), axis=-1, keepdims=True)
    ms = jnp.where(lo, s_lo, s_hi) * (1.0 / HEAD_DIM)
    return y * lax.rsqrt(ms + EPS) * gain


_C_Q, _C_CK, _C_CV, _C_SK, _C_SV, _C_WK, _C_WV = 0, 512, 640, 768, 896, 1024, 1152
_C_GATE, _C_GQ, _C_GK, _C_GV, _C_GR, _C_GLR, _C_PU = 1280, 1408, 1536, 1664, 1920, 2176, 2304


def _proj_kernel(x_ref, g_ref, w_ref, qg_ref, kg_ref, wa_ref, ba_ref,
                 q_o, ckv_o, skv_o, wkv_o, gt_o, gq_o, gk_o, gv_o, gr_o, gl_o, pu_o):
    x = x_ref[...]
    hn = (x * lax.rsqrt(jnp.mean(x * x, axis=-1, keepdims=True) + EPS) * g_ref[...]).astype(bf16)

    def proj(c0):
        y = _dot(hn, w_ref[:, c0:c0 + 2 * LANES])
        return y[:, :LANES], y[:, LANES:]

    for j in range(2):
        for half, y in enumerate(proj(_C_Q + 2 * j * LANES)):
            c = (2 * j + half) * LANES
            q_o[:, c:c + LANES] = _pair_rmsnorm(y, qg_ref[...]).astype(bf16)
    ckv_o[:, :LANES], ckv_o[:, LANES:] = proj(_C_CK)
    k, skv_o[:, LANES:] = proj(_C_SK)
    skv_o[:, :LANES] = _pair_rmsnorm(k, kg_ref[1:2, :])
    k, wkv_o[:, LANES:] = proj(_C_WK)
    wkv_o[:, :LANES] = _pair_rmsnorm(k, kg_ref[2:3, :])
    gate, gq_o[...] = proj(_C_GATE)
    gt_o[...] = jax.nn.sigmoid(gate)
    gk_o[...], gv_o[:, :LANES] = proj(_C_GK)
    gv_o[:, LANES:], gr_o[:, :LANES] = proj(_C_GV + LANES)
    gr_o[:, LANES:], glr = proj(_C_GR + LANES)
    z = _dot(glr.astype(bf16), wa_ref[...]) + ba_ref[...]
    gl_o[...] = jax.nn.log_sigmoid(z) * (1.0 / GLA_TAU)
    pu_o[:, :LANES], pu_o[:, LANES:] = proj(_C_PU)


def _in_proj(x, lw, tm):
    m = x.shape[0]
    row = lambda n: pl.BlockSpec((tm, n), lambda i: (i, 0))
    full = lambda a: pl.BlockSpec(a.shape, lambda i: (0,) * a.ndim)
    widths = (512, 256, 256, 256, 128, 128, 128, 256, 256, 128, 256)
    dtypes = (bf16,) + (f32,) * 10
    args = (x, lw['g_mix'], lw['w_in'], lw['q_gain'], lw['k_gain'], lw['w_alpha'], lw['b_alpha'])
    return pl.pallas_call(
        _proj_kernel,
        grid=(m // tm,),
        in_specs=[row(D_MODEL)] + [full(a) for a in args[1:]],
        out_specs=[row(n) for n in widths],
        out_shape=[jax.ShapeDtypeStruct((m, n), dt) for n, dt in zip(widths, dtypes)],
        compiler_params=_cparams(("parallel",)),
        name="in_proj",
    )(*args)


def _compress(get_x, w_ref, pe_ref, w2_ref):
    xs = jnp.concatenate([get_x(s).astype(bf16) for s in range(CMP_STRIDE)], axis=1)
    xs = jnp.concatenate([xs, pe_ref[...].astype(bf16)], axis=0)
    rows = xs.shape[0] - 2 * SUBLANES
    acc = _dot(xs, w_ref[...])
    a0 = acc[:rows, :2 * LANES]
    a1 = acc[:rows, 2 * LANES:]
    c0 = acc[rows:rows + 1, :2 * LANES] + acc[rows + SUBLANES:rows + SUBLANES + 1, 2 * LANES:]
    h = a0 + pltpu.roll(a1, rows - 1, 0) + c0
    return _dot(jax.nn.gelu(h).astype(bf16), w2_ref[...])


def _compress_prompt_kernel(ck_ref, cv_ref, wk_ref, wv_ref, pek_ref, pev_ref, w2k_ref, w2v_ref, kg_ref, kc_o, vc_o):
    kx = lambda s: ck_ref[pl.ds(s, LANES, stride=CMP_STRIDE), :]
    vx = lambda s: cv_ref[pl.ds(s, LANES, stride=CMP_STRIDE), :]
    kc_o[...] = _pair_rmsnorm(_compress(kx, wk_ref, pek_ref, w2k_ref), kg_ref[0:1, :])
    vc_o[...] = _compress(vx, wv_ref, pev_ref, w2v_ref)


def _compress_prompt(ckv, lw, nb, seq):
    full = lambda a: pl.BlockSpec(a.shape, lambda b: (0,) * a.ndim)
    kblk = pl.BlockSpec((seq, LANES), lambda b: (b, 0))
    vblk = pl.BlockSpec((seq, LANES), lambda b: (b, 1))
    outblk = pl.BlockSpec((None, LANES, LANES), lambda b: (b, 0, 0))
    ws = (lw['cmp_wk'], lw['cmp_wv'], lw['cmp_pek'], lw['cmp_pev'], lw['cmp_w2k'], lw['cmp_w2v'], lw['k_gain'])
    return pl.pallas_call(
        _compress_prompt_kernel,
        grid=(nb,),
        in_specs=[kblk, vblk] + [full(a) for a in ws],
        out_specs=[outblk, outblk],
        out_shape=[jax.ShapeDtypeStruct((nb, LANES, LANES), f32)] * 2,
        compiler_params=_cparams(("parallel",)),
        name="compress_prompt",
    )(ckv, ckv, *ws)


def _rank_select(score, n_blocks):
    blk = lax.broadcasted_iota(jnp.int32, score.shape, 1)
    rank = jnp.zeros(score.shape, f32)
    for m in range(n_blocks):
        sm = score[:, m:m + 1]
        beats = (sm > score) | ((sm == score) & (m < blk))
        rank = rank + jnp.where(beats, 1.0, 0.0)
    return jnp.where(rank < float(N_SEL), 1.0, 0.0)


def _rank_select_rows(score, n_blocks):
    blk = lax.broadcasted_iota(jnp.int32, score.shape, 0)
    rank = jnp.zeros(score.shape, f32)
    for m in range(n_blocks):
        sm = score[m:m + 1, :]
        beats = (sm > score) | ((sm == score) & (m < blk))
        rank = rank + jnp.where(beats, 1.0, 0.0)
    return jnp.where(rank < float(N_SEL), 1.0, 0.0)


def _nsa_prompt_kernel(q_ref, kc_ref, vc_ref, sk_ref, sv_ref, wk_ref, wv_ref, gt_ref,
                       td_ref, ts_ref, we_ref, bc_ref, ovt_ref, eet_ref, o_ref,
                       selx_ref, m_ref, l_ref, acc_ref):
    i = pl.program_id(1)
    lane = lax.broadcasted_iota(jnp.int32, (QT, LANES), 1)
    row = lax.broadcasted_iota(jnp.int32, (QT, LANES), 0)
    n_blk = selx_ref.shape[1] // SEL_BLOCK
    blk = lax.broadcasted_iota(jnp.int32, (n_blk, QT), 0)
    cur = (i * QT + lax.broadcasted_iota(jnp.int32, (n_blk, QT), 1)) >> 6
    gates_t = gt_ref[...].T
    outs = [[None, None] for _ in range(GQA)]

    def tile4(x):
        return jnp.concatenate([x] * GQA, axis=1)

    def scores(q4, k_ref, kt, bias, sel_head=None, live=None):
        off = pl.multiple_of(kt * QT, QT)
        s = _dot_nt(k_ref[pl.ds(off, QT), :].astype(bf16), q4)
        if bias is not None:
            s = s + bias
        if live is not None:
            s = s + jnp.where(live, 0.0, NEG)
        if sel_head is not None:
            s = s + tile4(selx_ref[sel_head, pl.ds(off, QT), :])
        return s, off

    def attend(v_ref, tiles, state=None):
        m = state[0] if state is not None else None
        for s, _ in tiles:
            ms = jnp.max(s, axis=0, keepdims=True)
            m = ms if m is None else jnp.maximum(m, ms)
        p = jnp.concatenate([jnp.exp(s - m) for s, _ in tiles], axis=0)
        v = jnp.concatenate([v_ref[pl.ds(off, QT), :] for _, off in tiles], axis=0)
        l = jnp.sum(p, axis=0, keepdims=True)
        acc = _dot_tn(v.astype(bf16), p.astype(bf16))
        if state is not None:
            alpha = jnp.exp(state[0] - m)
            l = l + alpha * state[1]
            acc = acc + alpha * state[2]
        return m, l, acc

    m_ref[...] = jnp.full(m_ref.shape, NEG, f32)
    l_ref[...] = jnp.zeros(l_ref.shape, f32)
    acc_ref[...] = jnp.zeros(acc_ref.shape, f32)

    q4s, o_cmps = [], []
    for h in range(N_KV):
        half = (lane >= HEAD_DIM) if h else (lane < HEAD_DIM)
        q4 = jnp.concatenate(
            [jnp.where(half, q_ref[:, g * LANES:(g + 1) * LANES] * jnp.asarray(ATT_SCALE, bf16), jnp.zeros((), bf16))
             for g in range(GQA)], axis=0)
        q4s.append(q4)
        bias_c = bc_ref[h, pl.ds(pl.multiple_of(LANES - SUBLANES * i, SUBLANES), LANES), :]
        s = _dot_nt(kc_ref[...].astype(bf16), q4) + bias_c
        e = jnp.exp(s - jnp.max(s, axis=0, keepdims=True))
        p = jnp.where(s > 0.5 * NEG, e / jnp.sum(e, axis=0, keepdims=True), 0.0)
        o_cmps.append(_dot_tn(vc_ref[...].astype(bf16), p.astype(bf16)))
        psum = p[:, 0:QT] + p[:, QT:2 * QT] + p[:, 2 * QT:3 * QT] + p[:, 3 * QT:4 * QT]
        hi = psum.astype(bf16)
        lo = (psum - hi.astype(f32)).astype(bf16)
        imp = (_dot(ovt_ref[...], hi) + _dot(ovt_ref[...], lo))[0:n_blk, :]
        forced = (blk == 0) | (blk == cur) | (blk == cur - 1)
        score = jnp.where(forced, BIG, jnp.where(blk <= cur, imp, -BIG))
        sel = _rank_select_rows(score, n_blk)
        neg = jnp.concatenate([(sel - 1.0) * (-NEG), jnp.zeros((LANES - n_blk, QT), f32)], axis=0).astype(bf16)
        selx_ref[h] = _dot(eet_ref[...], neg)

    n_far = jnp.maximum(i - 1, 0)

    def far_pairs(j, carry):
        kt1 = 2 * j + 1
        tiles = [[scores(q4s[h], sk_ref, 2 * j, None, sel_head=h),
                  scores(q4s[h], sk_ref, jnp.minimum(kt1, i), None, sel_head=h, live=kt1 < n_far)]
                 for h in range(N_KV)]
        for h in range(N_KV):
            m_ref[h], l_ref[h], acc_ref[h] = attend(sv_ref, tiles[h], (m_ref[h], l_ref[h], acc_ref[h]))
        return carry

    lax.fori_loop(0, (n_far + 1) // 2, far_pairs, 0)

    def back_tile(back):
        return dict(kt=jnp.maximum(i - back, 0), live=(i >= back) if back else None)

    slc_tiles = [[scores(q4s[h], sk_ref, bias=tab_ref[h], sel_head=h, **back_tile(back))
                  for back, tab_ref in ((1, ts_ref), (0, td_ref))] for h in range(N_KV)]
    win_tiles = [[scores(q4s[h], wk_ref, bias=bias, **back_tile(back))
                  for back, bias in ((4, we_ref[h]), (3, None), (2, None), (1, ts_ref[h]), (0, td_ref[h]))]
                 for h in range(N_KV)]
    for h in range(N_KV):
        _, l_s, acc_s = attend(sv_ref, slc_tiles[h], (m_ref[h], l_ref[h], acc_ref[h]))
        _, l_w, acc_w = attend(wv_ref, win_tiles[h])
        o_slc = acc_s / l_s
        o_win = acc_w / l_w
        for g in range(GQA):
            c = (h * GQA + g) * 3
            sl = slice(g * QT, (g + 1) * QT)
            outs[g][h] = (gates_t[c:c + 1, :] * o_cmps[h][:, sl] + gates_t[c + 1:c + 2, :] * o_slc[:, sl]
                          + gates_t[c + 2:c + 3, :] * o_win[:, sl])
    for g in range(GQA):
        o_ref[:, g * LANES:(g + 1) * LANES] = jnp.where(row < HEAD_DIM, outs[g][0], outs[g][1]).T.astype(bf16)


def _nsa_prompt(q, kc, vc, skv, wkv, gt, tabs, nb, seq):
    nt = seq // QT
    tile = lambda n: pl.BlockSpec((QT, n), lambda b, i: (b * nt + i, 0))
    kblk = pl.BlockSpec((seq, LANES), lambda b, i: (b, 0))
    vblk = pl.BlockSpec((seq, LANES), lambda b, i: (b, 1))
    cblk = pl.BlockSpec((None, LANES, LANES), lambda b, i: (b, 0, 0))
    full = lambda a: pl.BlockSpec(a.shape, lambda b, i: (0,) * a.ndim)
    consts = (tabs['td'], tabs['ts'], tabs['we'], tabs['bc'], tabs['ovt'], tabs['eet'])
    return pl.pallas_call(
        _nsa_prompt_kernel,
        grid=(nb, nt),
        in_specs=[tile(512), cblk, cblk, kblk, vblk, kblk, vblk, tile(LANES)] + [full(a) for a in consts],
        out_specs=tile(512),
        out_shape=jax.ShapeDtypeStruct((nb * seq, 512), bf16),
        scratch_shapes=[pltpu.VMEM((N_KV, seq, QT), f32), pltpu.VMEM((N_KV, 1, GQA * QT), f32),
                        pltpu.VMEM((N_KV, 1, GQA * QT), f32), pltpu.VMEM((N_KV, LANES, GQA * QT), f32)],
        compiler_params=_cparams(("parallel", "arbitrary")),
        name="nsa_prompt",
    )(q, kc, vc, skv, skv, wkv, wkv, gt, *consts)


N_PAGES = 16
SAMPLE_SEQS_PER_STEP = 4
COMPRESS_SEQS_PER_STEP = 4


def _compress_sample_kernel(n_seq, pt_ref, *refs):
    pages = refs[:n_seq * N_PAGES]
    (wcat_k, wcat_v, pek_ref, pev_ref, w2k_ref, w2v_ref, kg_ref, kc_o, vc_o, hk_ref, hv_ref) = refs[n_seq * N_PAGES:]

    def hist(kind, h_ref):
        for n, pg in enumerate(pages):
            h_ref[pl.ds(n * PAGE, PAGE), :] = pg[kind].T
        return lambda s: h_ref[pl.ds(s, n_seq * LANES, stride=CMP_STRIDE), :]

    kc_o[...] = _pair_rmsnorm(_compress(hist(0, hk_ref), wcat_k, pek_ref, w2k_ref), kg_ref[0:1, :])
    vc_o[...] = _compress(hist(1, hv_ref), wcat_v, pev_ref, w2v_ref)


def _nsa_sample_kernel(n_seq, pt_ref, *refs):
    seq_pages = [refs[t * N_PAGES:(t + 1) * N_PAGES] for t in range(n_seq)]
    (kc_ref, vc_ref, win_ref, qr_ref, gr_ref, knew_ref,
     bc_ref, bs_ref, bw_ref, b0_ref, ov_ref, o_ref, win_o) = refs[n_seq * N_PAGES:]
    for t in range(n_seq):
        rows = slice(t * LANES, (t + 1) * LANES)
        _nsa_sample_one(t, kc_ref[rows, :], vc_ref[rows, :], seq_pages[t], win_ref, qr_ref, gr_ref,
                        knew_ref, bc_ref, bs_ref, bw_ref, b0_ref, ov_ref, o_ref, win_o)


def _nsa_sample_one(t, kc, vc, slc_pages, win_ref, qr_ref, gr_ref, knew_ref, bc_ref, bs_ref, bw_ref, b0_ref,
                    ov_ref, o_ref, win_o):
    lane = lax.broadcasted_iota(jnp.int32, (SUBLANES, LANES), 1)
    qf = qr_ref[t]
    qb = qf.astype(bf16)
    gates = gr_ref[t]
    knew = knew_ref[t]
    hi_rows = lax.broadcasted_iota(jnp.int32, (SUBLANES, LANES), 0) >= GQA
    own = (lane >= HEAD_DIM) == hi_rows

    s = _dot_nt(qb, kc.astype(bf16)) * ATT_SCALE + bc_ref[...]
    mskc = lane < N_CMP
    s = jnp.where(mskc, s, NEG)
    e = jnp.exp(s - jnp.max(s, axis=-1, keepdims=True))
    p = jnp.where(mskc, e / jnp.sum(e, axis=-1, keepdims=True), 0.0)
    o_cmp = _dot(p.astype(bf16), vc.astype(bf16))
    psum = jnp.broadcast_to(jnp.sum(p.reshape(N_KV, GQA, LANES), axis=1, keepdims=True), (N_KV, GQA, LANES))
    imp = _dot_hilo(psum.reshape(SUBLANES, LANES), ov_ref[...])
    n_blk = N_PAGES * PAGE // SEL_BLOCK + 1
    cur = n_blk - 1
    forced = (lane == 0) | (lane == cur) | (lane == cur - 1)
    score = jnp.where(forced, BIG, imp)
    score = jnp.where(lane < n_blk, score, -3e38)
    sel = _rank_select(score, n_blk)

    def softmax_av(s_parts, s_new, vt_parts, v_new):
        mx = s_new
        for sp in s_parts:
            mx = jnp.maximum(mx, jnp.max(sp, axis=-1, keepdims=True))
        p_new = jnp.exp(s_new - mx)
        den = p_new
        acc = p_new * v_new
        for sp, vt in zip(s_parts, vt_parts):
            pp = jnp.exp(sp - mx)
            den = den + jnp.sum(pp, axis=-1, keepdims=True)
            acc = acc + _dot_nt(pp.astype(bf16), vt.astype(bf16))
        return acc / den

    def new_row(r):
        return knew[r:r + 1, :]

    def score_new(krow):
        return jnp.sum(jnp.where(own, qf * krow, 0.0), axis=-1, keepdims=True) * ATT_SCALE + b0_ref[:, 0:1]

    s_parts = []
    for j in range(N_PAGES):
        sj = _dot(qb, slc_pages[j][0].astype(bf16)) * ATT_SCALE + bs_ref[:, j * PAGE:(j + 1) * PAGE]
        mj = jnp.where(lane < SEL_BLOCK, sel[:, 2 * j:2 * j + 1], sel[:, 2 * j + 1:2 * j + 2]) > 0.5
        s_parts.append(jnp.where(mj, sj, NEG))
    o_slc = softmax_av(s_parts, score_new(new_row(0)), [pg[1] for pg in slc_pages], new_row(1))

    w_parts = []
    for j in range(WINDOW // PAGE):
        kw = win_ref[t, 0, :, j * PAGE:(j + 1) * PAGE]
        sj = _dot(qb, kw.astype(bf16)) * ATT_SCALE + bw_ref[:, j * PAGE:(j + 1) * PAGE]
        if j == 0:
            sj = jnp.where(lane >= 1, sj, NEG)
        w_parts.append(sj)
    vt_parts = [win_ref[t, 1, :, j * PAGE:(j + 1) * PAGE] for j in range(WINDOW // PAGE)]
    o_win = softmax_av(w_parts, score_new(new_row(2)), vt_parts, new_row(3))

    o_ref[t] = gates[:, 0:1] * o_cmp + gates[:, 1:2] * o_slc + gates[:, 2:3] * o_win

    new_cols = knew.T
    last = lax.broadcasted_iota(jnp.int32, (LANES, WINDOW), 1) == WINDOW - 1
    for kv in range(2):
        win_o[t, kv] = jnp.where(last, new_cols[:, 2 + kv:3 + kv], pltpu.roll(win_ref[t, kv], WINDOW - 1, 1))


def _nsa_sample(layer, page_table, cache_cmp, cache_slc, win_state, qrows, grows, knew, lw, tabs):
    nb = page_table.shape[0]
    full = lambda a: pl.BlockSpec(a.shape, lambda b, pt: (0,) * a.ndim)

    def page_specs(n_seq):
        return [pl.BlockSpec((None, None, 2, LANES, PAGE),
                             functools.partial(lambda t, j, b, pt: (layer, pt[b * n_seq + t, j], 0, 0, 0), t, j))
                for t in range(n_seq) for j in range(N_PAGES)]

    n_seq = COMPRESS_SEQS_PER_STEP
    ws = (lw['cmp_wk'], lw['cmp_wv'], lw['cmp_pek'], lw['cmp_pev'], lw['cmp_w2k'], lw['cmp_w2v'], lw['k_gain'])
    cblk = pl.BlockSpec((n_seq * LANES, LANES), lambda b, pt: (b, 0))
    kc, vc = pl.pallas_call(
        functools.partial(_compress_sample_kernel, n_seq),
        grid_spec=pltpu.PrefetchScalarGridSpec(
            num_scalar_prefetch=1, grid=(nb // n_seq,),
            in_specs=page_specs(n_seq) + [full(a) for a in ws], out_specs=[cblk, cblk],
            scratch_shapes=[pltpu.VMEM((n_seq * N_PAGES * PAGE, LANES), f32)] * 2),
        out_shape=[jax.ShapeDtypeStruct((nb * LANES, LANES), f32)] * 2,
        compiler_params=_cparams(("parallel",)),
        name="compress_sample",
    )(page_table, *([cache_cmp] * (n_seq * N_PAGES)), *ws)

    n_seq = SAMPLE_SEQS_PER_STEP
    per_b = pl.BlockSpec((n_seq, SUBLANES, LANES), lambda b, pt: (b, 0, 0))
    cblk = pl.BlockSpec((n_seq * LANES, LANES), lambda b, pt: (b, 0))
    consts = (tabs['bc_s'], tabs['bs_s'], tabs['bw_s'], tabs['b0_s'], tabs['ov_s'])
    in_specs = (page_specs(n_seq) + [cblk, cblk]
                + [pl.BlockSpec((None, n_seq, 2, LANES, WINDOW), lambda b, pt: (layer, b, 0, 0, 0)), per_b, per_b, per_b]
                + [full(a) for a in consts])
    return pl.pallas_call(
        functools.partial(_nsa_sample_kernel, n_seq),
        grid_spec=pltpu.PrefetchScalarGridSpec(
            num_scalar_prefetch=1, grid=(nb // n_seq,), in_specs=in_specs,
            out_specs=[per_b, pl.BlockSpec((n_seq, 2, LANES, WINDOW), lambda b, pt: (b, 0, 0, 0))]),
        out_shape=[jax.ShapeDtypeStruct((nb, SUBLANES, LANES), f32),
                   jax.ShapeDtypeStruct((nb, 2, LANES, WINDOW), f32)],
        compiler_params=_cparams(("parallel",)),
        name="nsa_sample",
    )(page_table, *([cache_slc] * (n_seq * N_PAGES)), kc, vc, win_state, qrows, grows, knew, *consts)


def _gla_out(o, gn_ref, gr):
    o = jnp.concatenate([_pair_rmsnorm(o[:, :LANES], gn_ref[:, :LANES]),
                         _pair_rmsnorm(o[:, LANES:], gn_ref[:, LANES:])], axis=1)
    return o * jax.nn.silu(gr)


def _gla_prompt_kernel(gq_ref, gk_ref, gl_ref, gv_ref, gr_ref, gn_ref, o_ref, st_o, st_ref):
    c_rows = GLA_CHUNK
    lane = lax.broadcasted_iota(jnp.int32, (c_rows, LANES), 1)
    row = lax.broadcasted_iota(jnp.int32, (c_rows, LANES), 0)
    tril = (lax.broadcasted_iota(jnp.int32, (c_rows, c_rows), 0)
            >= lax.broadcasted_iota(jnp.int32, (c_rows, c_rows), 1))
    lane_v = lax.broadcasted_iota(jnp.int32, (c_rows, GLA_HEADS * GLA_DV), 1)
    srow = lax.broadcasted_iota(jnp.int32, (GLA_HEADS * GLA_DV, LANES), 0)
    scol = lax.broadcasted_iota(jnp.int32, (GLA_HEADS * GLA_DV, LANES), 1)
    diag = (srow >> 6) == (scol >> 5)
    n_seqs, rows = gq_ref.shape[0], gq_ref.shape[1]

    @pl.when(pl.program_id(1) == 0)
    def _():
        st_ref[...] = jnp.zeros(st_ref.shape, f32)

    def body(c, carry):
        off = pl.multiple_of(c * c_rows, c_rows)
        results = [chunk(sq, off) for sq in range(n_seqs)]
        for sq, (st_new, out) in enumerate(results):
            st_ref[sq] = st_new
            o_ref[sq, pl.ds(off, c_rows), :] = out
        return carry

    def chunk(sq, off):
        q = gq_ref[sq, pl.ds(off, c_rows), :] * (GLA_DK ** -0.5)
        k = gk_ref[sq, pl.ds(off, c_rows), :]
        v = gv_ref[sq, pl.ds(off, c_rows), :]
        b = gl_ref[sq, pl.ds(off, c_rows), :]
        sh = 1
        while sh < c_rows:
            b = b + jnp.where(row >= sh, pltpu.roll(b, sh, 0), 0.0)
            sh *= 2
        qd = q * jnp.exp(b)
        kd = (k * jnp.exp(-b)).astype(bf16)
        bl = b[c_rows - 1:c_rows, :]
        kk = (k * jnp.exp(bl - b)).astype(bf16)
        st = st_ref[sq]
        vb = v.astype(bf16)
        o = _dot_nt(qd.astype(bf16), st.astype(bf16))
        for hh in range(GLA_HEADS):
            qm = jnp.where((lane >> 5) == hh, qd, 0.0).astype(bf16)
            a = jnp.where(tril, _dot_nt(qm, kd), 0.0)
            o = o + jnp.where((lane_v >> 6) == hh, _dot(a.astype(bf16), vb), 0.0)
        st_new = jnp.exp(bl) * st + jnp.where(diag, _dot_tn(vb, kk), 0.0)
        return st_new, _gla_out(o, gn_ref, gr_ref[sq, pl.ds(off, c_rows), :]).astype(bf16)

    lax.fori_loop(0, rows // c_rows, body, 0)
    st_o[...] = st_ref[...]


GLA_SEQS_PER_STEP = 8
GLA_ROWS_PER_STEP = 512


def _gla_prompt(gq, gk, gl, gv, gr, gn, nb, seq):
    ns, rows = GLA_SEQS_PER_STEP, GLA_ROWS_PER_STEP
    by_seq = lambda a: a.reshape(nb, seq, a.shape[-1])
    blk = lambda n: pl.BlockSpec((ns, rows, n), lambda b, r: (b, r, 0))
    o, st = pl.pallas_call(
        _gla_prompt_kernel,
        grid=(nb // ns, seq // rows),
        in_specs=[blk(LANES), blk(LANES), blk(LANES), blk(2 * LANES), blk(2 * LANES),
                  pl.BlockSpec(gn.shape, lambda b, r: (0, 0))],
        out_specs=[blk(2 * LANES), pl.BlockSpec((ns, GLA_HEADS * GLA_DV, LANES), lambda b, r: (b, 0, 0))],
        out_shape=[jax.ShapeDtypeStruct((nb, seq, 2 * LANES), bf16),
                   jax.ShapeDtypeStruct((nb, GLA_HEADS * GLA_DV, LANES), f32)],
        scratch_shapes=[pltpu.VMEM((ns, GLA_HEADS * GLA_DV, LANES), f32)],
        compiler_params=_cparams(("parallel", "arbitrary")),
        name="gla_prompt",
    )(by_seq(gq), by_seq(gk), by_seq(gl), by_seq(gv), by_seq(gr), gn)
    return o.reshape(nb * seq, 2 * LANES), st


def _gla_sample_kernel(cols_ref, v_ref, s0_ref, gr_ref, gn_ref, o_ref, s_o):
    nb = s0_ref.shape[0]
    c = cols_ref[...]
    q = c[:, :, 0:1] * (GLA_DK ** -0.5)
    k = c[:, :, 1:2]
    g = c[:, :, 2:3]
    eg = jnp.exp(g)
    qd = q * eg
    kd = k * jnp.exp(-g)
    s0 = s0_ref[...]
    v = v_ref[...]
    a = jnp.sum((qd * kd).reshape(nb, GLA_HEADS, GLA_DK, 1), axis=2)
    o = jnp.sum((qd * s0).reshape(nb, GLA_HEADS, GLA_DK, GLA_DV), axis=2) + a * v
    vexp = jnp.broadcast_to(v[:, :, None, :], (nb, GLA_HEADS, GLA_DK, GLA_DV)).reshape(nb, GLA_HEADS * GLA_DK, GLA_DV)
    s_o[...] = eg * s0 + k * vexp
    y = o * lax.rsqrt(jnp.mean(o * o, axis=-1, keepdims=True) + EPS) * gn_ref[...]
    o_ref[...] = y * jax.nn.silu(gr_ref[...])


def _gla_sample(cols, v, s0, gr, gn, bt=8):
    nb = s0.shape[0]
    blk = lambda a: pl.BlockSpec((bt,) + a.shape[1:], lambda i: (i,) + (0,) * (a.ndim - 1))
    return pl.pallas_call(
        _gla_sample_kernel,
        grid=(nb // bt,),
        in_specs=[blk(cols), blk(v), blk(s0), blk(gr), pl.BlockSpec(gn.shape, lambda i: (0, 0))],
        out_specs=[blk(v), blk(s0)],
        out_shape=[jax.ShapeDtypeStruct(v.shape, f32), jax.ShapeDtypeStruct(s0.shape, f32)],
        compiler_params=_cparams(("parallel",)),
        name="gla_sample",
    )(cols, v, s0, gr, gn)


def _pool_finish(sums, u, cnt_inv, w_ref, sc_ref):
    lane = lax.broadcasted_iota(jnp.int32, u.shape, 1)
    grp = lane >> 6
    s = jnp.where(grp == 0, sums[0], jnp.where(grp == 1, sums[1], jnp.where(grp == 2, sums[2], sums[3])))
    dlt = s * cnt_inv - u
    return _dot(dlt.astype(bf16), w_ref[...]) * sc_ref[...]


def _pool_prompt_kernel(u_ref, w_ref, sc_ref, o_ref):
    u = u_ref[...]
    row = lax.broadcasted_iota(jnp.int32, u.shape, 0)
    lane = lax.broadcasted_iota(jnp.int32, u.shape, 1)
    sums = []
    s = u
    sh = 1
    while sh < POOL_WINDOWS[-1]:
        s = s + jnp.where(row >= sh, pltpu.roll(s, sh, 0), 0.0)
        sums.append(s)
        sh *= 2
    grp = lane >> 6
    win = jnp.where(grp == 0, POOL_WINDOWS[0], jnp.where(grp == 1, POOL_WINDOWS[1],
                    jnp.where(grp == 2, POOL_WINDOWS[2], POOL_WINDOWS[3])))
    cnt = jnp.minimum(win, row + 1).astype(f32)
    o_ref[...] = _pool_finish(sums, u, 1.0 / cnt, w_ref, sc_ref).astype(bf16)


def _pool_prompt(pu, w, sc, nb, seq):
    blk = pl.BlockSpec((seq, 2 * LANES), lambda b: (b, 0))
    full = lambda a: pl.BlockSpec(a.shape, lambda b: (0,) * a.ndim)
    return pl.pallas_call(
        _pool_prompt_kernel,
        grid=(nb,),
        in_specs=[blk, full(w), full(sc)],
        out_specs=blk,
        out_shape=jax.ShapeDtypeStruct((nb * seq, 2 * LANES), bf16),
        compiler_params=_cparams(("parallel",)),
        name="pool_prompt",
    )(pu, w, sc)


def _pool_sample_kernel(u_ref, buf_ref, w_ref, sc_ref, o_ref):
    u = u_ref[...]
    lane = lax.broadcasted_iota(jnp.int32, u.shape, 1)
    sums = []
    s = u
    nxt = POOL_BUF - 1
    for win in POOL_WINDOWS:
        while POOL_BUF - nxt < win:
            s = s + buf_ref[nxt]
            nxt -= 1
        sums.append(s)
    grp = lane >> 6
    cnt_inv = jnp.where(grp == 0, 1.0 / POOL_WINDOWS[0], jnp.where(grp == 1, 1.0 / POOL_WINDOWS[1],
                        jnp.where(grp == 2, 1.0 / POOL_WINDOWS[2], 1.0 / POOL_WINDOWS[3])))
    o_ref[...] = _pool_finish(sums, u, cnt_inv, w_ref, sc_ref).astype(bf16)


def _pool_sample(pu, buf_t, w, sc):
    full = lambda a: pl.BlockSpec(a.shape, lambda i: (0,) * a.ndim)
    return pl.pallas_call(
        _pool_sample_kernel,
        grid=(1,),
        in_specs=[full(pu), full(buf_t), full(w), full(sc)],
        out_specs=full(pu),
        out_shape=jax.ShapeDtypeStruct(pu.shape, bf16),
        compiler_params=_cparams(("arbitrary",)),
        name="pool_sample",
    )(pu, buf_t, w, sc)


def _out_proj_kernel(a_ref, b_ref, c_ref, x_ref, w_ref, g_ref, x_o, h_o):
    y = (_dot(a_ref[...], w_ref[0:512, :]) + _dot(b_ref[...], w_ref[512:768, :])
         + _dot(c_ref[...], w_ref[768:1024, :]))
    x = x_ref[...] + y
    x_o[...] = x
    h_o[...] = (x * lax.rsqrt(jnp.mean(x * x, axis=-1, keepdims=True) + EPS) * g_ref[...]).astype(bf16)


def _out_proj(o_nsa, o_gla, o_pool, x, w, g, tm):
    m = x.shape[0]
    row = lambda n: pl.BlockSpec((tm, n), lambda i: (i, 0))
    full = lambda a: pl.BlockSpec(a.shape, lambda i: (0,) * a.ndim)
    return pl.pallas_call(
        _out_proj_kernel,
        grid=(m // tm,),
        in_specs=[row(512), row(256), row(256), row(D_MODEL), full(w), full(g)],
        out_specs=[row(D_MODEL), row(D_MODEL)],
        out_shape=[jax.ShapeDtypeStruct((m, D_MODEL), f32), jax.ShapeDtypeStruct((m, D_MODEL), bf16)],
        compiler_params=_cparams(("parallel",)),
        name="out_proj",
    )(o_nsa, o_gla, o_pool, x, w, g)


FFN_ROW_PARTS = 4
HALO = 16


def _ffn_kernel(tiles_per_seq, decode, h_ref, p_ref, x_ref, wa_ref, wu_ref, cw_ref, cb_ref, wd_ref, x_o, a_o):
    hn = h_ref[...]
    tm = hn.shape[0]
    if decode:
        a = _dot(hn, wa_ref[...])
        a2 = p_ref[0]
        a1 = p_ref[1]
        a_o[...] = a
        n_parts = 1
    else:
        first = (pl.program_id(0) % tiles_per_seq) == 0
        ax = _dot(jnp.concatenate([hn, p_ref[...]], axis=0), wa_ref[...])
        a = ax[:tm]
        ah = jnp.where(first, 0.0, ax[tm:])
        p1 = ah[HALO - 1:HALO, :]
        p2 = ah[HALO - 2:HALO - 1, :]
        row = lax.broadcasted_iota(jnp.int32, a.shape, 0)
        a1 = jnp.where(row == 0, p1, pltpu.roll(a, 1, 0))
        a2 = jnp.where(row == 0, p2, jnp.where(row == 1, p1, pltpu.roll(a, 2, 0)))
        a_o[...] = a[tm - SUBLANES:, :]
        n_parts = FFN_ROW_PARTS
    for part in range(n_parts):
        rs = slice(part * tm // n_parts, (part + 1) * tm // n_parts)
        u = _dot(hn[rs], wu_ref[...])
        ac = cb_ref[...] + a2[rs] * cw_ref[0:1, :] + a1[rs] * cw_ref[1:2, :] + a[rs] * cw_ref[2:3, :]
        act = (jax.nn.silu(ac) * u).astype(bf16)
        x_o[rs, :] = x_ref[rs, :] + _dot(act, wd_ref[...])


def _ffn(hn, prev, x, lw, tm, tiles_per_seq, decode):
    m = x.shape[0]
    nt = m // tm
    row = lambda n: pl.BlockSpec((tm, n), lambda i: (i, 0))
    const = lambda a: pl.BlockSpec(a.shape, lambda i: (0,) * a.ndim, pipeline_mode=pl.Buffered(1))
    if decode:
        prev_spec = pl.BlockSpec(prev.shape, lambda i: (0, 0, 0))
        a_spec = row(D_FF)
        a_shape = jax.ShapeDtypeStruct((m, D_FF), f32)
    else:
        per = tm // HALO
        prev_spec = pl.BlockSpec((HALO, D_MODEL), lambda i: (jnp.maximum(i * per - 1, 0), 0))
        a_spec = pl.BlockSpec((None, SUBLANES, D_FF), lambda i: (i, 0, 0))
        a_shape = jax.ShapeDtypeStruct((nt, SUBLANES, D_FF), f32)
    ws = (lw['w_up_a'], lw['w_up_u'], lw['conv_w'], lw['conv_b'], lw['w_down'])
    return pl.pallas_call(
        functools.partial(_ffn_kernel, tiles_per_seq, decode),
        grid=(nt,),
        in_specs=[row(D_MODEL), prev_spec, row(D_MODEL)] + [const(a) for a in ws],
        out_specs=[row(D_MODEL), a_spec],
        out_shape=[jax.ShapeDtypeStruct((m, D_MODEL), f32), a_shape],
        compiler_params=_cparams(("parallel",)),
        name="ffn_decode" if decode else "ffn_prompt",
    )(hn, prev, x, *ws)


def _rel_bucket_np(d):
    d = np.maximum(d, 0)
    exact = REL_BUCKETS // 2
    lg = np.log(np.maximum(d, 1).astype(np.float32) / np.float32(exact)) / np.float32(math.log(REL_MAX_DIST / exact))
    large = np.minimum(exact + (lg * np.float32(REL_BUCKETS - exact)).astype(np.int32), REL_BUCKETS - 1)
    return np.where(d < exact, d, large).astype(np.int32)


def _bias_lookup(rel_bias, dist):
    bucket = _rel_bucket_np(np.asarray(dist)).reshape(-1)
    onehot = jnp.asarray((bucket[:, None] == np.arange(REL_BUCKETS)[None, :]).astype(np.int8)).astype(f32)
    out = jnp.dot(onehot, rel_bias, precision=lax.Precision.HIGHEST)
    return out.reshape(tuple(np.shape(dist)) + (N_HEADS,))


def _bias_tables(rel_bias, seq, past_len):
    c = np.arange(QT)[:, None]
    r = np.arange(QT)[None, :]

    def key_major(t):
        rows = t.shape[0]
        return t.reshape(rows, QT, N_KV, GQA).transpose(2, 0, 3, 1).reshape(N_KV, rows, GQA * QT)

    def masked(t, keep):
        return jnp.where(jnp.asarray(np.tile(keep, (1, GQA)))[None], t, NEG)

    far = jnp.broadcast_to(rel_bias[REL_BUCKETS - 1].reshape(N_KV, 1, GQA, 1), (N_KV, QT, GQA, QT))
    far = far.reshape(N_KV, QT, GQA * QT)
    td = masked(key_major(_bias_lookup(rel_bias, r - c)) - far, c <= r)
    ts = key_major(_bias_lookup(rel_bias, QT + r - c)) - far
    we = masked(jnp.zeros_like(far), c > r)
    j = np.arange(2 * QT)[:, None]
    dist_c = r - (CMP_STRIDE * (j - QT) + CMP_LEN - 1)
    bc = masked(key_major(_bias_lookup(rel_bias, dist_c)), dist_c >= 0)
    cc = np.arange(LANES)[:, None]
    n = np.arange(LANES)[None, :]
    ov = ((cc * CMP_STRIDE < n * SEL_BLOCK + SEL_BLOCK) & (cc * CMP_STRIDE + CMP_LEN > n * SEL_BLOCK) & (cc < N_CMP))
    ovt = jnp.asarray((ov & (n < seq // SEL_BLOCK)).T, dtype=bf16)
    ov_s = jnp.asarray(ov & (n < past_len // SEL_BLOCK + 1), dtype=bf16)
    eet = jnp.asarray((np.arange(seq)[:, None] // SEL_BLOCK) == np.arange(LANES)[None, :], dtype=bf16)

    def look_rows(dist):
        return _bias_lookup(rel_bias, dist).T

    bc_s = look_rows(past_len - (np.arange(LANES) * CMP_STRIDE + CMP_LEN - 1))
    bs_s = look_rows(past_len - np.arange(past_len))
    bw_s = look_rows(WINDOW - np.arange(WINDOW))
    b0_s = look_rows(np.zeros((LANES,), np.int64))
    return dict(td=td, ts=ts, we=we, bc=bc, ovt=ovt, eet=eet,
                ov_s=ov_s, bc_s=bc_s, bs_s=bs_s, bw_s=bw_s, b0_s=b0_s)


def _pair_order(w, axis):
    shape = w.shape
    w = w.reshape(shape[:axis] + (N_KV, GQA, HEAD_DIM) + shape[axis + 1:])
    return jnp.swapaxes(w, axis, axis + 1).reshape(shape)


def _repack_w_in(w):
    z = lambda n: jnp.zeros((w.shape[0], n), w.dtype)
    return jnp.concatenate([
        _pair_order(w[:, :512], 1), w[:, 512:1280],
        w[:, 1280:1304], z(LANES - 24),
        w[:, 1304:2072],
        w[:, 2072:2088], z(LANES - GLA_RANK),
        w[:, 2088:2344]], axis=1)


def _block_diag(blocks):
    n, r, c = blocks.shape
    eye = jnp.eye(n, dtype=blocks.dtype)
    return (eye[:, None, :, None] * blocks[:, :, None, :]).reshape(n * r, n * c)


def _layer_weights(l, p):
    tile2 = lambda v: jnp.tile(v, 2)[None, :]
    w_in = _repack_w_in(p['w_in'][l]).astype(bf16)
    w_out = jnp.concatenate([_pair_order(p['w_out'][l][:512], 0), p['w_out'][l][512:]], axis=0).astype(bf16)
    k_gain = jnp.concatenate([jnp.tile(p['k_gain'][l], (1, 2)), jnp.zeros((SUBLANES - 3, LANES), f32)], axis=0)
    w_alpha = jnp.concatenate([p['gla_w_alpha'][l], jnp.zeros((LANES - GLA_RANK, LANES), f32)], axis=0).astype(bf16)

    def cmp_w(kv):
        w1 = p['cmp_w1'][l, kv]
        two = jax.vmap(lambda w: _block_diag(jnp.stack([w, w])))(w1)
        wcat = jnp.concatenate([two[:CMP_STRIDE], two[CMP_STRIDE:]], axis=-1).astype(bf16)
        pe = jnp.tile(p['cmp_pe'][l, kv], (1, 2))
        pe = jnp.concatenate([jnp.broadcast_to(pe[:CMP_STRIDE, None, :], (CMP_STRIDE, SUBLANES, LANES)),
                              jnp.broadcast_to(pe[CMP_STRIDE:, None, :], (CMP_STRIDE, SUBLANES, LANES))], axis=1)
        w2 = _block_diag(jnp.stack([p['cmp_w2'][l, kv]] * 2)).astype(bf16)
        wcat = wcat.reshape(CMP_STRIDE * LANES, 4 * LANES)
        pe = pe.transpose(1, 0, 2).reshape(2 * SUBLANES, CMP_STRIDE * LANES)
        return wcat, pe, w2

    wk, pek, w2k = cmp_w(0)
    wv, pev, w2v = cmp_w(1)
    w_up = p['w_ffn_up'][l].astype(bf16)
    conv_w = jnp.concatenate([p['ffn_conv_w'][l], jnp.zeros((SUBLANES - 3, D_FF), f32)], axis=0)
    return dict(
        g_mix=p['g_mix'][l][None, :], w_in=w_in, q_gain=tile2(p['q_gain'][l]), k_gain=k_gain,
        w_alpha=w_alpha, b_alpha=p['gla_b_alpha'][l][None, :],
        cmp_wk=wk, cmp_wv=wv, cmp_pek=pek, cmp_pev=pev, cmp_w2k=w2k, cmp_w2v=w2v,
        gla_norm=p['gla_norm'][l][None, :],
        pool_w=_block_diag(p['pool_w'][l]).astype(bf16), pool_scale=p['pool_scale'][l][None, :],
        w_out=w_out,
        g_ffn=p['g_ffn'][l][None, :],
        w_up_a=w_up[:, :D_FF], w_up_u=w_up[:, D_FF:], conv_w=conv_w, conv_b=p['ffn_conv_b'][l][None, :],
        w_down=p['w_ffn_down'][l].astype(bf16),
    )


def _prompt_layer(x, lw, tabs, nb, seq):
    tm = 512
    (q, ckv, skv, wkv, gt, gq, gk, gv, gr, gl, pu) = _in_proj(x, lw, tm)
    kc, vc = _compress_prompt(ckv, lw, nb, seq)
    o_nsa = _nsa_prompt(q, kc, vc, skv, wkv, gt, tabs, nb, seq)
    o_gla, st = _gla_prompt(gq, gk, gl, gv, gr, lw['gla_norm'], nb, seq)
    o_pool = _pool_prompt(pu, lw['pool_w'], lw['pool_scale'], nb, seq)
    x_mid, hn = _out_proj(o_nsa, o_gla, o_pool, x, lw['w_out'], lw['g_ffn'], tm)
    x_out, a_tail = _ffn(hn, hn, x_mid, lw, tm, seq // tm, False)
    win_tail = wkv.reshape(nb, seq, 2 * LANES)[:, seq - WINDOW:]
    st = st.reshape(nb, GLA_HEADS, GLA_DV, GLA_HEADS, GLA_DK)
    gla_state = jnp.stack([st[:, h, :, h, :] for h in range(GLA_HEADS)], axis=1).transpose(0, 1, 3, 2)
    pool_state = pu.reshape(nb, seq, 2 * LANES)[:, seq - POOL_BUF:]
    a_tail = a_tail.reshape(nb, seq // tm, SUBLANES, D_FF)
    conv_state = a_tail[:, -1, SUBLANES - 2:, :]
    return x_out, (ckv, skv, win_tail, gla_state, pool_state, conv_state)


def _sample_layer(l, x, lw, tabs, page_table, cache_cmp, cache_slc, win_all, state_gla, state_pool, state_conv):
    nb = x.shape[0]
    (q, ckv, skv, wkv, gt, gq, gk, gv, gr, gl, pu) = _in_proj(x, lw, nb)
    lane_hi = (np.arange(LANES) >= HEAD_DIM)
    own = jnp.asarray((lane_hi[None, :] == (np.arange(N_KV)[:, None] == 1)).astype(np.float32))
    qrows = (q.astype(f32).reshape(nb, 1, GQA, LANES) * own[None, :, None, :]).reshape(nb, N_HEADS, LANES)
    grows = jnp.pad(gt[:, :N_HEADS * 3].reshape(nb, N_HEADS, 3), ((0, 0), (0, 0), (0, LANES - 3)))
    knew = jnp.pad(jnp.concatenate([skv, wkv], axis=1).reshape(nb, 4, LANES), ((0, 0), (0, SUBLANES - 4), (0, 0)))
    o, win_next = _nsa_sample(l, page_table, cache_cmp, cache_slc, win_all, qrows, grows, knew, lw, tabs)
    o = o.reshape(nb, N_KV, GQA, N_KV, HEAD_DIM)
    o_nsa = jnp.stack([o[:, h, :, h, :] for h in range(N_KV)], axis=2).reshape(nb, GQA * LANES).astype(bf16)
    cols = jnp.stack([gq, gk, gl], axis=-1)
    o_gla, s_new = _gla_sample(cols, gv.reshape(nb, GLA_HEADS, GLA_DV),
                               state_gla[l].reshape(nb, GLA_HEADS * GLA_DK, GLA_DV),
                               gr.reshape(nb, GLA_HEADS, GLA_DV), lw['gla_norm'].reshape(GLA_HEADS, GLA_DV))
    o_gla = o_gla.reshape(nb, GLA_HEADS * GLA_DV).astype(bf16)
    o_pool = _pool_sample(pu, state_pool[l].transpose(1, 0, 2), lw['pool_w'], lw['pool_scale'])
    x_mid, hn = _out_proj(o_nsa, o_gla, o_pool, x, lw['w_out'], lw['g_ffn'], nb)
    x_out, a = _ffn(hn, state_conv[l].transpose(1, 0, 2), x_mid, lw, nb, 1, True)
    gla_state = s_new.reshape(nb, GLA_HEADS, GLA_DK, GLA_DV)
    return x_out, (ckv, skv, win_next, gla_state, pu, a)


def kernel(x_prompt, x_sample, cache_cmp_kv, cache_slc_kv, page_table, state_win_kv, state_gla, state_pool,
           state_ffn_conv, rel_bias, g_mix, w_in, q_gain, k_gain, cmp_pe, cmp_w1, cmp_w2, gla_w_alpha, gla_b_alpha,
           gla_norm, pool_w, pool_scale, w_out, g_ffn, w_ffn_up, ffn_conv_w, ffn_conv_b, w_ffn_down):
    nb, seq, _ = x_prompt.shape
    db = x_sample.shape[0]
    depth = w_in.shape[0]
    n_phys = cache_cmp_kv.shape[1]
    past_len = page_table.shape[1] * cache_cmp_kv.shape[2]
    assert (seq, past_len, page_table.shape[1], cache_cmp_kv.shape[2]) == (2048, 2048, N_PAGES, PAGE)
    assert state_win_kv.shape[2] == WINDOW and x_sample.shape[1] == 1
    params = dict(g_mix=g_mix, w_in=w_in, q_gain=q_gain, k_gain=k_gain, cmp_pe=cmp_pe, cmp_w1=cmp_w1, cmp_w2=cmp_w2,
                  gla_w_alpha=gla_w_alpha, gla_b_alpha=gla_b_alpha, gla_norm=gla_norm, pool_w=pool_w,
                  pool_scale=pool_scale, w_out=w_out, g_ffn=g_ffn, w_ffn_up=w_ffn_up, ffn_conv_w=ffn_conv_w,
                  ffn_conv_b=ffn_conv_b, w_ffn_down=w_ffn_down)
    tabs = _bias_tables(rel_bias, seq, past_len)
    cache_cmp = cache_cmp_kv.transpose(0, 1, 3, 4, 5, 2).reshape(depth, n_phys, 2, LANES, PAGE)
    cache_slc = cache_slc_kv.transpose(0, 1, 3, 4, 5, 2).reshape(depth, n_phys, 2, LANES, PAGE)
    win_all = state_win_kv.transpose(0, 1, 3, 4, 5, 2).reshape(depth, db, 2, LANES, WINDOW)
    xp = x_prompt.reshape(nb * seq, D_MODEL)
    xs = x_sample.reshape(db, D_MODEL)
    st_p = [[] for _ in range(6)]
    st_s = [[] for _ in range(6)]
    for l in range(depth):
        lw = _layer_weights(l, params)
        xp, new_p = _prompt_layer(xp, lw, tabs, nb, seq)
        xs, new_s = _sample_layer(l, xs, lw, tabs, page_table, cache_cmp, cache_slc, win_all, state_gla,
                                  state_pool, state_ffn_conv)
        for i in range(6):
            st_p[i].append(new_p[i])
            st_s[i].append(new_s[i])
    stk = lambda a: jnp.stack(a, axis=0)
    kv = lambda a, rows: a.reshape(depth, a.shape[1] // rows, rows, 2, N_KV, HEAD_DIM)
    outs_p = (kv(stk(st_p[0]), seq), kv(stk(st_p[1]), seq), kv(stk(st_p[2]).reshape(depth, nb * WINDOW, -1), WINDOW),
              stk(st_p[3]), stk(st_p[4]), stk(st_p[5]))
    shift_in = lambda old, new: jnp.concatenate([old[:, :, 1:], stk(new)[:, :, None, :]], axis=2)
    win_s = stk(st_s[2]).reshape(depth, db, 2, N_KV, HEAD_DIM, WINDOW).transpose(0, 1, 5, 2, 3, 4)
    outs_s = (kv(stk(st_s[0]), 1), kv(stk(st_s[1]), 1), win_s, stk(st_s[3]),
              shift_in(state_pool, st_s[4]), shift_in(state_ffn_conv, st_s[5]))
    return (xp.reshape(nb, seq, D_MODEL), xs.reshape(db, 1, D_MODEL), *outs_p, *outs_s)
```

```python
import functools
import math

import numpy as np
import jax
import jax.numpy as jnp
from jax import lax
from jax.experimental import pallas as pl
from jax.experimental.pallas import tpu as pltpu

f32 = jnp.float32
bf16 = jnp.bfloat16

D_MODEL = 1024
HEAD_DIM = 64
N_KV = 2
GQA = 4
N_HEADS = 8
CMP_STRIDE = 16
CMP_LEN = 32
N_CMP = 127
SEL_BLOCK = 64
N_SEL = 16
WINDOW = 512
ATT_SCALE = HEAD_DIM ** -0.5
REL_BUCKETS = 32
REL_MAX_DIST = 128
GLA_HEADS = 4
GLA_DK = 32
GLA_DV = 64
GLA_RANK = 16
GLA_TAU = 16.0
GLA_CHUNK = 64
POOL_WINDOWS = (2, 4, 8, 16)
POOL_BUF = 15
D_FF = 2816
EPS = 1e-6
NEG = -1e30
BIG = 1e9
PAGE = 128

LANES = 128
SUBLANES = 8
QT = 128
VMEM_LIMIT = 56 * 1024 * 1024


def _cparams(sem):
    return pltpu.CompilerParams(dimension_semantics=sem, vmem_limit_bytes=VMEM_LIMIT)


def _dot(a, b):
    return jnp.dot(a, b, preferred_element_type=f32)


def _dot_nt(a, b):
    return lax.dot_general(a, b, (((1,), (1,)), ((), ())), preferred_element_type=f32)


def _dot_tn(a, b):
    return lax.dot_general(a, b, (((0,), (0,)), ((), ())), preferred_element_type=f32)


def _dot_hilo(a, b_bf16):
    hi = a.astype(bf16)
    lo = (a - hi.astype(f32)).astype(bf16)
    return _dot(hi, b_bf16) + _dot(lo, b_bf16)


def _pair_rmsnorm(y, gain):
    lane = lax.broadcasted_iota(jnp.int32, y.shape, y.ndim - 1)
    lo = lane < HEAD_DIM
    y2 = y * y
    s_lo = jnp.sum(jnp.where(lo, y2, 0.0), axis=-1, keepdims=True)
    s_hi = jnp.sum(jnp.where(lo, 0.0, y2), axis=-1, keepdims=True)
    ms = jnp.where(lo, s_lo, s_hi) * (1.0 / HEAD_DIM)
    return y * lax.rsqrt(ms + EPS) * gain


_C_Q, _C_CK, _C_SK, _C_WK = 0, 512, 768, 1024
_C_GATE, _C_GK, _C_GV, _C_GR, _C_PU = 1280, 1536, 1664, 1920, 2304


def _proj_kernel(x_ref, g_ref, w_ref, qg_ref, kg_ref, wa_ref, ba_ref,
                 q_o, ckv_o, skv_o, wkv_o, gt_o, gq_o, gk_o, gv_o, gr_o, gl_o, pu_o):
    x = x_ref[...]
    hn = (x * lax.rsqrt(jnp.mean(x * x, axis=-1, keepdims=True) + EPS) * g_ref[...]).astype(bf16)

    def proj(c0):
        y = _dot(hn, w_ref[:, c0:c0 + 2 * LANES])
        return y[:, :LANES], y[:, LANES:]

    for j in range(2):
        for half, y in enumerate(proj(_C_Q + 2 * j * LANES)):
            c = (2 * j + half) * LANES
            q_o[:, c:c + LANES] = _pair_rmsnorm(y, qg_ref[...]).astype(bf16)
    ckv_o[:, :LANES], ckv_o[:, LANES:] = proj(_C_CK)
    k, skv_o[:, LANES:] = proj(_C_SK)
    skv_o[:, :LANES] = _pair_rmsnorm(k, kg_ref[1:2, :])
    k, wkv_o[:, LANES:] = proj(_C_WK)
    wkv_o[:, :LANES] = _pair_rmsnorm(k, kg_ref[2:3, :])
    gate, gq_o[...] = proj(_C_GATE)
    gt_o[...] = jax.nn.sigmoid(gate)
    gk_o[...], gv_o[:, :LANES] = proj(_C_GK)
    gv_o[:, LANES:], gr_o[:, :LANES] = proj(_C_GV + LANES)
    gr_o[:, LANES:], glr = proj(_C_GR + LANES)
    z = _dot(glr.astype(bf16), wa_ref[...]) + ba_ref[...]
    gl_o[...] = jax.nn.log_sigmoid(z) * (1.0 / GLA_TAU)
    pu_o[:, :LANES], pu_o[:, LANES:] = proj(_C_PU)


def _in_proj(x, lw, tm):
    m = x.shape[0]
    row = lambda n: pl.BlockSpec((tm, n), lambda i: (i, 0))
    full = lambda a: pl.BlockSpec(a.shape, lambda i: (0,) * a.ndim)
    widths = (512, 256, 256, 256, 128, 128, 128, 256, 256, 128, 256)
    dtypes = (bf16,) + (f32,) * 10
    args = (x, lw['g_mix'], lw['w_in'], lw['q_gain'], lw['k_gain'], lw['w_alpha'], lw['b_alpha'])
    return pl.pallas_call(
        _proj_kernel,
        grid=(m // tm,),
        in_specs=[row(D_MODEL)] + [full(a) for a in args[1:]],
        out_specs=[row(n) for n in widths],
        out_shape=[jax.ShapeDtypeStruct((m, n), dt) for n, dt in zip(widths, dtypes)],
        compiler_params=_cparams(("parallel",)),
        name="in_proj",
    )(*args)


def _compress(get_x, w_ref, pe_ref, w2_ref):
    xs = jnp.concatenate([get_x(s).astype(bf16) for s in range(CMP_STRIDE)], axis=1)
    xs = jnp.concatenate([xs, pe_ref[...].astype(bf16)], axis=0)
    rows = xs.shape[0] - 2 * SUBLANES
    acc = _dot(xs, w_ref[...])
    a0 = acc[:rows, :2 * LANES]
    a1 = acc[:rows, 2 * LANES:]
    c0 = acc[rows:rows + 1, :2 * LANES] + acc[rows + SUBLANES:rows + SUBLANES + 1, 2 * LANES:]
    h = a0 + pltpu.roll(a1, rows - 1, 0) + c0
    return _dot(jax.nn.gelu(h).astype(bf16), w2_ref[...])


def _compress_prompt_kernel(ck_ref, cv_ref, wk_ref, wv_ref, pek_ref, pev_ref, w2k_ref, w2v_ref, kg_ref, kc_o, vc_o):
    kx = lambda s: ck_ref[pl.ds(s, LANES, stride=CMP_STRIDE), :]
    vx = lambda s: cv_ref[pl.ds(s, LANES, stride=CMP_STRIDE), :]
    kc_o[...] = _pair_rmsnorm(_compress(kx, wk_ref, pek_ref, w2k_ref), kg_ref[0:1, :])
    vc_o[...] = _compress(vx, wv_ref, pev_ref, w2v_ref)


def _compress_prompt(ckv, lw, nb, seq):
    full = lambda a: pl.BlockSpec(a.shape, lambda b: (0,) * a.ndim)
    kblk = pl.BlockSpec((seq, LANES), lambda b: (b, 0))
    vblk = pl.BlockSpec((seq, LANES), lambda b: (b, 1))
    outblk = pl.BlockSpec((None, LANES, LANES), lambda b: (b, 0, 0))
    ws = (lw['cmp_wk'], lw['cmp_wv'], lw['cmp_pek'], lw['cmp_pev'], lw['cmp_w2k'], lw['cmp_w2v'], lw['k_gain'])
    return pl.pallas_call(
        _compress_prompt_kernel,
        grid=(nb,),
        in_specs=[kblk, vblk] + [full(a) for a in ws],
        out_specs=[outblk, outblk],
        out_shape=[jax.ShapeDtypeStruct((nb, LANES, LANES), f32)] * 2,
        compiler_params=_cparams(("parallel",)),
        name="compress_prompt",
    )(ckv, ckv, *ws)


def _rank_select(score, n_blocks):
    blk = lax.broadcasted_iota(jnp.int32, score.shape, 1)
    rank = jnp.zeros(score.shape, f32)
    for m in range(n_blocks):
        sm = score[:, m:m + 1]
        beats = (sm > score) | ((sm == score) & (m < blk))
        rank = rank + jnp.where(beats, 1.0, 0.0)
    return jnp.where(rank < float(N_SEL), 1.0, 0.0)


def _rank_select_rows(score, n_blocks):
    blk = lax.broadcasted_iota(jnp.int32, score.shape, 0)
    rank = jnp.zeros(score.shape, f32)
    for m in range(n_blocks):
        sm = score[m:m + 1, :]
        beats = (sm > score) | ((sm == score) & (m < blk))
        rank = rank + jnp.where(beats, 1.0, 0.0)
    return jnp.where(rank < float(N_SEL), 1.0, 0.0)


def _nsa_prompt_kernel(q_ref, kc_ref, vc_ref, sk_ref, sv_ref, wk_ref, wv_ref, gt_ref,
                       td_ref, ts_ref, we_ref, bc_ref, ovt_ref, eet_ref, o_ref,
                       selx_ref, m_ref, l_ref, acc_ref):
    i = pl.program_id(1)
    lane = lax.broadcasted_iota(jnp.int32, (QT, LANES), 1)
    row = lax.broadcasted_iota(jnp.int32, (QT, LANES), 0)
    n_blk = selx_ref.shape[1] // SEL_BLOCK
    blk = lax.broadcasted_iota(jnp.int32, (n_blk, QT), 0)
    cur = (i * QT + lax.broadcasted_iota(jnp.int32, (n_blk, QT), 1)) >> 6
    gates_t = gt_ref[...].T
    outs = [[None, None] for _ in range(GQA)]

    def tile4(x):
        return jnp.concatenate([x] * GQA, axis=1)

    def scores(q4, k_ref, kt, bias, sel_head=None, live=None):
        off = pl.multiple_of(kt * QT, QT)
        s = _dot_nt(k_ref[pl.ds(off, QT), :].astype(bf16), q4)
        if bias is not None:
            s = s + bias
        if live is not None:
            s = s + jnp.where(live, 0.0, NEG)
        if sel_head is not None:
            s = s + tile4(selx_ref[sel_head, pl.ds(off, QT), :])
        return s, off

    def attend(v_ref, tiles, state=None):
        m = state[0] if state is not None else None
        for s, _ in tiles:
            ms = jnp.max(s, axis=0, keepdims=True)
            m = ms if m is None else jnp.maximum(m, ms)
        p = jnp.concatenate([jnp.exp(s - m) for s, _ in tiles], axis=0)
        v = jnp.concatenate([v_ref[pl.ds(off, QT), :] for _, off in tiles], axis=0)
        l = jnp.sum(p, axis=0, keepdims=True)
        acc = _dot_tn(v.astype(bf16), p.astype(bf16))
        if state is not None:
            alpha = jnp.exp(state[0] - m)
            l = l + alpha * state[1]
            acc = acc + alpha * state[2]
        return m, l, acc

    m_ref[...] = jnp.full(m_ref.shape, NEG, f32)
    l_ref[...] = jnp.zeros(l_ref.shape, f32)
    acc_ref[...] = jnp.zeros(acc_ref.shape, f32)

    q4s, o_cmps = [], []
    for h in range(N_KV):
        half = (lane >= HEAD_DIM) if h else (lane < HEAD_DIM)
        q4 = jnp.concatenate(
            [jnp.where(half, q_ref[:, g * LANES:(g + 1) * LANES] * jnp.asarray(ATT_SCALE, bf16), jnp.zeros((), bf16))
             for g in range(GQA)], axis=0)
        q4s.append(q4)
        bias_c = bc_ref[h, pl.ds(pl.multiple_of(LANES - SUBLANES * i, SUBLANES), LANES), :]
        s = _dot_nt(kc_ref[...].astype(bf16), q4) + bias_c
        e = jnp.exp(s - jnp.max(s, axis=0, keepdims=True))
        p = jnp.where(s > 0.5 * NEG, e / jnp.sum(e, axis=0, keepdims=True), 0.0)
        o_cmps.append(_dot_tn(vc_ref[...].astype(bf16), p.astype(bf16)))
        psum = p[:, 0:QT] + p[:, QT:2 * QT] + p[:, 2 * QT:3 * QT] + p[:, 3 * QT:4 * QT]
        hi = psum.astype(bf16)
        lo = (psum - hi.astype(f32)).astype(bf16)
        imp = (_dot(ovt_ref[...], hi) + _dot(ovt_ref[...], lo))[0:n_blk, :]
        forced = (blk == 0) | (blk == cur) | (blk == cur - 1)
        score = jnp.where(forced, BIG, jnp.where(blk <= cur, imp, -BIG))
        sel = _rank_select_rows(score, n_blk)
        neg = jnp.concatenate([(sel - 1.0) * (-NEG), jnp.zeros((LANES - n_blk, QT), f32)], axis=0).astype(bf16)
        selx_ref[h] = _dot(eet_ref[...], neg)

    n_far = jnp.maximum(i - 1, 0)

    def far_pairs(j, carry):
        kt1 = 2 * j + 1
        tiles = [[scores(q4s[h], sk_ref, 2 * j, None, sel_head=h),
                  scores(q4s[h], sk_ref, jnp.minimum(kt1, i), None, sel_head=h, live=kt1 < n_far)]
                 for h in range(N_KV)]
        for h in range(N_KV):
            m_ref[h], l_ref[h], acc_ref[h] = attend(sv_ref, tiles[h], (m_ref[h], l_ref[h], acc_ref[h]))
        return carry

    lax.fori_loop(0, (n_far + 1) // 2, far_pairs, 0)

    def back_tile(back):
        return dict(kt=jnp.maximum(i - back, 0), live=(i >= back) if back else None)

    slc_tiles = [[scores(q4s[h], sk_ref, bias=tab_ref[h], sel_head=h, **back_tile(back))
                  for back, tab_ref in ((1, ts_ref), (0, td_ref))] for h in range(N_KV)]
    win_tiles = [[scores(q4s[h], wk_ref, bias=bias, **back_tile(back))
                  for back, bias in ((4, we_ref[h]), (3, None), (2, None), (1, ts_ref[h]), (0, td_ref[h]))]
                 for h in range(N_KV)]
    for h in range(N_KV):
        _, l_s, acc_s = attend(sv_ref, slc_tiles[h], (m_ref[h], l_ref[h], acc_ref[h]))
        _, l_w, acc_w = attend(wv_ref, win_tiles[h])
        o_slc = acc_s / l_s
        o_win = acc_w / l_w
        for g in range(GQA):
            c = (h * GQA + g) * 3
            sl = slice(g * QT, (g + 1) * QT)
            outs[g][h] = (gates_t[c:c + 1, :] * o_cmps[h][:, sl] + gates_t[c + 1:c + 2, :] * o_slc[:, sl]
                          + gates_t[c + 2:c + 3, :] * o_win[:, sl])
    for g in range(GQA):
        o_ref[:, g * LANES:(g + 1) * LANES] = jnp.where(row < HEAD_DIM, outs[g][0], outs[g][1]).T.astype(bf16)


def _nsa_prompt(q, kc, vc, skv, wkv, gt, tabs, nb, seq):
    nt = seq // QT
    tile = lambda n: pl.BlockSpec((QT, n), lambda b, i: (b * nt + i, 0))
    kblk = pl.BlockSpec((seq, LANES), lambda b, i: (b, 0))
    vblk = pl.BlockSpec((seq, LANES), lambda b, i: (b, 1))
    cblk = pl.BlockSpec((None, LANES, LANES), lambda b, i: (b, 0, 0))
    full = lambda a: pl.BlockSpec(a.shape, lambda b, i: (0,) * a.ndim)
    consts = (tabs['td'], tabs['ts'], tabs['we'], tabs['bc'], tabs['ovt'], tabs['eet'])
    return pl.pallas_call(
        _nsa_prompt_kernel,
        grid=(nb, nt),
        in_specs=[tile(512), cblk, cblk, kblk, vblk, kblk, vblk, tile(LANES)] + [full(a) for a in consts],
        out_specs=tile(512),
        out_shape=jax.ShapeDtypeStruct((nb * seq, 512), bf16),
        scratch_shapes=[pltpu.VMEM((N_KV, seq, QT), f32), pltpu.VMEM((N_KV, 1, GQA * QT), f32),
                        pltpu.VMEM((N_KV, 1, GQA * QT), f32), pltpu.VMEM((N_KV, LANES, GQA * QT), f32)],
        compiler_params=_cparams(("parallel", "arbitrary")),
        name="nsa_prompt",
    )(q, kc, vc, skv, skv, wkv, wkv, gt, *consts)


N_PAGES = 16
SAMPLE_SEQS_PER_STEP = 4
COMPRESS_SEQS_PER_STEP = 4


def _compress_sample_kernel(n_seq, pt_ref, *refs):
    pages = refs[:n_seq * N_PAGES]
    (wcat_k, wcat_v, pek_ref, pev_ref, w2k_ref, w2v_ref, kg_ref, kc_o, vc_o, hk_ref, hv_ref) = refs[n_seq * N_PAGES:]

    def hist(kind, h_ref):
        for n, pg in enumerate(pages):
            h_ref[pl.ds(n * PAGE, PAGE), :] = pg[kind].T
        return lambda s: h_ref[pl.ds(s, n_seq * LANES, stride=CMP_STRIDE), :]

    kc_o[...] = _pair_rmsnorm(_compress(hist(0, hk_ref), wcat_k, pek_ref, w2k_ref), kg_ref[0:1, :])
    vc_o[...] = _compress(hist(1, hv_ref), wcat_v, pev_ref, w2v_ref)


def _nsa_sample_kernel(n_seq, pt_ref, *refs):
    seq_pages = [refs[t * N_PAGES:(t + 1) * N_PAGES] for t in range(n_seq)]
    (kc_ref, vc_ref, win_ref, qr_ref, gr_ref, knew_ref,
     bc_ref, bs_ref, bw_ref, b0_ref, ov_ref, o_ref, win_o) = refs[n_seq * N_PAGES:]
    for t in range(n_seq):
        rows = slice(t * LANES, (t + 1) * LANES)
        _nsa_sample_one(t, kc_ref[rows, :], vc_ref[rows, :], seq_pages[t], win_ref, qr_ref, gr_ref,
                        knew_ref, bc_ref, bs_ref, bw_ref, b0_ref, ov_ref, o_ref, win_o)


def _nsa_sample_one(t, kc, vc, slc_pages, win_ref, qr_ref, gr_ref, knew_ref, bc_ref, bs_ref, bw_ref, b0_ref,
                    ov_ref, o_ref, win_o):
    lane = lax.broadcasted_iota(jnp.int32, (SUBLANES, LANES), 1)
    qf = qr_ref[t]
    qb = qf.astype(bf16)
    gates = gr_ref[t]
    knew = knew_ref[t]
    hi_rows = lax.broadcasted_iota(jnp.int32, (SUBLANES, LANES), 0) >= GQA
    own = (lane >= HEAD_DIM) == hi_rows

    s = _dot_nt(qb, kc.astype(bf16)) * ATT_SCALE + bc_ref[...]
    mskc = lane < N_CMP
    s = jnp.where(mskc, s, NEG)
    e = jnp.exp(s - jnp.max(s, axis=-1, keepdims=True))
    p = jnp.where(mskc, e / jnp.sum(e, axis=-1, keepdims=True), 0.0)
    o_cmp = _dot(p.astype(bf16), vc.astype(bf16))
    psum = jnp.broadcast_to(jnp.sum(p.reshape(N_KV, GQA, LANES), axis=1, keepdims=True), (N_KV, GQA, LANES))
    imp = _dot_hilo(psum.reshape(SUBLANES, LANES), ov_ref[...])
    n_blk = N_PAGES * PAGE // SEL_BLOCK + 1
    cur = n_blk - 1
    forced = (lane == 0) | (lane == cur) | (lane == cur - 1)
    score = jnp.where(forced, BIG, imp)
    score = jnp.where(lane < n_blk, score, -3e38)
    sel = _rank_select(score, n_blk)

    def softmax_av(s_parts, s_new, vt_parts, v_new):
        mx = s_new
        for sp in s_parts:
            mx = jnp.maximum(mx, jnp.max(sp, axis=-1, keepdims=True))
        p_new = jnp.exp(s_new - mx)
        den = p_new
        acc = p_new * v_new
        for sp, vt in zip(s_parts, vt_parts):
            pp = jnp.exp(sp - mx)
            den = den + jnp.sum(pp, axis=-1, keepdims=True)
            acc = acc + _dot_nt(pp.astype(bf16), vt.astype(bf16))
        return acc / den

    def new_row(r):
        return knew[r:r + 1, :]

    def score_new(krow):
        return jnp.sum(jnp.where(own, qf * krow, 0.0), axis=-1, keepdims=True) * ATT_SCALE + b0_ref[:, 0:1]

    s_parts = []
    for j in range(N_PAGES):
        sj = _dot(qb, slc_pages[j][0].astype(bf16)) * ATT_SCALE + bs_ref[:, j * PAGE:(j + 1) * PAGE]
        mj = jnp.where(lane < SEL_BLOCK, sel[:, 2 * j:2 * j + 1], sel[:, 2 * j + 1:2 * j + 2]) > 0.5
        s_parts.append(jnp.where(mj, sj, NEG))
    o_slc = softmax_av(s_parts, score_new(new_row(0)), [pg[1] for pg in slc_pages], new_row(1))

    w_parts = []
    for j in range(WINDOW // PAGE):
        kw = win_ref[t, 0, :, j * PAGE:(j + 1) * PAGE]
        sj = _dot(qb, kw.astype(bf16)) * ATT_SCALE + bw_ref[:, j * PAGE:(j + 1) * PAGE]
        if j == 0:
            sj = jnp.where(lane >= 1, sj, NEG)
        w_parts.append(sj)
    vt_parts = [win_ref[t, 1, :, j * PAGE:(j + 1) * PAGE] for j in range(WINDOW // PAGE)]
    o_win = softmax_av(w_parts, score_new(new_row(2)), vt_parts, new_row(3))

    o_ref[t] = gates[:, 0:1] * o_cmp + gates[:, 1:2] * o_slc + gates[:, 2:3] * o_win

    new_cols = knew.T
    last = lax.broadcasted_iota(jnp.int32, (LANES, WINDOW), 1) == WINDOW - 1
    for kv in range(2):
        win_o[t, kv] = jnp.where(last, new_cols[:, 2 + kv:3 + kv], pltpu.roll(win_ref[t, kv], WINDOW - 1, 1))


def _nsa_sample(layer, page_table, cache_cmp, cache_slc, win_state, qrows, grows, knew, lw, tabs):
    nb = page_table.shape[0]
    full = lambda a: pl.BlockSpec(a.shape, lambda b, pt: (0,) * a.ndim)

    def page_specs(n_seq):
        return [pl.BlockSpec((None, None, 2, LANES, PAGE),
                             functools.partial(lambda t, j, b, pt: (layer, pt[b * n_seq + t, j], 0, 0, 0), t, j))
                for t in range(n_seq) for j in range(N_PAGES)]

    n_seq = COMPRESS_SEQS_PER_STEP
    ws = (lw['cmp_wk'], lw['cmp_wv'], lw['cmp_pek'], lw['cmp_pev'], lw['cmp_w2k'], lw['cmp_w2v'], lw['k_gain'])
    cblk = pl.BlockSpec((n_seq * LANES, LANES), lambda b, pt: (b, 0))
    kc, vc = pl.pallas_call(
        functools.partial(_compress_sample_kernel, n_seq),
        grid_spec=pltpu.PrefetchScalarGridSpec(
            num_scalar_prefetch=1, grid=(nb // n_seq,),
            in_specs=page_specs(n_seq) + [full(a) for a in ws], out_specs=[cblk, cblk],
            scratch_shapes=[pltpu.VMEM((n_seq * N_PAGES * PAGE, LANES), f32)] * 2),
        out_shape=[jax.ShapeDtypeStruct((nb * LANES, LANES), f32)] * 2,
        compiler_params=_cparams(("parallel",)),
        name="compress_sample",
    )(page_table, *([cache_cmp] * (n_seq * N_PAGES)), *ws)

    n_seq = SAMPLE_SEQS_PER_STEP
    per_b = pl.BlockSpec((n_seq, SUBLANES, LANES), lambda b, pt: (b, 0, 0))
    cblk = pl.BlockSpec((n_seq * LANES, LANES), lambda b, pt: (b, 0))
    consts = (tabs['bc_s'], tabs['bs_s'], tabs['bw_s'], tabs['b0_s'], tabs['ov_s'])
    in_specs = (page_specs(n_seq) + [cblk, cblk]
                + [pl.BlockSpec((None, n_seq, 2, LANES, WINDOW), lambda b, pt: (layer, b, 0, 0, 0)), per_b, per_b, per_b]
                + [full(a) for a in consts])
    return pl.pallas_call(
        functools.partial(_nsa_sample_kernel, n_seq),
        grid_spec=pltpu.PrefetchScalarGridSpec(
            num_scalar_prefetch=1, grid=(nb // n_seq,), in_specs=in_specs,
            out_specs=[per_b, pl.BlockSpec((n_seq, 2, LANES, WINDOW), lambda b, pt: (b, 0, 0, 0))]),
        out_shape=[jax.ShapeDtypeStruct((nb, SUBLANES, LANES), f32),
                   jax.ShapeDtypeStruct((nb, 2, LANES, WINDOW), f32)],
        compiler_params=_cparams(("parallel",)),
        name="nsa_sample",
    )(page_table, *([cache_slc] * (n_seq * N_PAGES)), kc, vc, win_state, qrows, grows, knew, *consts)


def _gla_out(o, gn_ref, gr):
    o = jnp.concatenate([_pair_rmsnorm(o[:, :LANES], gn_ref[:, :LANES]),
                         _pair_rmsnorm(o[:, LANES:], gn_ref[:, LANES:])], axis=1)
    return o * jax.nn.silu(gr)


def _gla_prompt_kernel(gq_ref, gk_ref, gl_ref, gv_ref, gr_ref, gn_ref, o_ref, st_o, st_ref):
    c_rows = GLA_CHUNK
    lane = lax.broadcasted_iota(jnp.int32, (c_rows, LANES), 1)
    row = lax.broadcasted_iota(jnp.int32, (c_rows, LANES), 0)
    tril = (lax.broadcasted_iota(jnp.int32, (c_rows, c_rows), 0)
            >= lax.broadcasted_iota(jnp.int32, (c_rows, c_rows), 1))
    lane_v = lax.broadcasted_iota(jnp.int32, (c_rows, GLA_HEADS * GLA_DV), 1)
    srow = lax.broadcasted_iota(jnp.int32, (GLA_HEADS * GLA_DV, LANES), 0)
    scol = lax.broadcasted_iota(jnp.int32, (GLA_HEADS * GLA_DV, LANES), 1)
    diag = (srow >> 6) == (scol >> 5)
    n_seqs, rows = gq_ref.shape[0], gq_ref.shape[1]

    @pl.when(pl.program_id(1) == 0)
    def _():
        st_ref[...] = jnp.zeros(st_ref.shape, f32)

    def body(c, carry):
        off = pl.multiple_of(c * c_rows, c_rows)
        results = [chunk(sq, off) for sq in range(n_seqs)]
        for sq, (st_new, out) in enumerate(results):
            st_ref[sq] = st_new
            o_ref[sq, pl.ds(off, c_rows), :] = out
        return carry

    def chunk(sq, off):
        q = gq_ref[sq, pl.ds(off, c_rows), :] * (GLA_DK ** -0.5)
        k = gk_ref[sq, pl.ds(off, c_rows), :]
        v = gv_ref[sq, pl.ds(off, c_rows), :]
        b = gl_ref[sq, pl.ds(off, c_rows), :]
        sh = 1
        while sh < c_rows:
            b = b + jnp.where(row >= sh, pltpu.roll(b, sh, 0), 0.0)
            sh *= 2
        qd = q * jnp.exp(b)
        kd = (k * jnp.exp(-b)).astype(bf16)
        bl = b[c_rows - 1:c_rows, :]
        kk = (k * jnp.exp(bl - b)).astype(bf16)
        st = st_ref[sq]
        vb = v.astype(bf16)
        o = _dot_nt(qd.astype(bf16), st.astype(bf16))
        for hh in range(GLA_HEADS):
            qm = jnp.where((lane >> 5) == hh, qd, 0.0).astype(bf16)
            a = jnp.where(tril, _dot_nt(qm, kd), 0.0)
            o = o + jnp.where((lane_v >> 6) == hh, _dot(a.astype(bf16), vb), 0.0)
        st_new = jnp.exp(bl) * st + jnp.where(diag, _dot_tn(vb, kk), 0.0)
        return st_new, _gla_out(o, gn_ref, gr_ref[sq, pl.ds(off, c_rows), :]).astype(bf16)

    lax.fori_loop(0, rows // c_rows, body, 0)
    st_o[...] = st_ref[...]


GLA_SEQS_PER_STEP = 8
GLA_ROWS_PER_STEP = 512


def _gla_prompt(gq, gk, gl, gv, gr, gn, nb, seq):
    ns, rows = GLA_SEQS_PER_STEP, GLA_ROWS_PER_STEP
    by_seq = lambda a: a.reshape(nb, seq, a.shape[-1])
    blk = lambda n: pl.BlockSpec((ns, rows, n), lambda b, r: (b, r, 0))
    o, st = pl.pallas_call(
        _gla_prompt_kernel,
        grid=(nb // ns, seq // rows),
        in_specs=[blk(LANES), blk(LANES), blk(LANES), blk(2 * LANES), blk(2 * LANES),
                  pl.BlockSpec(gn.shape, lambda b, r: (0, 0))],
        out_specs=[blk(2 * LANES), pl.BlockSpec((ns, GLA_HEADS * GLA_DV, LANES), lambda b, r: (b, 0, 0))],
        out_shape=[jax.ShapeDtypeStruct((nb, seq, 2 * LANES), bf16),
                   jax.ShapeDtypeStruct((nb, GLA_HEADS * GLA_DV, LANES), f32)],
        scratch_shapes=[pltpu.VMEM((ns, GLA_HEADS * GLA_DV, LANES), f32)],
        compiler_params=_cparams(("parallel", "arbitrary")),
        name="gla_prompt",
    )(by_seq(gq), by_seq(gk), by_seq(gl), by_seq(gv), by_seq(gr), gn)
    return o.reshape(nb * seq, 2 * LANES), st


def _gla_sample_kernel(cols_ref, v_ref, s0_ref, gr_ref, gn_ref, o_ref, s_o):
    nb = s0_ref.shape[0]
    c = cols_ref[...]
    q = c[:, :, 0:1] * (GLA_DK ** -0.5)
    k = c[:, :, 1:2]
    g = c[:, :, 2:3]
    eg = jnp.exp(g)
    qd = q * eg
    kd = k * jnp.exp(-g)
    s0 = s0_ref[...]
    v = v_ref[...]
    a = jnp.sum((qd * kd).reshape(nb, GLA_HEADS, GLA_DK, 1), axis=2)
    o = jnp.sum((qd * s0).reshape(nb, GLA_HEADS, GLA_DK, GLA_DV), axis=2) + a * v
    vexp = jnp.broadcast_to(v[:, :, None, :], (nb, GLA_HEADS, GLA_DK, GLA_DV)).reshape(nb, GLA_HEADS * GLA_DK, GLA_DV)
    s_o[...] = eg * s0 + k * vexp
    y = o * lax.rsqrt(jnp.mean(o * o, axis=-1, keepdims=True) + EPS) * gn_ref[...]
    o_ref[...] = y * jax.nn.silu(gr_ref[...])


def _gla_sample(cols, v, s0, gr, gn, bt=8):
    nb = s0.shape[0]
    blk = lambda a: pl.BlockSpec((bt,) + a.shape[1:], lambda i: (i,) + (0,) * (a.ndim - 1))
    return pl.pallas_call(
        _gla_sample_kernel,
        grid=(nb // bt,),
        in_specs=[blk(cols), blk(v), blk(s0), blk(gr), pl.BlockSpec(gn.shape, lambda i: (0, 0))],
        out_specs=[blk(v), blk(s0)],
        out_shape=[jax.ShapeDtypeStruct(v.shape, f32), jax.ShapeDtypeStruct(s0.shape, f32)],
        compiler_params=_cparams(("parallel",)),
        name="gla_sample",
    )(cols, v, s0, gr, gn)


def _pool_finish(sums, u, cnt_inv, w_ref, sc_ref):
    lane = lax.broadcasted_iota(jnp.int32, u.shape, 1)
    grp = lane >> 6
    s = jnp.where(grp == 0, sums[0], jnp.where(grp == 1, sums[1], jnp.where(grp == 2, sums[2], sums[3])))
    dlt = s * cnt_inv - u
    return _dot(dlt.astype(bf16), w_ref[...]) * sc_ref[...]


def _pool_prompt_kernel(u_ref, w_ref, sc_ref, o_ref):
    u = u_ref[...]
    row = lax.broadcasted_iota(jnp.int32, u.shape, 0)
    lane = lax.broadcasted_iota(jnp.int32, u.shape, 1)
    sums = []
    s = u
    sh = 1
    while sh < POOL_WINDOWS[-1]:
        s = s + jnp.where(row >= sh, pltpu.roll(s, sh, 0), 0.0)
        sums.append(s)
        sh *= 2
    grp = lane >> 6
    win = jnp.where(grp == 0, POOL_WINDOWS[0], jnp.where(grp == 1, POOL_WINDOWS[1],
                    jnp.where(grp == 2, POOL_WINDOWS[2], POOL_WINDOWS[3])))
    cnt = jnp.minimum(win, row + 1).astype(f32)
    o_ref[...] = _pool_finish(sums, u, 1.0 / cnt, w_ref, sc_ref).astype(bf16)


def _pool_prompt(pu, w, sc, nb, seq):
    blk = pl.BlockSpec((seq, 2 * LANES), lambda b: (b, 0))
    full = lambda a: pl.BlockSpec(a.shape, lambda b: (0,) * a.ndim)
    return pl.pallas_call(
        _pool_prompt_kernel,
        grid=(nb,),
        in_specs=[blk, full(w), full(sc)],
        out_specs=blk,
        out_shape=jax.ShapeDtypeStruct((nb * seq, 2 * LANES), bf16),
        compiler_params=_cparams(("parallel",)),
        name="pool_prompt",
    )(pu, w, sc)


def _pool_sample_kernel(u_ref, buf_ref, w_ref, sc_ref, o_ref):
    u = u_ref[...]
    lane = lax.broadcasted_iota(jnp.int32, u.shape, 1)
    sums = []
    s = u
    nxt = POOL_BUF - 1
    for win in POOL_WINDOWS:
        while POOL_BUF - nxt < win:
            s = s + buf_ref[nxt]
            nxt -= 1
        sums.append(s)
    grp = lane >> 6
    cnt_inv = jnp.where(grp == 0, 1.0 / POOL_WINDOWS[0], jnp.where(grp == 1, 1.0 / POOL_WINDOWS[1],
                        jnp.where(grp == 2, 1.0 / POOL_WINDOWS[2], 1.0 / POOL_WINDOWS[3])))
    o_ref[...] = _pool_finish(sums, u, cnt_inv, w_ref, sc_ref).astype(bf16)


def _pool_sample(pu, buf_t, w, sc):
    full = lambda a: pl.BlockSpec(a.shape, lambda i: (0,) * a.ndim)
    return pl.pallas_call(
        _pool_sample_kernel,
        grid=(1,),
        in_specs=[full(pu), full(buf_t), full(w), full(sc)],
        out_specs=full(pu),
        out_shape=jax.ShapeDtypeStruct(pu.shape, bf16),
        compiler_params=_cparams(("arbitrary",)),
        name="pool_sample",
    )(pu, buf_t, w, sc)


def _out_proj_kernel(a_ref, b_ref, c_ref, x_ref, w_ref, g_ref, x_o, h_o):
    y = (_dot(a_ref[...], w_ref[0:512, :]) + _dot(b_ref[...], w_ref[512:768, :])
         + _dot(c_ref[...], w_ref[768:1024, :]))
    x = x_ref[...] + y
    x_o[...] = x
    h_o[...] = (x * lax.rsqrt(jnp.mean(x * x, axis=-1, keepdims=True) + EPS) * g_ref[...]).astype(bf16)


def _out_proj(o_nsa, o_gla, o_pool, x, w, g, tm):
    m = x.shape[0]
    row = lambda n: pl.BlockSpec((tm, n), lambda i: (i, 0))
    full = lambda a: pl.BlockSpec(a.shape, lambda i: (0,) * a.ndim)
    return pl.pallas_call(
        _out_proj_kernel,
        grid=(m // tm,),
        in_specs=[row(512), row(256), row(256), row(D_MODEL), full(w), full(g)],
        out_specs=[row(D_MODEL), row(D_MODEL)],
        out_shape=[jax.ShapeDtypeStruct((m, D_MODEL), f32), jax.ShapeDtypeStruct((m, D_MODEL), bf16)],
        compiler_params=_cparams(("parallel",)),
        name="out_proj",
    )(o_nsa, o_gla, o_pool, x, w, g)


FFN_ROW_PARTS = 4
HALO = 16


def _ffn_kernel(tiles_per_seq, decode, h_ref, p_ref, x_ref, wa_ref, wu_ref, cw_ref, cb_ref, wd_ref, x_o, a_o):
    hn = h_ref[...]
    tm = hn.shape[0]
    if decode:
        a = _dot(hn, wa_ref[...])
        a2 = p_ref[0]
        a1 = p_ref[1]
        a_o[...] = a
        n_parts = 1
    else:
        first = (pl.program_id(0) % tiles_per_seq) == 0
        ax = _dot(jnp.concatenate([hn, p_ref[...]], axis=0), wa_ref[...])
        a = ax[:tm]
        ah = jnp.where(first, 0.0, ax[tm:])
        p1 = ah[HALO - 1:HALO, :]
        p2 = ah[HALO - 2:HALO - 1, :]
        row = lax.broadcasted_iota(jnp.int32, a.shape, 0)
        a1 = jnp.where(row == 0, p1, pltpu.roll(a, 1, 0))
        a2 = jnp.where(row == 0, p2, jnp.where(row == 1, p1, pltpu.roll(a, 2, 0)))
        a_o[...] = a[tm - SUBLANES:, :]
        n_parts = FFN_ROW_PARTS
    for part in range(n_parts):
        rs = slice(part * tm // n_parts, (part + 1) * tm // n_parts)
        u = _dot(hn[rs], wu_ref[...])
        ac = cb_ref[...] + a2[rs] * cw_ref[0:1, :] + a1[rs] * cw_ref[1:2, :] + a[rs] * cw_ref[2:3, :]
        act = (jax.nn.silu(ac) * u).astype(bf16)
        x_o[rs, :] = x_ref[rs, :] + _dot(act, wd_ref[...])


def _ffn(hn, prev, x, lw, tm, tiles_per_seq, decode):
    m = x.shape[0]
    nt = m // tm
    row = lambda n: pl.BlockSpec((tm, n), lambda i: (i, 0))
    const = lambda a: pl.BlockSpec(a.shape, lambda i: (0,) * a.ndim, pipeline_mode=pl.Buffered(1))
    if decode:
        prev_spec = pl.BlockSpec(prev.shape, lambda i: (0, 0, 0))
        a_spec = row(D_FF)
        a_shape = jax.ShapeDtypeStruct((m, D_FF), f32)
    else:
        per = tm // HALO
        prev_spec = pl.BlockSpec((HALO, D_MODEL), lambda i: (jnp.maximum(i * per - 1, 0), 0))
        a_spec = pl.BlockSpec((None, SUBLANES, D_FF), lambda i: (i, 0, 0))
        a_shape = jax.ShapeDtypeStruct((nt, SUBLANES, D_FF), f32)
    ws = (lw['w_up_a'], lw['w_up_u'], lw['conv_w'], lw['conv_b'], lw['w_down'])
    return pl.pallas_call(
        functools.partial(_ffn_kernel, tiles_per_seq, decode),
        grid=(nt,),
        in_specs=[row(D_MODEL), prev_spec, row(D_MODEL)] + [const(a) for a in ws],
        out_specs=[row(D_MODEL), a_spec],
        out_shape=[jax.ShapeDtypeStruct((m, D_MODEL), f32), a_shape],
        compiler_params=_cparams(("parallel",)),
        name="ffn_decode" if decode else "ffn_prompt",
    )(hn, prev, x, *ws)


def _rel_bucket_np(d):
    d = np.maximum(d, 0)
    exact = REL_BUCKETS // 2
    lg = np.log(np.maximum(d, 1).astype(np.float32) / np.float32(exact)) / np.float32(math.log(REL_MAX_DIST / exact))
    large = np.minimum(exact + (lg * np.float32(REL_BUCKETS - exact)).astype(np.int32), REL_BUCKETS - 1)
    return np.where(d < exact, d, large).astype(np.int32)


def _bias_lookup(rel_bias, dist):
    bucket = _rel_bucket_np(np.asarray(dist)).reshape(-1)
    onehot = jnp.asarray((bucket[:, None] == np.arange(REL_BUCKETS)[None, :]).astype(np.int8)).astype(f32)
    out = jnp.dot(onehot, rel_bias, precision=lax.Precision.HIGHEST)
    return out.reshape(tuple(np.shape(dist)) + (N_HEADS,))


def _bias_tables(rel_bias, seq, past_len):
    c = np.arange(QT)[:, None]
    r = np.arange(QT)[None, :]

    def key_major(t):
        rows = t.shape[0]
        return t.reshape(rows, QT, N_KV, GQA).transpose(2, 0, 3, 1).reshape(N_KV, rows, GQA * QT)

    def masked(t, keep):
        return jnp.where(jnp.asarray(np.tile(keep, (1, GQA)))[None], t, NEG)

    far = jnp.broadcast_to(rel_bias[REL_BUCKETS - 1].reshape(N_KV, 1, GQA, 1), (N_KV, QT, GQA, QT))
    far = far.reshape(N_KV, QT, GQA * QT)
    td = masked(key_major(_bias_lookup(rel_bias, r - c)) - far, c <= r)
    ts = key_major(_bias_lookup(rel_bias, QT + r - c)) - far
    we = masked(jnp.zeros_like(far), c > r)
    j = np.arange(2 * QT)[:, None]
    dist_c = r - (CMP_STRIDE * (j - QT) + CMP_LEN - 1)
    bc = masked(key_major(_bias_lookup(rel_bias, dist_c)), dist_c >= 0)
    cc = np.arange(LANES)[:, None]
    n = np.arange(LANES)[None, :]
    ov = ((cc * CMP_STRIDE < n * SEL_BLOCK + SEL_BLOCK) & (cc * CMP_STRIDE + CMP_LEN > n * SEL_BLOCK) & (cc < N_CMP))
    ovt = jnp.asarray((ov & (n < seq // SEL_BLOCK)).T, dtype=bf16)
    ov_s = jnp.asarray(ov & (n < past_len // SEL_BLOCK + 1), dtype=bf16)
    eet = jnp.asarray((np.arange(seq)[:, None] // SEL_BLOCK) == np.arange(LANES)[None, :], dtype=bf16)

    def look_rows(dist):
        return _bias_lookup(rel_bias, dist).T

    bc_s = look_rows(past_len - (np.arange(LANES) * CMP_STRIDE + CMP_LEN - 1))
    bs_s = look_rows(past_len - np.arange(past_len))
    bw_s = look_rows(WINDOW - np.arange(WINDOW))
    b0_s = look_rows(np.zeros((LANES,), np.int64))
    return dict(td=td, ts=ts, we=we, bc=bc, ovt=ovt, eet=eet,
                ov_s=ov_s, bc_s=bc_s, bs_s=bs_s, bw_s=bw_s, b0_s=b0_s)


def _pair_order(w, axis):
    shape = w.shape
    w = w.reshape(shape[:axis] + (N_KV, GQA, HEAD_DIM) + shape[axis + 1:])
    return jnp.swapaxes(w, axis, axis + 1).reshape(shape)


def _repack_w_in(w):
    z = lambda n: jnp.zeros((w.shape[0], n), w.dtype)
    return jnp.concatenate([
        _pair_order(w[:, :512], 1), w[:, 512:1280],
        w[:, 1280:1304], z(LANES - 24),
        w[:, 1304:2072],
        w[:, 2072:2088], z(LANES - GLA_RANK),
        w[:, 2088:2344]], axis=1)


def _block_diag(blocks):
    n, r, c = blocks.shape
    eye = jnp.eye(n, dtype=blocks.dtype)
    return (eye[:, None, :, None] * blocks[:, :, None, :]).reshape(n * r, n * c)


def _layer_weights(l, p):
    tile2 = lambda v: jnp.tile(v, 2)[None, :]
    w_in = _repack_w_in(p['w_in'][l]).astype(bf16)
    w_out = jnp.concatenate([_pair_order(p['w_out'][l][:512], 0), p['w_out'][l][512:]], axis=0).astype(bf16)
    k_gain = jnp.concatenate([jnp.tile(p['k_gain'][l], (1, 2)), jnp.zeros((SUBLANES - 3, LANES), f32)], axis=0)
    w_alpha = jnp.concatenate([p['gla_w_alpha'][l], jnp.zeros((LANES - GLA_RANK, LANES), f32)], axis=0).astype(bf16)

    def cmp_w(kv):
        w1 = p['cmp_w1'][l, kv]
        two = jax.vmap(lambda w: _block_diag(jnp.stack([w, w])))(w1)
        wcat = jnp.concatenate([two[:CMP_STRIDE], two[CMP_STRIDE:]], axis=-1).astype(bf16)
        pe = jnp.tile(p['cmp_pe'][l, kv], (1, 2))
        pe = jnp.concatenate([jnp.broadcast_to(pe[:CMP_STRIDE, None, :], (CMP_STRIDE, SUBLANES, LANES)),
                              jnp.broadcast_to(pe[CMP_STRIDE:, None, :], (CMP_STRIDE, SUBLANES, LANES))], axis=1)
        w2 = _block_diag(jnp.stack([p['cmp_w2'][l, kv]] * 2)).astype(bf16)
        wcat = wcat.reshape(CMP_STRIDE * LANES, 4 * LANES)
        pe = pe.transpose(1, 0, 2).reshape(2 * SUBLANES, CMP_STRIDE * LANES)
        return wcat, pe, w2

    wk, pek, w2k = cmp_w(0)
    wv, pev, w2v = cmp_w(1)
    w_up = p['w_ffn_up'][l].astype(bf16)
    conv_w = jnp.concatenate([p['ffn_conv_w'][l], jnp.zeros((SUBLANES - 3, D_FF), f32)], axis=0)
    return dict(
        g_mix=p['g_mix'][l][None, :], w_in=w_in, q_gain=tile2(p['q_gain'][l]), k_gain=k_gain,
        w_alpha=w_alpha, b_alpha=p['gla_b_alpha'][l][None, :],
        cmp_wk=wk, cmp_wv=wv, cmp_pek=pek, cmp_pev=pev, cmp_w2k=w2k, cmp_w2v=w2v,
        gla_norm=p['gla_norm'][l][None, :],
        pool_w=_block_diag(p['pool_w'][l]).astype(bf16), pool_scale=p['pool_scale'][l][None, :],
        w_out=w_out,
        g_ffn=p['g_ffn'][l][None, :],
        w_up_a=w_up[:, :D_FF], w_up_u=w_up[:, D_FF:], conv_w=conv_w, conv_b=p['ffn_conv_b'][l][None, :],
        w_down=p['w_ffn_down'][l].astype(bf16),
    )


def _prompt_layer(x, lw, tabs, nb, seq):
    tm = 512
    (q, ckv, skv, wkv, gt, gq, gk, gv, gr, gl, pu) = _in_proj(x, lw, tm)
    kc, vc = _compress_prompt(ckv, lw, nb, seq)
    o_nsa = _nsa_prompt(q, kc, vc, skv, wkv, gt, tabs, nb, seq)
    o_gla, st = _gla_prompt(gq, gk, gl, gv, gr, lw['gla_norm'], nb, seq)
    o_pool = _pool_prompt(pu, lw['pool_w'], lw['pool_scale'], nb, seq)
    x_mid, hn = _out_proj(o_nsa, o_gla, o_pool, x, lw['w_out'], lw['g_ffn'], tm)
    x_out, a_tail = _ffn(hn, hn, x_mid, lw, tm, seq // tm, False)
    win_tail = wkv.reshape(nb, seq, 2 * LANES)[:, seq - WINDOW:]
    st = st.reshape(nb, GLA_HEADS, GLA_DV, GLA_HEADS, GLA_DK)
    gla_state = jnp.stack([st[:, h, :, h, :] for h in range(GLA_HEADS)], axis=1).transpose(0, 1, 3, 2)
    pool_state = pu.reshape(nb, seq, 2 * LANES)[:, seq - POOL_BUF:]
    a_tail = a_tail.reshape(nb, seq // tm, SUBLANES, D_FF)
    conv_state = a_tail[:, -1, SUBLANES - 2:, :]
    return x_out, (ckv, skv, win_tail, gla_state, pool_state, conv_state)


def _sample_layer(l, x, lw, tabs, page_table, cache_cmp, cache_slc, win_all, state_gla, state_pool, state_conv):
    nb = x.shape[0]
    (q, ckv, skv, wkv, gt, gq, gk, gv, gr, gl, pu) = _in_proj(x, lw, nb)
    lane_hi = (np.arange(LANES) >= HEAD_DIM)
    own = jnp.asarray((lane_hi[None, :] == (np.arange(N_KV)[:, None] == 1)).astype(np.float32))
    qrows = (q.astype(f32).reshape(nb, 1, GQA, LANES) * own[None, :, None, :]).reshape(nb, N_HEADS, LANES)
    grows = jnp.pad(gt[:, :N_HEADS * 3].reshape(nb, N_HEADS, 3), ((0, 0), (0, 0), (0, LANES - 3)))
    knew = jnp.pad(jnp.concatenate([skv, wkv], axis=1).reshape(nb, 4, LANES), ((0, 0), (0, SUBLANES - 4), (0, 0)))
    o, win_next = _nsa_sample(l, page_table, cache_cmp, cache_slc, win_all, qrows, grows, knew, lw, tabs)
    o = o.reshape(nb, N_KV, GQA, N_KV, HEAD_DIM)
    o_nsa = jnp.stack([o[:, h, :, h, :] for h in range(N_KV)], axis=2).reshape(nb, GQA * LANES).astype(bf16)
    cols = jnp.stack([gq, gk, gl], axis=-1)
    o_gla, s_new = _gla_sample(cols, gv.reshape(nb, GLA_HEADS, GLA_DV),
                               state_gla[l].reshape(nb, GLA_HEADS * GLA_DK, GLA_DV),
                               gr.reshape(nb, GLA_HEADS, GLA_DV), lw['gla_norm'].reshape(GLA_HEADS, GLA_DV))
    o_gla = o_gla.reshape(nb, GLA_HEADS * GLA_DV).astype(bf16)
    o_pool = _pool_sample(pu, state_pool[l].transpose(1, 0, 2), lw['pool_w'], lw['pool_scale'])
    x_mid, hn = _out_proj(o_nsa, o_gla, o_pool, x, lw['w_out'], lw['g_ffn'], nb)
    x_out, a = _ffn(hn, state_conv[l].transpose(1, 0, 2), x_mid, lw, nb, 1, True)
    gla_state = s_new.reshape(nb, GLA_HEADS, GLA_DK, GLA_DV)
    return x_out, (ckv, skv, win_next, gla_state, pu, a)


def kernel(x_prompt, x_sample, cache_cmp_kv, cache_slc_kv, page_table, state_win_kv, state_gla, state_pool,
           state_ffn_conv, rel_bias, g_mix, w_in, q_gain, k_gain, cmp_pe, cmp_w1, cmp_w2, gla_w_alpha, gla_b_alpha,
           gla_norm, pool_w, pool_scale, w_out, g_ffn, w_ffn_up, ffn_conv_w, ffn_conv_b, w_ffn_down):
    nb, seq, _ = x_prompt.shape
    db = x_sample.shape[0]
    depth = w_in.shape[0]
    n_phys = cache_cmp_kv.shape[1]
    past_len = page_table.shape[1] * cache_cmp_kv.shape[2]
    assert (seq, past_len, page_table.shape[1], cache_cmp_kv.shape[2]) == (2048, 2048, N_PAGES, PAGE)
    assert state_win_kv.shape[2] == WINDOW and x_sample.shape[1] == 1
    params = dict(g_mix=g_mix, w_in=w_in, q_gain=q_gain, k_gain=k_gain, cmp_pe=cmp_pe, cmp_w1=cmp_w1, cmp_w2=cmp_w2,
                  gla_w_alpha=gla_w_alpha, gla_b_alpha=gla_b_alpha, gla_norm=gla_norm, pool_w=pool_w,
                  pool_scale=pool_scale, w_out=w_out, g_ffn=g_ffn, w_ffn_up=w_ffn_up, ffn_conv_w=ffn_conv_w,
                  ffn_conv_b=ffn_conv_b, w_ffn_down=w_ffn_down)
    tabs = _bias_tables(rel_bias, seq, past_len)
    cache_cmp = cache_cmp_kv.transpose(0, 1, 3, 4, 5, 2).reshape(depth, n_phys, 2, LANES, PAGE)
    cache_slc = cache_slc_kv.transpose(0, 1, 3, 4, 5, 2).reshape(depth, n_phys, 2, LANES, PAGE)
    win_all = state_win_kv.transpose(0, 1, 3, 4, 5, 2).reshape(depth, db, 2, LANES, WINDOW)
    xp = x_prompt.reshape(nb * seq, D_MODEL)
    xs = x_sample.reshape(db, D_MODEL)
    st_p = [[] for _ in range(6)]
    st_s = [[] for _ in range(6)]
    for l in range(depth):
        lw = _layer_weights(l, params)
        xp, new_p = _prompt_layer(xp, lw, tabs, nb, seq)
        xs, new_s = _sample_layer(l, xs, lw, tabs, page_table, cache_cmp, cache_slc, win_all, state_gla,
                                  state_pool, state_ffn_conv)
        for i in range(6):
            st_p[i].append(new_p[i])
            st_s[i].append(new_s[i])
    stk = lambda a: jnp.stack(a, axis=0)
    kv = lambda a, rows: a.reshape(depth, a.shape[1] // rows, rows, 2, N_KV, HEAD_DIM)
    outs_p = (kv(stk(st_p[0]), seq), kv(stk(st_p[1]), seq), kv(stk(st_p[2]).reshape(depth, nb * WINDOW, -1), WINDOW),
              stk(st_p[3]), stk(st_p[4]), stk(st_p[5]))
    shift_in = lambda old, new: jnp.concatenate([old[:, :, 1:], stk(new)[:, :, None, :]], axis=2)
    win_s = stk(st_s[2]).reshape(depth, db, 2, N_KV, HEAD_DIM, WINDOW).transpose(0, 1, 5, 2, 3, 4)
    outs_s = (kv(stk(st_s[0]), 1), kv(stk(st_s[1]), 1), win_s, stk(st_s[3]),
              shift_in(state_pool, st_s[4]), shift_in(state_ffn_conv, st_s[5]))
    return (xp.reshape(nb, seq, D_MODEL), xs.reshape(db, 1, D_MODEL), *outs_p, *outs_s)
```

```python
import functools
import math

import numpy as np
import jax
import jax.numpy as jnp
from jax import lax
from jax.experimental import pallas as pl
from jax.experimental.pallas import tpu as pltpu

f32 = jnp.float32
bf16 = jnp.bfloat16

D_MODEL = 1024
HEAD_DIM = 64
N_KV = 2
GQA = 4
N_HEADS = 8
CMP_STRIDE = 16
CMP_LEN = 32
N_CMP = 127
SEL_BLOCK = 64
N_SEL = 16
WINDOW = 512
ATT_SCALE = HEAD_DIM ** -0.5
REL_BUCKETS = 32
REL_MAX_DIST = 128
GLA_HEADS = 4
GLA_DK = 32
GLA_DV = 64
GLA_RANK = 16
GLA_TAU = 16.0
GLA_CHUNK = 64
POOL_WINDOWS = (2, 4, 8, 16)
POOL_BUF = 15
D_FF = 2816
EPS = 1e-6
NEG = -1e30
BIG = 1e9
PAGE = 128

LANES = 128
SUBLANES = 8
QT = 128
VMEM_LIMIT = 56 * 1024 * 1024


def _cparams(sem):
    return pltpu.CompilerParams(dimension_semantics=sem, vmem_limit_bytes=VMEM_LIMIT)


def _dot(a, b):
    return jnp.dot(a, b, preferred_element_type=f32)


def _dot_nt(a, b):
    return lax.dot_general(a, b, (((1,), (1,)), ((), ())), preferred_element_type=f32)


def _dot_tn(a, b):
    return lax.dot_general(a, b, (((0,), (0,)), ((), ())), preferred_element_type=f32)


def _dot_hilo(a, b_bf16):
    hi = a.astype(bf16)
    lo = (a - hi.astype(f32)).astype(bf16)
    return _dot(hi, b_bf16) + _dot(lo, b_bf16)


def _pair_rmsnorm(y, gain):
    lane = lax.broadcasted_iota(jnp.int32, y.shape, y.ndim - 1)
    lo = lane < HEAD_DIM
    y2 = y * y
    s_lo = jnp.sum(jnp.where(lo, y2, 0.0), axis=-1, keepdims=True)
    s_hi = jnp.sum(jnp.where(lo, 0.0, y2), axis=-1, keepdims=True)
    ms = jnp.where(lo, s_lo, s_hi) * (1.0 / HEAD_DIM)
    return y * lax.rsqrt(ms + EPS) * gain


_C_Q, _C_CK, _C_SK, _C_WK = 0, 512, 768, 1024
_C_GATE, _C_GK, _C_GV, _C_GR, _C_PU = 1280, 1536, 1664, 1920, 2304


def _proj_kernel(x_ref, g_ref, w_ref, qg_ref, kg_ref, wa_ref, ba_ref,
                 q_o, ckv_o, skv_o, wkv_o, gt_o, gq_o, gk_o, gv_o, gr_o, gl_o, pu_o, skv_b, wkv_b):
    x = x_ref[...]
    hn = (x * lax.rsqrt(jnp.mean(x * x, axis=-1, keepdims=True) + EPS) * g_ref[...]).astype(bf16)

    def proj(c0):
        y = _dot(hn, w_ref[:, c0:c0 + 2 * LANES])
        return y[:, :LANES], y[:, LANES:]

    for j in range(2):
        for half, y in enumerate(proj(_C_Q + 2 * j * LANES)):
            c = (2 * j + half) * LANES
            q_o[:, c:c + LANES] = _pair_rmsnorm(y, qg_ref[...]).astype(bf16)
    ckv_o[:, :LANES], ckv_o[:, LANES:] = proj(_C_CK)
    for c0, gain_row, o_f32, o_b in ((_C_SK, 1, skv_o, skv_b), (_C_WK, 2, wkv_o, wkv_b)):
        k, v = proj(c0)
        k = _pair_rmsnorm(k, kg_ref[gain_row:gain_row + 1, :])
        o_f32[:, :LANES], o_f32[:, LANES:] = k, v
        o_b[:, :LANES], o_b[:, LANES:] = k.astype(bf16), v.astype(bf16)
    gate, gq_o[...] = proj(_C_GATE)
    gt_o[...] = jax.nn.sigmoid(gate)
    gk_o[...], gv_o[:, :LANES] = proj(_C_GK)
    gv_o[:, LANES:], gr_o[:, :LANES] = proj(_C_GV + LANES)
    gr_o[:, LANES:], glr = proj(_C_GR + LANES)
    z = _dot(glr.astype(bf16), wa_ref[...]) + ba_ref[...]
    gl_o[...] = jax.nn.log_sigmoid(z) * (1.0 / GLA_TAU)
    pu_o[:, :LANES], pu_o[:, LANES:] = proj(_C_PU)


def _in_proj(x, lw, tm):
    m = x.shape[0]
    row = lambda n: pl.BlockSpec((tm, n), lambda i: (i, 0))
    full = lambda a: pl.BlockSpec(a.shape, lambda i: (0,) * a.ndim)
    widths = (512, 256, 256, 256, 128, 128, 128, 256, 256, 128, 256, 256, 256)
    dtypes = (bf16,) + (f32,) * 10 + (bf16, bf16)
    args = (x, lw['g_mix'], lw['w_in'], lw['q_gain'], lw['k_gain'], lw['w_alpha'], lw['b_alpha'])
    return pl.pallas_call(
        _proj_kernel,
        grid=(m // tm,),
        in_specs=[row(D_MODEL)] + [full(a) for a in args[1:]],
        out_specs=[row(n) for n in widths],
        out_shape=[jax.ShapeDtypeStruct((m, n), dt) for n, dt in zip(widths, dtypes)],
        compiler_params=_cparams(("parallel",)),
        name="in_proj",
    )(*args)


def _compress(get_x, w_ref, pe_ref, w2_ref):
    xs = jnp.concatenate([get_x(s).astype(bf16) for s in range(CMP_STRIDE)], axis=1)
    xs = jnp.concatenate([xs, pe_ref[...].astype(bf16)], axis=0)
    rows = xs.shape[0] - 2 * SUBLANES
    acc = _dot(xs, w_ref[...])
    a0 = acc[:rows, :2 * LANES]
    a1 = acc[:rows, 2 * LANES:]
    c0 = acc[rows:rows + 1, :2 * LANES] + acc[rows + SUBLANES:rows + SUBLANES + 1, 2 * LANES:]
    h = a0 + pltpu.roll(a1, rows - 1, 0) + c0
    return _dot(jax.nn.gelu(h).astype(bf16), w2_ref[...])


def _compress_prompt_kernel(ck_ref, cv_ref, wk_ref, wv_ref, pek_ref, pev_ref, w2k_ref, w2v_ref, kg_ref, kc_o, vc_o):
    kx = lambda s: ck_ref[pl.ds(s, LANES, stride=CMP_STRIDE), :]
    vx = lambda s: cv_ref[pl.ds(s, LANES, stride=CMP_STRIDE), :]
    kc_o[...] = _pair_rmsnorm(_compress(kx, wk_ref, pek_ref, w2k_ref), kg_ref[0:1, :])
    vc_o[...] = _compress(vx, wv_ref, pev_ref, w2v_ref)


def _compress_prompt(ckv, lw, nb, seq):
    full = lambda a: pl.BlockSpec(a.shape, lambda b: (0,) * a.ndim)
    kblk = pl.BlockSpec((seq, LANES), lambda b: (b, 0))
    vblk = pl.BlockSpec((seq, LANES), lambda b: (b, 1))
    outblk = pl.BlockSpec((None, LANES, LANES), lambda b: (b, 0, 0))
    ws = (lw['cmp_wk'], lw['cmp_wv'], lw['cmp_pek'], lw['cmp_pev'], lw['cmp_w2k'], lw['cmp_w2v'], lw['k_gain'])
    return pl.pallas_call(
        _compress_prompt_kernel,
        grid=(nb,),
        in_specs=[kblk, vblk] + [full(a) for a in ws],
        out_specs=[outblk, outblk],
        out_shape=[jax.ShapeDtypeStruct((nb, LANES, LANES), f32)] * 2,
        compiler_params=_cparams(("parallel",)),
        name="compress_prompt",
    )(ckv, ckv, *ws)


def _rank_select(score, n_blocks):
    blk = lax.broadcasted_iota(jnp.int32, score.shape, 1)
    rank = jnp.zeros(score.shape, f32)
    for m in range(n_blocks):
        sm = score[:, m:m + 1]
        beats = (sm > score) | ((sm == score) & (m < blk))
        rank = rank + jnp.where(beats, 1.0, 0.0)
    return jnp.where(rank < float(N_SEL), 1.0, 0.0)


def _rank_select_rows(score, n_blocks):
    blk = lax.broadcasted_iota(jnp.int32, score.shape, 0)
    rank = jnp.zeros(score.shape, f32)
    for m in range(n_blocks):
        sm = score[m:m + 1, :]
        beats = (sm > score) | ((sm == score) & (m < blk))
        rank = rank + jnp.where(beats, 1.0, 0.0)
    return jnp.where(rank < float(N_SEL), 1.0, 0.0)


def _nsa_prompt_kernel(q_ref, kc_ref, vc_ref, sk_ref, sv_ref, wk_ref, wv_ref, gt_ref,
                       td_ref, ts_ref, we_ref, bc_ref, ovt_ref, eet_ref, o_ref,
                       selx_ref, m_ref, l_ref, acc_ref):
    i = pl.program_id(1)
    lane = lax.broadcasted_iota(jnp.int32, (QT, LANES), 1)
    row = lax.broadcasted_iota(jnp.int32, (QT, LANES), 0)
    n_blk = selx_ref.shape[1] // SEL_BLOCK
    blk = lax.broadcasted_iota(jnp.int32, (n_blk, QT), 0)
    cur = (i * QT + lax.broadcasted_iota(jnp.int32, (n_blk, QT), 1)) >> 6
    gates_t = gt_ref[...].T
    outs = [[None, None] for _ in range(GQA)]

    def tile4(x):
        return jnp.concatenate([x] * GQA, axis=1)

    def scores(q4, k_ref, kt, bias, sel_head=None, live=None):
        off = pl.multiple_of(kt * QT, QT)
        s = _dot_nt(k_ref[pl.ds(off, QT), :].astype(bf16), q4)
        if bias is not None:
            s = s + bias
        if live is not None:
            s = s + jnp.where(live, 0.0, NEG)
        if sel_head is not None:
            s = s + tile4(selx_ref[sel_head, pl.ds(off, QT), :])
        return s, off

    def attend(v_ref, tiles, state=None):
        m = state[0] if state is not None else None
        for s, _ in tiles:
            ms = jnp.max(s, axis=0, keepdims=True)
            m = ms if m is None else jnp.maximum(m, ms)
        p = jnp.concatenate([jnp.exp(s - m) for s, _ in tiles], axis=0)
        v = jnp.concatenate([v_ref[pl.ds(off, QT), :] for _, off in tiles], axis=0)
        l = jnp.sum(p, axis=0, keepdims=True)
        acc = _dot_tn(v.astype(bf16), p.astype(bf16))
        if state is not None:
            alpha = jnp.exp(state[0] - m)
            l = l + alpha * state[1]
            acc = acc + alpha * state[2]
        return m, l, acc

    m_ref[...] = jnp.full(m_ref.shape, NEG, f32)
    l_ref[...] = jnp.zeros(l_ref.shape, f32)
    acc_ref[...] = jnp.zeros(acc_ref.shape, f32)

    q4s, o_cmps = [], []
    for h in range(N_KV):
        half = (lane >= HEAD_DIM) if h else (lane < HEAD_DIM)
        q4 = jnp.concatenate(
            [jnp.where(half, q_ref[:, g * LANES:(g + 1) * LANES] * jnp.asarray(ATT_SCALE, bf16), jnp.zeros((), bf16))
             for g in range(GQA)], axis=0)
        q4s.append(q4)
        bias_c = bc_ref[h, pl.ds(pl.multiple_of(LANES - SUBLANES * i, SUBLANES), LANES), :]
        s = _dot_nt(kc_ref[...].astype(bf16), q4) + bias_c
        e = jnp.exp(s - jnp.max(s, axis=0, keepdims=True))
        p = jnp.where(s > 0.5 * NEG, e / jnp.sum(e, axis=0, keepdims=True), 0.0)
        o_cmps.append(_dot_tn(vc_ref[...].astype(bf16), p.astype(bf16)))
        psum = p[:, 0:QT] + p[:, QT:2 * QT] + p[:, 2 * QT:3 * QT] + p[:, 3 * QT:4 * QT]
        hi = psum.astype(bf16)
        lo = (psum - hi.astype(f32)).astype(bf16)
        imp = (_dot(ovt_ref[...], hi) + _dot(ovt_ref[...], lo))[0:n_blk, :]
        forced = (blk == 0) | (blk == cur) | (blk == cur - 1)
        score = jnp.where(forced, BIG, jnp.where(blk <= cur, imp, -BIG))
        sel = _rank_select_rows(score, n_blk)
        neg = jnp.concatenate([(sel - 1.0) * (-NEG), jnp.zeros((LANES - n_blk, QT), f32)], axis=0).astype(bf16)
        selx_ref[h] = _dot(eet_ref[...], neg)

    n_far = jnp.maximum(i - 1, 0)

    def far_pairs(j, carry):
        kt1 = 2 * j + 1
        tiles = [[scores(q4s[h], sk_ref, 2 * j, None, sel_head=h),
                  scores(q4s[h], sk_ref, jnp.minimum(kt1, i), None, sel_head=h, live=kt1 < n_far)]
                 for h in range(N_KV)]
        for h in range(N_KV):
            m_ref[h], l_ref[h], acc_ref[h] = attend(sv_ref, tiles[h], (m_ref[h], l_ref[h], acc_ref[h]))
        return carry

    lax.fori_loop(0, (n_far + 1) // 2, far_pairs, 0)

    def back_tile(back):
        return dict(kt=jnp.maximum(i - back, 0), live=(i >= back) if back else None)

    slc_tiles = [[scores(q4s[h], sk_ref, bias=tab_ref[h], sel_head=h, **back_tile(back))
                  for back, tab_ref in ((1, ts_ref), (0, td_ref))] for h in range(N_KV)]
    win_tiles = [[scores(q4s[h], wk_ref, bias=bias, **back_tile(back))
                  for back, bias in ((4, we_ref[h]), (3, None), (2, None), (1, ts_ref[h]), (0, td_ref[h]))]
                 for h in range(N_KV)]
    for h in range(N_KV):
        _, l_s, acc_s = attend(sv_ref, slc_tiles[h], (m_ref[h], l_ref[h], acc_ref[h]))
        _, l_w, acc_w = attend(wv_ref, win_tiles[h])
        o_slc = acc_s / l_s
        o_win = acc_w / l_w
        for g in range(GQA):
            c = (h * GQA + g) * 3
            sl = slice(g * QT, (g + 1) * QT)
            outs[g][h] = (gates_t[c:c + 1, :] * o_cmps[h][:, sl] + gates_t[c + 1:c + 2, :] * o_slc[:, sl]
                          + gates_t[c + 2:c + 3, :] * o_win[:, sl])
    for g in range(GQA):
        o_ref[:, g * LANES:(g + 1) * LANES] = jnp.where(row < HEAD_DIM, outs[g][0], outs[g][1]).T.astype(bf16)


def _nsa_prompt(q, kc, vc, skv, wkv, gt, tabs, nb, seq):
    nt = seq // QT
    tile = lambda n: pl.BlockSpec((QT, n), lambda b, i: (b * nt + i, 0))
    kblk = pl.BlockSpec((seq, LANES), lambda b, i: (b, 0))
    vblk = pl.BlockSpec((seq, LANES), lambda b, i: (b, 1))
    cblk = pl.BlockSpec((None, LANES, LANES), lambda b, i: (b, 0, 0))
    full = lambda a: pl.BlockSpec(a.shape, lambda b, i: (0,) * a.ndim)
    consts = (tabs['td'], tabs['ts'], tabs['we'], tabs['bc'], tabs['ovt'], tabs['eet'])
    return pl.pallas_call(
        _nsa_prompt_kernel,
        grid=(nb, nt),
        in_specs=[tile(512), cblk, cblk, kblk, vblk, kblk, vblk, tile(LANES)] + [full(a) for a in consts],
        out_specs=tile(512),
        out_shape=jax.ShapeDtypeStruct((nb * seq, 512), bf16),
        scratch_shapes=[pltpu.VMEM((N_KV, seq, QT), f32), pltpu.VMEM((N_KV, 1, GQA * QT), f32),
                        pltpu.VMEM((N_KV, 1, GQA * QT), f32), pltpu.VMEM((N_KV, LANES, GQA * QT), f32)],
        compiler_params=_cparams(("parallel", "arbitrary")),
        name="nsa_prompt",
    )(q, kc, vc, skv, skv, wkv, wkv, gt, *consts)


N_PAGES = 16
SAMPLE_SEQS_PER_STEP = 4
COMPRESS_SEQS_PER_STEP = 4


def _compress_sample_kernel(n_seq, pt_ref, *refs):
    pages = refs[:n_seq * N_PAGES]
    (wcat_k, wcat_v, pek_ref, pev_ref, w2k_ref, w2v_ref, kg_ref, kc_o, vc_o, hk_ref, hv_ref) = refs[n_seq * N_PAGES:]

    def hist(kind, h_ref):
        for n, pg in enumerate(pages):
            h_ref[pl.ds(n * PAGE, PAGE), :] = pg[kind].T
        return lambda s: h_ref[pl.ds(s, n_seq * LANES, stride=CMP_STRIDE), :]

    kc_o[...] = _pair_rmsnorm(_compress(hist(0, hk_ref), wcat_k, pek_ref, w2k_ref), kg_ref[0:1, :])
    vc_o[...] = _compress(hist(1, hv_ref), wcat_v, pev_ref, w2v_ref)


def _nsa_sample_kernel(n_seq, pt_ref, *refs):
    seq_pages = [refs[t * N_PAGES:(t + 1) * N_PAGES] for t in range(n_seq)]
    (kc_ref, vc_ref, win_ref, qr_ref, gr_ref, knew_ref,
     bc_ref, bs_ref, bw_ref, b0_ref, ov_ref, o_ref, win_o) = refs[n_seq * N_PAGES:]
    for t in range(n_seq):
        rows = slice(t * LANES, (t + 1) * LANES)
        _nsa_sample_one(t, kc_ref[rows, :], vc_ref[rows, :], seq_pages[t], win_ref, qr_ref, gr_ref,
                        knew_ref, bc_ref, bs_ref, bw_ref, b0_ref, ov_ref, o_ref, win_o)


def _nsa_sample_one(t, kc, vc, slc_pages, win_ref, qr_ref, gr_ref, knew_ref, bc_ref, bs_ref, bw_ref, b0_ref,
                    ov_ref, o_ref, win_o):
    lane = lax.broadcasted_iota(jnp.int32, (SUBLANES, LANES), 1)
    qf = qr_ref[t]
    qb = qf.astype(bf16)
    gates = gr_ref[t]
    knew = knew_ref[t]
    hi_rows = lax.broadcasted_iota(jnp.int32, (SUBLANES, LANES), 0) >= GQA
    own = (lane >= HEAD_DIM) == hi_rows

    s = _dot_nt(qb, kc.astype(bf16)) * ATT_SCALE + bc_ref[...]
    mskc = lane < N_CMP
    s = jnp.where(mskc, s, NEG)
    e = jnp.exp(s - jnp.max(s, axis=-1, keepdims=True))
    p = jnp.where(mskc, e / jnp.sum(e, axis=-1, keepdims=True), 0.0)
    o_cmp = _dot(p.astype(bf16), vc.astype(bf16))
    psum = jnp.broadcast_to(jnp.sum(p.reshape(N_KV, GQA, LANES), axis=1, keepdims=True), (N_KV, GQA, LANES))
    imp = _dot_hilo(psum.reshape(SUBLANES, LANES), ov_ref[...])
    n_blk = N_PAGES * PAGE // SEL_BLOCK + 1
    cur = n_blk - 1
    forced = (lane == 0) | (lane == cur) | (lane == cur - 1)
    score = jnp.where(forced, BIG, imp)
    score = jnp.where(lane < n_blk, score, -3e38)
    sel = _rank_select(score, n_blk)

    def softmax_av(s_parts, s_new, vt_parts, v_new):
        mx = s_new
        for sp in s_parts:
            mx = jnp.maximum(mx, jnp.max(sp, axis=-1, keepdims=True))
        p_new = jnp.exp(s_new - mx)
        den = p_new
        acc = p_new * v_new
        for sp, vt in zip(s_parts, vt_parts):
            pp = jnp.exp(sp - mx)
            den = den + jnp.sum(pp, axis=-1, keepdims=True)
            acc = acc + _dot_nt(pp.astype(bf16), vt.astype(bf16))
        return acc / den

    def new_row(r):
        return knew[r:r + 1, :]

    def score_new(krow):
        return jnp.sum(jnp.where(own, qf * krow, 0.0), axis=-1, keepdims=True) * ATT_SCALE + b0_ref[:, 0:1]

    s_parts = []
    for j in range(N_PAGES):
        sj = _dot(qb, slc_pages[j][0].astype(bf16)) * ATT_SCALE + bs_ref[:, j * PAGE:(j + 1) * PAGE]
        mj = jnp.where(lane < SEL_BLOCK, sel[:, 2 * j:2 * j + 1], sel[:, 2 * j + 1:2 * j + 2]) > 0.5
        s_parts.append(jnp.where(mj, sj, NEG))
    o_slc = softmax_av(s_parts, score_new(new_row(0)), [pg[1] for pg in slc_pages], new_row(1))

    w_parts = []
    for j in range(WINDOW // PAGE):
        kw = win_ref[t, 0, :, j * PAGE:(j + 1) * PAGE]
        sj = _dot(qb, kw.astype(bf16)) * ATT_SCALE + bw_ref[:, j * PAGE:(j + 1) * PAGE]
        if j == 0:
            sj = jnp.where(lane >= 1, sj, NEG)
        w_parts.append(sj)
    vt_parts = [win_ref[t, 1, :, j * PAGE:(j + 1) * PAGE] for j in range(WINDOW // PAGE)]
    o_win = softmax_av(w_parts, score_new(new_row(2)), vt_parts, new_row(3))

    o_ref[t] = gates[:, 0:1] * o_cmp + gates[:, 1:2] * o_slc + gates[:, 2:3] * o_win

    new_cols = knew.T
    last = lax.broadcasted_iota(jnp.int32, (LANES, WINDOW), 1) == WINDOW - 1
    for kv in range(2):
        win_o[t, kv] = jnp.where(last, new_cols[:, 2 + kv:3 + kv], pltpu.roll(win_ref[t, kv], WINDOW - 1, 1))


def _nsa_sample(layer, page_table, cache_cmp, cache_slc, win_state, qrows, grows, knew, lw, tabs):
    nb = page_table.shape[0]
    full = lambda a: pl.BlockSpec(a.shape, lambda b, pt: (0,) * a.ndim)

    def page_specs(n_seq):
        return [pl.BlockSpec((None, None, 2, LANES, PAGE),
                             functools.partial(lambda t, j, b, pt: (layer, pt[b * n_seq + t, j], 0, 0, 0), t, j))
                for t in range(n_seq) for j in range(N_PAGES)]

    n_seq = COMPRESS_SEQS_PER_STEP
    ws = (lw['cmp_wk'], lw['cmp_wv'], lw['cmp_pek'], lw['cmp_pev'], lw['cmp_w2k'], lw['cmp_w2v'], lw['k_gain'])
    cblk = pl.BlockSpec((n_seq * LANES, LANES), lambda b, pt: (b, 0))
    kc, vc = pl.pallas_call(
        functools.partial(_compress_sample_kernel, n_seq),
        grid_spec=pltpu.PrefetchScalarGridSpec(
            num_scalar_prefetch=1, grid=(nb // n_seq,),
            in_specs=page_specs(n_seq) + [full(a) for a in ws], out_specs=[cblk, cblk],
            scratch_shapes=[pltpu.VMEM((n_seq * N_PAGES * PAGE, LANES), f32)] * 2),
        out_shape=[jax.ShapeDtypeStruct((nb * LANES, LANES), f32)] * 2,
        compiler_params=_cparams(("parallel",)),
        name="compress_sample",
    )(page_table, *([cache_cmp] * (n_seq * N_PAGES)), *ws)

    n_seq = SAMPLE_SEQS_PER_STEP
    per_b = pl.BlockSpec((n_seq, SUBLANES, LANES), lambda b, pt: (b, 0, 0))
    cblk = pl.BlockSpec((n_seq * LANES, LANES), lambda b, pt: (b, 0))
    consts = (tabs['bc_s'], tabs['bs_s'], tabs['bw_s'], tabs['b0_s'], tabs['ov_s'])
    in_specs = (page_specs(n_seq) + [cblk, cblk]
                + [pl.BlockSpec((None, n_seq, 2, LANES, WINDOW), lambda b, pt: (layer, b, 0, 0, 0)), per_b, per_b, per_b]
                + [full(a) for a in consts])
    return pl.pallas_call(
        functools.partial(_nsa_sample_kernel, n_seq),
        grid_spec=pltpu.PrefetchScalarGridSpec(
            num_scalar_prefetch=1, grid=(nb // n_seq,), in_specs=in_specs,
            out_specs=[per_b, pl.BlockSpec((n_seq, 2, LANES, WINDOW), lambda b, pt: (b, 0, 0, 0))]),
        out_shape=[jax.ShapeDtypeStruct((nb, SUBLANES, LANES), f32),
                   jax.ShapeDtypeStruct((nb, 2, LANES, WINDOW), f32)],
        compiler_params=_cparams(("parallel",)),
        name="nsa_sample",
    )(page_table, *([cache_slc] * (n_seq * N_PAGES)), kc, vc, win_state, qrows, grows, knew, *consts)


def _gla_out(o, gn_ref, gr):
    o = jnp.concatenate([_pair_rmsnorm(o[:, :LANES], gn_ref[:, :LANES]),
                         _pair_rmsnorm(o[:, LANES:], gn_ref[:, LANES:])], axis=1)
    return o * jax.nn.silu(gr)


def _gla_prompt_kernel(gq_ref, gk_ref, gl_ref, gv_ref, gr_ref, gn_ref, o_ref, st_o, st_ref):
    c_rows = GLA_CHUNK
    lane = lax.broadcasted_iota(jnp.int32, (c_rows, LANES), 1)
    row = lax.broadcasted_iota(jnp.int32, (c_rows, LANES), 0)
    tril = (lax.broadcasted_iota(jnp.int32, (c_rows, c_rows), 0)
            >= lax.broadcasted_iota(jnp.int32, (c_rows, c_rows), 1))
    lane_v = lax.broadcasted_iota(jnp.int32, (c_rows, GLA_HEADS * GLA_DV), 1)
    srow = lax.broadcasted_iota(jnp.int32, (GLA_HEADS * GLA_DV, LANES), 0)
    scol = lax.broadcasted_iota(jnp.int32, (GLA_HEADS * GLA_DV, LANES), 1)
    diag = (srow >> 6) == (scol >> 5)
    n_seqs, rows = gq_ref.shape[0], gq_ref.shape[1]

    @pl.when(pl.program_id(1) == 0)
    def _():
        st_ref[...] = jnp.zeros(st_ref.shape, f32)

    def body(c, carry):
        off = pl.multiple_of(c * c_rows, c_rows)
        results = [chunk(sq, off) for sq in range(n_seqs)]
        for sq, (st_new, out) in enumerate(results):
            st_ref[sq] = st_new
            o_ref[sq, pl.ds(off, c_rows), :] = out
        return carry

    def chunk(sq, off):
        q = gq_ref[sq, pl.ds(off, c_rows), :] * (GLA_DK ** -0.5)
        k = gk_ref[sq, pl.ds(off, c_rows), :]
        v = gv_ref[sq, pl.ds(off, c_rows), :]
        b = gl_ref[sq, pl.ds(off, c_rows), :]
        sh = 1
        while sh < c_rows:
            b = b + jnp.where(row >= sh, pltpu.roll(b, sh, 0), 0.0)
            sh *= 2
        qd = q * jnp.exp(b)
        kd = (k * jnp.exp(-b)).astype(bf16)
        bl = b[c_rows - 1:c_rows, :]
        kk = (k * jnp.exp(bl - b)).astype(bf16)
        st = st_ref[sq]
        vb = v.astype(bf16)
        o = _dot_nt(qd.astype(bf16), st.astype(bf16))
        for hh in range(GLA_HEADS):
            qm = jnp.where((lane >> 5) == hh, qd, 0.0).astype(bf16)
            a = jnp.where(tril, _dot_nt(qm, kd), 0.0)
            o = o + jnp.where((lane_v >> 6) == hh, _dot(a.astype(bf16), vb), 0.0)
        st_new = jnp.exp(bl) * st + jnp.where(diag, _dot_tn(vb, kk), 0.0)
        return st_new, _gla_out(o, gn_ref, gr_ref[sq, pl.ds(off, c_rows), :]).astype(bf16)

    lax.fori_loop(0, rows // c_rows, body, 0)
    st_o[...] = st_ref[...]


GLA_SEQS_PER_STEP = 8
GLA_ROWS_PER_STEP = 512


def _gla_prompt(gq, gk, gl, gv, gr, gn, nb, seq):
    ns, rows = GLA_SEQS_PER_STEP, GLA_ROWS_PER_STEP
    by_seq = lambda a: a.reshape(nb, seq, a.shape[-1])
    blk = lambda n: pl.BlockSpec((ns, rows, n), lambda b, r: (b, r, 0))
    o, st = pl.pallas_call(
        _gla_prompt_kernel,
        grid=(nb // ns, seq // rows),
        in_specs=[blk(LANES), blk(LANES), blk(LANES), blk(2 * LANES), blk(2 * LANES),
                  pl.BlockSpec(gn.shape, lambda b, r: (0, 0))],
        out_specs=[blk(2 * LANES), pl.BlockSpec((ns, GLA_HEADS * GLA_DV, LANES), lambda b, r: (b, 0, 0))],
        out_shape=[jax.ShapeDtypeStruct((nb, seq, 2 * LANES), bf16),
                   jax.ShapeDtypeStruct((nb, GLA_HEADS * GLA_DV, LANES), f32)],
        scratch_shapes=[pltpu.VMEM((ns, GLA_HEADS * GLA_DV, LANES), f32)],
        compiler_params=_cparams(("parallel", "arbitrary")),
        name="gla_prompt",
    )(by_seq(gq), by_seq(gk), by_seq(gl), by_seq(gv), by_seq(gr), gn)
    return o.reshape(nb * seq, 2 * LANES), st


def _gla_sample_kernel(cols_ref, v_ref, s0_ref, gr_ref, gn_ref, o_ref, s_o):
    nb = s0_ref.shape[0]
    c = cols_ref[...]
    q = c[:, :, 0:1] * (GLA_DK ** -0.5)
    k = c[:, :, 1:2]
    g = c[:, :, 2:3]
    eg = jnp.exp(g)
    qd = q * eg
    kd = k * jnp.exp(-g)
    s0 = s0_ref[...]
    v = v_ref[...]
    a = jnp.sum((qd * kd).reshape(nb, GLA_HEADS, GLA_DK, 1), axis=2)
    o = jnp.sum((qd * s0).reshape(nb, GLA_HEADS, GLA_DK, GLA_DV), axis=2) + a * v
    vexp = jnp.broadcast_to(v[:, :, None, :], (nb, GLA_HEADS, GLA_DK, GLA_DV)).reshape(nb, GLA_HEADS * GLA_DK, GLA_DV)
    s_o[...] = eg * s0 + k * vexp
    y = o * lax.rsqrt(jnp.mean(o * o, axis=-1, keepdims=True) + EPS) * gn_ref[...]
    o_ref[...] = y * jax.nn.silu(gr_ref[...])


def _gla_sample(cols, v, s0, gr, gn, bt=8):
    nb = s0.shape[0]
    blk = lambda a: pl.BlockSpec((bt,) + a.shape[1:], lambda i: (i,) + (0,) * (a.ndim - 1))
    return pl.pallas_call(
        _gla_sample_kernel,
        grid=(nb // bt,),
        in_specs=[blk(cols), blk(v), blk(s0), blk(gr), pl.BlockSpec(gn.shape, lambda i: (0, 0))],
        out_specs=[blk(v), blk(s0)],
        out_shape=[jax.ShapeDtypeStruct(v.shape, f32), jax.ShapeDtypeStruct(s0.shape, f32)],
        compiler_params=_cparams(("parallel",)),
        name="gla_sample",
    )(cols, v, s0, gr, gn)


def _pool_finish(sums, u, cnt_inv, w_ref, sc_ref):
    lane = lax.broadcasted_iota(jnp.int32, u.shape, 1)
    grp = lane >> 6
    s = jnp.where(grp == 0, sums[0], jnp.where(grp == 1, sums[1], jnp.where(grp == 2, sums[2], sums[3])))
    dlt = s * cnt_inv - u
    return _dot(dlt.astype(bf16), w_ref[...]) * sc_ref[...]


def _pool_prompt_kernel(u_ref, w_ref, sc_ref, o_ref):
    u = u_ref[...]
    row = lax.broadcasted_iota(jnp.int32, u.shape, 0)
    lane = lax.broadcasted_iota(jnp.int32, u.shape, 1)
    sums = []
    s = u
    sh = 1
    while sh < POOL_WINDOWS[-1]:
        s = s + jnp.where(row >= sh, pltpu.roll(s, sh, 0), 0.0)
        sums.append(s)
        sh *= 2
    grp = lane >> 6
    win = jnp.where(grp == 0, POOL_WINDOWS[0], jnp.where(grp == 1, POOL_WINDOWS[1],
                    jnp.where(grp == 2, POOL_WINDOWS[2], POOL_WINDOWS[3])))
    cnt = jnp.minimum(win, row + 1).astype(f32)
    o_ref[...] = _pool_finish(sums, u, 1.0 / cnt, w_ref, sc_ref).astype(bf16)


def _pool_prompt(pu, w, sc, nb, seq):
    blk = pl.BlockSpec((seq, 2 * LANES), lambda b: (b, 0))
    full = lambda a: pl.BlockSpec(a.shape, lambda b: (0,) * a.ndim)
    return pl.pallas_call(
        _pool_prompt_kernel,
        grid=(nb,),
        in_specs=[blk, full(w), full(sc)],
        out_specs=blk,
        out_shape=jax.ShapeDtypeStruct((nb * seq, 2 * LANES), bf16),
        compiler_params=_cparams(("parallel",)),
        name="pool_prompt",
    )(pu, w, sc)


def _pool_sample_kernel(u_ref, buf_ref, w_ref, sc_ref, o_ref):
    u = u_ref[...]
    lane = lax.broadcasted_iota(jnp.int32, u.shape, 1)
    sums = []
    s = u
    nxt = POOL_BUF - 1
    for win in POOL_WINDOWS:
        while POOL_BUF - nxt < win:
            s = s + buf_ref[nxt]
            nxt -= 1
        sums.append(s)
    grp = lane >> 6
    cnt_inv = jnp.where(grp == 0, 1.0 / POOL_WINDOWS[0], jnp.where(grp == 1, 1.0 / POOL_WINDOWS[1],
                        jnp.where(grp == 2, 1.0 / POOL_WINDOWS[2], 1.0 / POOL_WINDOWS[3])))
    o_ref[...] = _pool_finish(sums, u, cnt_inv, w_ref, sc_ref).astype(bf16)


def _pool_sample(pu, buf_t, w, sc):
    full = lambda a: pl.BlockSpec(a.shape, lambda i: (0,) * a.ndim)
    return pl.pallas_call(
        _pool_sample_kernel,
        grid=(1,),
        in_specs=[full(pu), full(buf_t), full(w), full(sc)],
        out_specs=full(pu),
        out_shape=jax.ShapeDtypeStruct(pu.shape, bf16),
        compiler_params=_cparams(("arbitrary",)),
        name="pool_sample",
    )(pu, buf_t, w, sc)


def _out_proj_kernel(a_ref, b_ref, c_ref, x_ref, w_ref, g_ref, x_o, h_o):
    y = (_dot(a_ref[...], w_ref[0:512, :]) + _dot(b_ref[...], w_ref[512:768, :])
         + _dot(c_ref[...], w_ref[768:1024, :]))
    x = x_ref[...] + y
    x_o[...] = x
    h_o[...] = (x * lax.rsqrt(jnp.mean(x * x, axis=-1, keepdims=True) + EPS) * g_ref[...]).astype(bf16)


def _out_proj(o_nsa, o_gla, o_pool, x, w, g, tm):
    m = x.shape[0]
    row = lambda n: pl.BlockSpec((tm, n), lambda i: (i, 0))
    full = lambda a: pl.BlockSpec(a.shape, lambda i: (0,) * a.ndim)
    return pl.pallas_call(
        _out_proj_kernel,
        grid=(m // tm,),
        in_specs=[row(512), row(256), row(256), row(D_MODEL), full(w), full(g)],
        out_specs=[row(D_MODEL), row(D_MODEL)],
        out_shape=[jax.ShapeDtypeStruct((m, D_MODEL), f32), jax.ShapeDtypeStruct((m, D_MODEL), bf16)],
        compiler_params=_cparams(("parallel",)),
        name="out_proj",
    )(o_nsa, o_gla, o_pool, x, w, g)


FFN_ROW_PARTS = 4
HALO = 16


def _ffn_kernel(tiles_per_seq, decode, h_ref, p_ref, x_ref, wa_ref, wu_ref, cw_ref, cb_ref, wd_ref, x_o, a_o):
    hn = h_ref[...]
    tm = hn.shape[0]
    if decode:
        a = _dot(hn, wa_ref[...])
        a2 = p_ref[0]
        a1 = p_ref[1]
        a_o[...] = a
        n_parts = 1
    else:
        first = (pl.program_id(0) % tiles_per_seq) == 0
        ax = _dot(jnp.concatenate([hn, p_ref[...]], axis=0), wa_ref[...])
        a = ax[:tm]
        ah = jnp.where(first, 0.0, ax[tm:])
        p1 = ah[HALO - 1:HALO, :]
        p2 = ah[HALO - 2:HALO - 1, :]
        row = lax.broadcasted_iota(jnp.int32, a.shape, 0)
        a1 = jnp.where(row == 0, p1, pltpu.roll(a, 1, 0))
        a2 = jnp.where(row == 0, p2, jnp.where(row == 1, p1, pltpu.roll(a, 2, 0)))
        a_o[...] = a[tm - SUBLANES:, :]
        n_parts = FFN_ROW_PARTS
    for part in range(n_parts):
        rs = slice(part * tm // n_parts, (part + 1) * tm // n_parts)
        u = _dot(hn[rs], wu_ref[...])
        ac = cb_ref[...] + a2[rs] * cw_ref[0:1, :] + a1[rs] * cw_ref[1:2, :] + a[rs] * cw_ref[2:3, :]
        act = (jax.nn.silu(ac) * u).astype(bf16)
        x_o[rs, :] = x_ref[rs, :] + _dot(act, wd_ref[...])


def _ffn(hn, prev, x, lw, tm, tiles_per_seq, decode):
    m = x.shape[0]
    nt = m // tm
    row = lambda n: pl.BlockSpec((tm, n), lambda i: (i, 0))
    const = lambda a: pl.BlockSpec(a.shape, lambda i: (0,) * a.ndim, pipeline_mode=pl.Buffered(1))
    if decode:
        prev_spec = pl.BlockSpec(prev.shape, lambda i: (0, 0, 0))
        a_spec = row(D_FF)
        a_shape = jax.ShapeDtypeStruct((m, D_FF), f32)
    else:
        per = tm // HALO
        prev_spec = pl.BlockSpec((HALO, D_MODEL), lambda i: (jnp.maximum(i * per - 1, 0), 0))
        a_spec = pl.BlockSpec((None, SUBLANES, D_FF), lambda i: (i, 0, 0))
        a_shape = jax.ShapeDtypeStruct((nt, SUBLANES, D_FF), f32)
    ws = (lw['w_up_a'], lw['w_up_u'], lw['conv_w'], lw['conv_b'], lw['w_down'])
    return pl.pallas_call(
        functools.partial(_ffn_kernel, tiles_per_seq, decode),
        grid=(nt,),
        in_specs=[row(D_MODEL), prev_spec, row(D_MODEL)] + [const(a) for a in ws],
        out_specs=[row(D_MODEL), a_spec],
        out_shape=[jax.ShapeDtypeStruct((m, D_MODEL), f32), a_shape],
        compiler_params=_cparams(("parallel",)),
        name="ffn_decode" if decode else "ffn_prompt",
    )(hn, prev, x, *ws)


def _rel_bucket_np(d):
    d = np.maximum(d, 0)
    exact = REL_BUCKETS // 2
    lg = np.log(np.maximum(d, 1).astype(np.float32) / np.float32(exact)) / np.float32(math.log(REL_MAX_DIST / exact))
    large = np.minimum(exact + (lg * np.float32(REL_BUCKETS - exact)).astype(np.int32), REL_BUCKETS - 1)
    return np.where(d < exact, d, large).astype(np.int32)


def _bias_lookup(rel_bias, dist):
    bucket = _rel_bucket_np(np.asarray(dist)).reshape(-1)
    onehot = jnp.asarray((bucket[:, None] == np.arange(REL_BUCKETS)[None, :]).astype(np.int8)).astype(f32)
    out = jnp.dot(onehot, rel_bias, precision=lax.Precision.HIGHEST)
    return out.reshape(tuple(np.shape(dist)) + (N_HEADS,))


def _bias_tables(rel_bias, seq, past_len):
    c = np.arange(QT)[:, None]
    r = np.arange(QT)[None, :]

    def key_major(t):
        rows = t.shape[0]
        return t.reshape(rows, QT, N_KV, GQA).transpose(2, 0, 3, 1).reshape(N_KV, rows, GQA * QT)

    def masked(t, keep):
        return jnp.where(jnp.asarray(np.tile(keep, (1, GQA)))[None], t, NEG)

    far = jnp.broadcast_to(rel_bias[REL_BUCKETS - 1].reshape(N_KV, 1, GQA, 1), (N_KV, QT, GQA, QT))
    far = far.reshape(N_KV, QT, GQA * QT)
    td = masked(key_major(_bias_lookup(rel_bias, r - c)) - far, c <= r)
    ts = key_major(_bias_lookup(rel_bias, QT + r - c)) - far
    we = masked(jnp.zeros_like(far), c > r)
    j = np.arange(2 * QT)[:, None]
    dist_c = r - (CMP_STRIDE * (j - QT) + CMP_LEN - 1)
    bc = masked(key_major(_bias_lookup(rel_bias, dist_c)), dist_c >= 0)
    cc = np.arange(LANES)[:, None]
    n = np.arange(LANES)[None, :]
    ov = ((cc * CMP_STRIDE < n * SEL_BLOCK + SEL_BLOCK) & (cc * CMP_STRIDE + CMP_LEN > n * SEL_BLOCK) & (cc < N_CMP))
    ovt = jnp.asarray((ov & (n < seq // SEL_BLOCK)).T, dtype=bf16)
    ov_s = jnp.asarray(ov & (n < past_len // SEL_BLOCK + 1), dtype=bf16)
    eet = jnp.asarray((np.arange(seq)[:, None] // SEL_BLOCK) == np.arange(LANES)[None, :], dtype=bf16)

    def look_rows(dist):
        return _bias_lookup(rel_bias, dist).T

    bc_s = look_rows(past_len - (np.arange(LANES) * CMP_STRIDE + CMP_LEN - 1))
    bs_s = look_rows(past_len - np.arange(past_len))
    bw_s = look_rows(WINDOW - np.arange(WINDOW))
    b0_s = look_rows(np.zeros((LANES,), np.int64))
    return dict(td=td, ts=ts, we=we, bc=bc, ovt=ovt, eet=eet,
                ov_s=ov_s, bc_s=bc_s, bs_s=bs_s, bw_s=bw_s, b0_s=b0_s)


def _pair_order(w, axis):
    shape = w.shape
    w = w.reshape(shape[:axis] + (N_KV, GQA, HEAD_DIM) + shape[axis + 1:])
    return jnp.swapaxes(w, axis, axis + 1).reshape(shape)


def _repack_w_in(w):
    z = lambda n: jnp.zeros((w.shape[0], n), w.dtype)
    return jnp.concatenate([
        _pair_order(w[:, :512], 1), w[:, 512:1280],
        w[:, 1280:1304], z(LANES - 24),
        w[:, 1304:2072],
        w[:, 2072:2088], z(LANES - GLA_RANK),
        w[:, 2088:2344]], axis=1)


def _block_diag(blocks):
    n, r, c = blocks.shape
    eye = jnp.eye(n, dtype=blocks.dtype)
    return (eye[:, None, :, None] * blocks[:, :, None, :]).reshape(n * r, n * c)


def _layer_weights(l, p):
    tile2 = lambda v: jnp.tile(v, 2)[None, :]
    w_in = _repack_w_in(p['w_in'][l]).astype(bf16)
    w_out = jnp.concatenate([_pair_order(p['w_out'][l][:512], 0), p['w_out'][l][512:]], axis=0).astype(bf16)
    k_gain = jnp.concatenate([jnp.tile(p['k_gain'][l], (1, 2)), jnp.zeros((SUBLANES - 3, LANES), f32)], axis=0)
    w_alpha = jnp.concatenate([p['gla_w_alpha'][l], jnp.zeros((LANES - GLA_RANK, LANES), f32)], axis=0).astype(bf16)

    def cmp_w(kv):
        w1 = p['cmp_w1'][l, kv]
        two = jax.vmap(lambda w: _block_diag(jnp.stack([w, w])))(w1)
        wcat = jnp.concatenate([two[:CMP_STRIDE], two[CMP_STRIDE:]], axis=-1).astype(bf16)
        pe = jnp.tile(p['cmp_pe'][l, kv], (1, 2))
        pe = jnp.concatenate([jnp.broadcast_to(pe[:CMP_STRIDE, None, :], (CMP_STRIDE, SUBLANES, LANES)),
                              jnp.broadcast_to(pe[CMP_STRIDE:, None, :], (CMP_STRIDE, SUBLANES, LANES))], axis=1)
        w2 = _block_diag(jnp.stack([p['cmp_w2'][l, kv]] * 2)).astype(bf16)
        wcat = wcat.reshape(CMP_STRIDE * LANES, 4 * LANES)
        pe = pe.transpose(1, 0, 2).reshape(2 * SUBLANES, CMP_STRIDE * LANES)
        return wcat, pe, w2

    wk, pek, w2k = cmp_w(0)
    wv, pev, w2v = cmp_w(1)
    w_up = p['w_ffn_up'][l].astype(bf16)
    conv_w = jnp.concatenate([p['ffn_conv_w'][l], jnp.zeros((SUBLANES - 3, D_FF), f32)], axis=0)
    return dict(
        g_mix=p['g_mix'][l][None, :], w_in=w_in, q_gain=tile2(p['q_gain'][l]), k_gain=k_gain,
        w_alpha=w_alpha, b_alpha=p['gla_b_alpha'][l][None, :],
        cmp_wk=wk, cmp_wv=wv, cmp_pek=pek, cmp_pev=pev, cmp_w2k=w2k, cmp_w2v=w2v,
        gla_norm=p['gla_norm'][l][None, :],
        pool_w=_block_diag(p['pool_w'][l]).astype(bf16), pool_scale=p['pool_scale'][l][None, :],
        w_out=w_out,
        g_ffn=p['g_ffn'][l][None, :],
        w_up_a=w_up[:, :D_FF], w_up_u=w_up[:, D_FF:], conv_w=conv_w, conv_b=p['ffn_conv_b'][l][None, :],
        w_down=p['w_ffn_down'][l].astype(bf16),
    )


def _prompt_layer(x, lw, tabs, nb, seq):
    tm = 512
    (q, ckv, skv, wkv, gt, gq, gk, gv, gr, gl, pu, skv_b, wkv_b) = _in_proj(x, lw, tm)
    kc, vc = _compress_prompt(ckv, lw, nb, seq)
    o_nsa = _nsa_prompt(q, kc, vc, skv_b, wkv_b, gt, tabs, nb, seq)
    o_gla, st = _gla_prompt(gq, gk, gl, gv, gr, lw['gla_norm'], nb, seq)
    o_pool = _pool_prompt(pu, lw['pool_w'], lw['pool_scale'], nb, seq)
    x_mid, hn = _out_proj(o_nsa, o_gla, o_pool, x, lw['w_out'], lw['g_ffn'], tm)
    x_out, a_tail = _ffn(hn, hn, x_mid, lw, tm, seq // tm, False)
    win_tail = wkv.reshape(nb, seq, 2 * LANES)[:, seq - WINDOW:]
    st = st.reshape(nb, GLA_HEADS, GLA_DV, GLA_HEADS, GLA_DK)
    gla_state = jnp.stack([st[:, h, :, h, :] for h in range(GLA_HEADS)], axis=1).transpose(0, 1, 3, 2)
    pool_state = pu.reshape(nb, seq, 2 * LANES)[:, seq - POOL_BUF:]
    a_tail = a_tail.reshape(nb, seq // tm, SUBLANES, D_FF)
    conv_state = a_tail[:, -1, SUBLANES - 2:, :]
    return x_out, (ckv, skv, win_tail, gla_state, pool_state, conv_state)


def _sample_layer(l, x, lw, tabs, page_table, cache_cmp, cache_slc, win_all, state_gla, state_pool, state_conv):
    nb = x.shape[0]
    (q, ckv, skv, wkv, gt, gq, gk, gv, gr, gl, pu, _, _) = _in_proj(x, lw, nb)
    lane_hi = (np.arange(LANES) >= HEAD_DIM)
    own = jnp.asarray((lane_hi[None, :] == (np.arange(N_KV)[:, None] == 1)).astype(np.float32))
    qrows = (q.astype(f32).reshape(nb, 1, GQA, LANES) * own[None, :, None, :]).reshape(nb, N_HEADS, LANES)
    grows = jnp.pad(gt[:, :N_HEADS * 3].reshape(nb, N_HEADS, 3), ((0, 0), (0, 0), (0, LANES - 3)))
    knew = jnp.pad(jnp.concatenate([skv, wkv], axis=1).reshape(nb, 4, LANES), ((0, 0), (0, SUBLANES - 4), (0, 0)))
    o, win_next = _nsa_sample(l, page_table, cache_cmp, cache_slc, win_all, qrows, grows, knew, lw, tabs)
    o = o.reshape(nb, N_KV, GQA, N_KV, HEAD_DIM)
    o_nsa = jnp.stack([o[:, h, :, h, :] for h in range(N_KV)], axis=2).reshape(nb, GQA * LANES).astype(bf16)
    cols = jnp.stack([gq, gk, gl], axis=-1)
    o_gla, s_new = _gla_sample(cols, gv.reshape(nb, GLA_HEADS, GLA_DV),
                               state_gla[l].reshape(nb, GLA_HEADS * GLA_DK, GLA_DV),
                               gr.reshape(nb, GLA_HEADS, GLA_DV), lw['gla_norm'].reshape(GLA_HEADS, GLA_DV))
    o_gla = o_gla.reshape(nb, GLA_HEADS * GLA_DV).astype(bf16)
    o_pool = _pool_sample(pu, state_pool[l].transpose(1, 0, 2), lw['pool_w'], lw['pool_scale'])
    x_mid, hn = _out_proj(o_nsa, o_gla, o_pool, x, lw['w_out'], lw['g_ffn'], nb)
    x_out, a = _ffn(hn, state_conv[l].transpose(1, 0, 2), x_mid, lw, nb, 1, True)
    gla_state = s_new.reshape(nb, GLA_HEADS, GLA_DK, GLA_DV)
    return x_out, (ckv, skv, win_next, gla_state, pu, a)


def kernel(x_prompt, x_sample, cache_cmp_kv, cache_slc_kv, page_table, state_win_kv, state_gla, state_pool,
           state_ffn_conv, rel_bias, g_mix, w_in, q_gain, k_gain, cmp_pe, cmp_w1, cmp_w2, gla_w_alpha, gla_b_alpha,
           gla_norm, pool_w, pool_scale, w_out, g_ffn, w_ffn_up, ffn_conv_w, ffn_conv_b, w_ffn_down):
    nb, seq, _ = x_prompt.shape
    db = x_sample.shape[0]
    depth = w_in.shape[0]
    n_phys = cache_cmp_kv.shape[1]
    past_len = page_table.shape[1] * cache_cmp_kv.shape[2]
    assert (seq, past_len, page_table.shape[1], cache_cmp_kv.shape[2]) == (2048, 2048, N_PAGES, PAGE)
    assert state_win_kv.shape[2] == WINDOW and x_sample.shape[1] == 1
    params = dict(g_mix=g_mix, w_in=w_in, q_gain=q_gain, k_gain=k_gain, cmp_pe=cmp_pe, cmp_w1=cmp_w1, cmp_w2=cmp_w2,
                  gla_w_alpha=gla_w_alpha, gla_b_alpha=gla_b_alpha, gla_norm=gla_norm, pool_w=pool_w,
                  pool_scale=pool_scale, w_out=w_out, g_ffn=g_ffn, w_ffn_up=w_ffn_up, ffn_conv_w=ffn_conv_w,
                  ffn_conv_b=ffn_conv_b, w_ffn_down=w_ffn_down)
    tabs = _bias_tables(rel_bias, seq, past_len)
    cache_cmp = cache_cmp_kv.transpose(0, 1, 3, 4, 5, 2).reshape(depth, n_phys, 2, LANES, PAGE)
    cache_slc = cache_slc_kv.transpose(0, 1, 3, 4, 5, 2).reshape(depth, n_phys, 2, LANES, PAGE)
    win_all = state_win_kv.transpose(0, 1, 3, 4, 5, 2).reshape(depth, db, 2, LANES, WINDOW)
    xp = x_prompt.reshape(nb * seq, D_MODEL)
    xs = x_sample.reshape(db, D_MODEL)
    st_p = [[] for _ in range(6)]
    st_s = [[] for _ in range(6)]
    for l in range(depth):
        lw = _layer_weights(l, params)
        xp, new_p = _prompt_layer(xp, lw, tabs, nb, seq)
        xs, new_s = _sample_layer(l, xs, lw, tabs, page_table, cache_cmp, cache_slc, win_all, state_gla,
                                  state_pool, state_ffn_conv)
        for i in range(6):
            st_p[i].append(new_p[i])
            st_s[i].append(new_s[i])
    stk = lambda a: jnp.stack(a, axis=0)
    kv = lambda a, rows: a.reshape(depth, a.shape[1] // rows, rows, 2, N_KV, HEAD_DIM)
    outs_p = (kv(stk(st_p[0]), seq), kv(stk(st_p[1]), seq), kv(stk(st_p[2]).reshape(depth, nb * WINDOW, -1), WINDOW),
              stk(st_p[3]), stk(st_p[4]), stk(st_p[5]))
    shift_in = lambda old, new: jnp.concatenate([old[:, :, 1:], stk(new)[:, :, None, :]], axis=2)
    win_s = stk(st_s[2]).reshape(depth, db, 2, N_KV, HEAD_DIM, WINDOW).transpose(0, 1, 5, 2, 3, 4)
    outs_s = (kv(stk(st_s[0]), 1), kv(stk(st_s[1]), 1), win_s, stk(st_s[3]),
              shift_in(state_pool, st_s[4]), shift_in(state_ffn_conv, st_s[5]))
    return (xp.reshape(nb, seq, D_MODEL), xs.reshape(db, 1, D_MODEL), *outs_p, *outs_s)
```
